```python
import math
import jax, jax.numpy as jnp
from jax import lax
import numpy as np

D_MODEL = 1024
BATCH = 2
SEQ = 8192
DEPTH = 1
DEC_BATCH = 16
DEC_SEQ = 32
PAST_LEN = 2048

CHUNK = 64
EPS = 1e-6
DN_HEADS = 8
DN_DK = 128
DN_DV = 128
DN_QK = DN_HEADS * DN_DK
DN_V = DN_HEADS * DN_DV
DN_CONV_CH = 2 * DN_QK + DN_V
CONV_W = 4
MLA_HEADS = 8
QK_NOPE = 128
QK_ROPE = 64
QK_HEAD = QK_NOPE + QK_ROPE
V_HEAD = 128
KV_RANK = 512
MLA_Q = MLA_HEADS * QK_HEAD
MLA_V = MLA_HEADS * V_HEAD
ROPE_THETA = 10000.0
Q_BLOCK = 128
IN_WIDTHS = (DN_CONV_CH, DN_V, DN_HEADS, DN_HEADS, MLA_Q, KV_RANK, QK_ROPE, MLA_V, D_MODEL, D_MODEL)
IN_WIDTH = DN_CONV_CH + DN_V + 2 * DN_HEADS + MLA_Q + KV_RANK + QK_ROPE + MLA_V + 2 * D_MODEL

kernel_name = 'hybrid_gdn_mla_stream_step'


def _rms(x, gain):
    xf = x.astype(jnp.float32)
    y = xf * lax.rsqrt(jnp.mean(xf * xf, axis=-1, keepdims=True) + EPS)
    return (y * gain.astype(jnp.float32)).astype(x.dtype)


def _l2norm(x):
    return x * lax.rsqrt(jnp.sum(x * x, axis=-1, keepdims=True) + EPS)


def _rope(x, pos):
    half = QK_ROPE // 2
    inv = ROPE_THETA ** (-jnp.arange(half, dtype=jnp.float32) / half)
    ang = pos.astype(jnp.float32)[:, None] * inv
    ang = ang.reshape(ang.shape[0], *([1] * (x.ndim - 3)), half)
    cos, sin = jnp.cos(ang), jnp.sin(ang)
    xf = x.astype(jnp.float32)
    x1, x2 = xf[..., :half], xf[..., half:]
    return jnp.concatenate([x1 * cos - x2 * sin, x2 * cos + x1 * sin], axis=-1).astype(x.dtype)


def _split_in(proj):
    idx, s = [], 0
    for w in IN_WIDTHS[:-1]:
        s += w
        idx.append(s)
    return jnp.split(proj, idx, axis=-1)


def _short_conv(u, buf, w_conv):
    full = jnp.concatenate([buf, u], axis=1)
    T = u.shape[1]
    out = full[:, 0:T] * w_conv[0]
    for i in range(1, CONV_W):
        out = out + full[:, i:i + T] * w_conv[i]
    return jax.nn.silu(out), full[:, -(CONV_W - 1):]


def _gated_delta(q, k, v, g, beta, s0):
    B, T, H, _ = q.shape
    C = min(CHUNK, T)
    N = T // C
    qc, kc, vc = [jnp.moveaxis(a.reshape(B, N, C, H, a.shape[-1]), 3, 2) for a in (q, k, v)]
    gc, bc = [jnp.moveaxis(a.reshape(B, N, C, H), 3, 2) for a in (g, beta)]
    gcum = jnp.cumsum(gc, axis=-1)
    tri_incl = jnp.tril(jnp.ones((C, C), bool))
    tri_strict = jnp.tril(jnp.ones((C, C), bool), -1)
    decay = jnp.exp(jnp.where(tri_incl, gcum[..., :, None] - gcum[..., None, :], -jnp.inf))
    kb = kc * bc[..., None]
    L = jnp.where(tri_strict, jnp.einsum('bnhid,bnhjd->bnhij', kb, kc) * decay, 0.0)
    eye = jnp.eye(C, dtype=q.dtype)
    tinv = lax.linalg.triangular_solve(eye + L, jnp.broadcast_to(eye, L.shape), left_side=True, lower=True)
    eg = jnp.exp(gcum)
    w = tinv @ (kb * eg[..., None])
    u = tinv @ (vc * bc[..., None])
    attn = jnp.einsum('bnhid,bnhjd->bnhij', qc, kc) * decay
    qg = qc * eg[..., None]
    kd = kc * jnp.exp(gcum[..., -1:] - gcum)[..., None]
    glast = jnp.exp(gcum[..., -1])

    def step(S, xs):
        w_, u_, qg_, attn_, kd_, gl_ = xs
        vnew = u_ - w_ @ S
        o = qg_ @ S + attn_ @ vnew
        S = S * gl_[..., None, None] + jnp.swapaxes(kd_, -1, -2) @ vnew
        return S, o

    xs = tuple(jnp.moveaxis(a, 1, 0) for a in (w, u, qg, attn, kd, glast))
    S, o = lax.scan(step, s0, xs)
    o = jnp.moveaxis(jnp.moveaxis(o, 0, 1), 2, 3).reshape(B, T, H, v.shape[-1])
    return o, S


def _deltanet(qkv_raw, beta_raw, alpha_raw, conv_buf, s0, p):
    qkv, conv_new = _short_conv(qkv_raw, conv_buf, p['w_conv'])
    B, T, _ = qkv.shape
    f32 = jnp.float32
    q, k, v = jnp.split(qkv.astype(f32), [DN_QK, 2 * DN_QK], axis=-1)
    q = _l2norm(q.reshape(B, T, DN_HEADS, DN_DK)) * (DN_DK ** -0.5)
    k = _l2norm(k.reshape(B, T, DN_HEADS, DN_DK))
    v = v.reshape(B, T, DN_HEADS, DN_DV)
    beta = jax.nn.sigmoid(beta_raw.astype(f32))
    g = -jnp.exp(p['a_log'].astype(f32)) * jax.nn.softplus(alpha_raw.astype(f32) + p['dt_bias'].astype(f32))
    o, S = _gated_delta(q, k, v, g, beta, s0.astype(f32))
    o = _rms(o, p['dn_out_norm'])
    return o.reshape(B, T, DN_V).astype(qkv_raw.dtype), conv_new, S.astype(qkv_raw.dtype)


def _attend(qn, qr, qpos, kn, kr, v, kpos):
    s = jnp.einsum('bqhd,bkhd->bhqk', qn, kn) + jnp.einsum('bqhd,bkd->bhqk', qr, kr)
    s = s.astype(jnp.float32) * (QK_HEAD ** -0.5)
    mask = (kpos[None, :] // CHUNK) <= (qpos[:, None] // CHUNK)
    s = jnp.where(mask, s, -jnp.inf)
    pr = jax.nn.softmax(s, axis=-1).astype(v.dtype)
    return jnp.einsum('bhqk,bkhd->bqhd', pr, v)


def _mla(q_raw, ckv_raw, kr_raw, pos, past_ckv, past_kr, p):
    B, T, _ = q_raw.shape
    q = q_raw.reshape(B, T, MLA_HEADS, QK_HEAD)
    q_nope = _rms(q[..., :QK_NOPE], p['q_nope_norm'])
    q_rope = _rope(_rms(q[..., QK_NOPE:], p['q_rope_norm']), pos)
    ckv_new = _rms(ckv_raw, p['kv_norm'])
    kr_new = _rope(_rms(kr_raw, p['k_rope_norm']), pos)
    if past_ckv is None:
        ckv_all, kr_all, kpos = ckv_new, kr_new, pos
    else:
        ckv_all = jnp.concatenate([past_ckv, ckv_new], axis=1)
        kr_all = jnp.concatenate([past_kr, kr_new], axis=1)
        kpos = jnp.concatenate([jnp.arange(past_ckv.shape[1]), pos])
    Tk = ckv_all.shape[1]
    k_nope = _rms((ckv_all @ p['w_uk']).reshape(B, Tk, MLA_HEADS, QK_NOPE), p['k_nope_norm'])
    v = (ckv_all @ p['w_uv']).reshape(B, Tk, MLA_HEADS, V_HEAD)
    qb = Q_BLOCK if T % Q_BLOCK == 0 else T
    nb = T // qb
    qn_b = jnp.moveaxis(q_nope.reshape(B, nb, qb, MLA_HEADS, QK_NOPE), 1, 0)
    qr_b = jnp.moveaxis(q_rope.reshape(B, nb, qb, MLA_HEADS, QK_ROPE), 1, 0)
    qp_b = pos.reshape(nb, qb)
    o = lax.map(lambda a: _attend(a[0], a[1], a[2], k_nope, kr_all, v, kpos), (qn_b, qr_b, qp_b))
    o = jnp.moveaxis(o, 0, 1).reshape(B, T, MLA_V)
    return o, ckv_new, kr_new


def _layer(x, c, pos, conv_buf, s0, past_ckv, past_kr, p):
    B = x.shape[0]
    mod = c @ p['w_ada'] + p['b_ada']
    shift, scale, gate = jnp.split(mod, 3, axis=-1)
    h = _rms(x, p['norm_gain']) * (1 + scale[:, None, :]) + shift[:, None, :]
    (qkv_raw, z_a, beta_raw, alpha_raw, q_raw, ckv_raw, kr_raw, z_b, g_a, g_b) = _split_in(h @ p['w_in'])
    if conv_buf is None:
        conv_buf = jnp.zeros((B, CONV_W - 1, DN_CONV_CH), x.dtype)
    if s0 is None:
        s0 = jnp.zeros((B, DN_HEADS, DN_DK, DN_DV), jnp.float32)
    o_a, conv_new, s_new = _deltanet(qkv_raw, beta_raw, alpha_raw, conv_buf, s0, p)
    o_b, ckv_new, kr_new = _mla(q_raw, ckv_raw, kr_raw, pos, past_ckv, past_kr, p)
    u_a = o_a * jax.nn.silu(z_a)
    u_b = o_b * jax.nn.silu(z_b)
    merged = jax.nn.sigmoid(g_a) * (u_a @ p['w_o_dn']) + jax.nn.sigmoid(g_b) * (u_b @ p['w_o_mla'])
    y = x + gate[:, None, :] * (merged @ p['w_out'])
    return y, conv_new, s_new, ckv_new, kr_new


def setup_inputs(seed: int = 0) -> dict:
    key = jax.random.key(seed)
    ks = jax.random.split(key, 32)
    f32 = jnp.float32

    def nrm(k, shape, scale):
        return jax.random.normal(k, shape, f32) * scale

    def gain(k, n):
        return 1.0 + 0.05 * jax.random.normal(k, (DEPTH, n), f32)

    dt = jnp.exp(jax.random.uniform(ks[12], (DEPTH, DN_HEADS), f32, math.log(1e-3), math.log(1e-1)))
    return {
        'x_prompt': nrm(ks[0], (BATCH, SEQ, D_MODEL), 1.0),
        'x_sample': nrm(ks[1], (DEC_BATCH, DEC_SEQ, D_MODEL), 1.0),
        'c_prompt': nrm(ks[2], (BATCH, D_MODEL), 1.0),
        'c_sample': nrm(ks[3], (DEC_BATCH, D_MODEL), 1.0),
        'cache_ckv': nrm(ks[4], (DEPTH, DEC_BATCH, PAST_LEN, KV_RANK), 1.0),
        'cache_krope': nrm(ks[5], (DEPTH, DEC_BATCH, PAST_LEN, QK_ROPE), 1.0),
        'state_delta': nrm(ks[6], (DEPTH, DEC_BATCH, DN_HEADS, DN_DK, DN_DV), 0.1),
        'state_conv': nrm(ks[7], (DEPTH, DEC_BATCH, CONV_W - 1, DN_CONV_CH), 1.0),
        'norm_gain': gain(ks[8], D_MODEL),
        'w_ada': nrm(ks[9], (DEPTH, D_MODEL, 3 * D_MODEL), 0.5 * D_MODEL ** -0.5),
        'b_ada': nrm(ks[10], (DEPTH, 3 * D_MODEL), 0.01),
        'w_in': nrm(ks[11], (DEPTH, D_MODEL, IN_WIDTH), D_MODEL ** -0.5),
        'w_conv': nrm(ks[13], (DEPTH, CONV_W, DN_CONV_CH), CONV_W ** -0.5),
        'a_log': jnp.log(jax.random.uniform(ks[14], (DEPTH, DN_HEADS), f32, 1.0, 16.0)),
        'dt_bias': dt + jnp.log(-jnp.expm1(-dt)),
        'dn_out_norm': gain(ks[15], DN_DV),
        'q_nope_norm': gain(ks[16], QK_NOPE),
        'q_rope_norm': gain(ks[17], QK_ROPE),
        'k_nope_norm': gain(ks[18], QK_NOPE),
        'k_rope_norm': gain(ks[19], QK_ROPE),
        'kv_norm': gain(ks[20], KV_RANK),
        'w_uk': nrm(ks[21], (DEPTH, KV_RANK, MLA_HEADS * QK_NOPE), KV_RANK ** -0.5),
        'w_uv': nrm(ks[22], (DEPTH, KV_RANK, MLA_V), KV_RANK ** -0.5),
        'w_o_dn': nrm(ks[23], (DEPTH, DN_V, D_MODEL), DN_V ** -0.5),
        'w_o_mla': nrm(ks[24], (DEPTH, MLA_V, D_MODEL), MLA_V ** -0.5),
        'w_out': nrm(ks[25], (DEPTH, D_MODEL, D_MODEL), D_MODEL ** -0.5),
    }


def reference(x_prompt, x_sample, c_prompt, c_sample, cache_ckv, cache_krope, state_delta, state_conv,
              norm_gain, w_ada, b_ada, w_in, w_conv, a_log, dt_bias, dn_out_norm,
              q_nope_norm, q_rope_norm, k_nope_norm, k_rope_norm, kv_norm,
              w_uk, w_uv, w_o_dn, w_o_mla, w_out):
    pos_p = jnp.arange(x_prompt.shape[1])
    pos_s = cache_ckv.shape[2] + jnp.arange(x_sample.shape[1])
    yp, ys = x_prompt, x_sample
    ckv_p, kr_p, sd_p, cv_p = [], [], [], []
    ckv_s, kr_s, sd_s, cv_s = [], [], [], []
    for l in range(DEPTH):
        p = dict(norm_gain=norm_gain[l], w_ada=w_ada[l], b_ada=b_ada[l], w_in=w_in[l], w_conv=w_conv[l],
                 a_log=a_log[l], dt_bias=dt_bias[l], dn_out_norm=dn_out_norm[l],
                 q_nope_norm=q_nope_norm[l], q_rope_norm=q_rope_norm[l], k_nope_norm=k_nope_norm[l],
                 k_rope_norm=k_rope_norm[l], kv_norm=kv_norm[l], w_uk=w_uk[l], w_uv=w_uv[l],
                 w_o_dn=w_o_dn[l], w_o_mla=w_o_mla[l], w_out=w_out[l])
        yp, cvp, sdp, kvp, krp = _layer(yp, c_prompt, pos_p, None, None, None, None, p)
        ys, cvs, sds, kvs, krs = _layer(ys, c_sample, pos_s, state_conv[l], state_delta[l],
                                        cache_ckv[l], cache_krope[l], p)
        ckv_p.append(kvp); kr_p.append(krp); sd_p.append(sdp); cv_p.append(cvp)
        ckv_s.append(kvs); kr_s.append(krs); sd_s.append(sds); cv_s.append(cvs)
    new_ckv_p, new_kr_p = jnp.stack(ckv_p), jnp.stack(kr_p)
    new_sd_p, new_cv_p = jnp.stack(sd_p), jnp.stack(cv_p)
    new_ckv_s, new_kr_s = jnp.stack(ckv_s), jnp.stack(kr_s)
    new_sd_s, new_cv_s = jnp.stack(sd_s), jnp.stack(cv_s)
    return (yp, ys, new_ckv_p, new_kr_p, new_sd_p, new_cv_p, new_ckv_s, new_kr_s, new_sd_s, new_cv_s)
```

```python
import functools
import math

import jax
import jax.numpy as jnp
from jax import lax
from jax.experimental import pallas as pl
from jax.experimental.pallas import tpu as pltpu

D_MODEL = 1024
CHUNK = 64
EPS = 1e-6
DN_HEADS = 8
DN_DK = 128
DN_DV = 128
DN_QK = DN_HEADS * DN_DK
DN_V = DN_HEADS * DN_DV
DN_CONV_CH = 2 * DN_QK + DN_V
CONV_W = 4
MLA_HEADS = 8
QK_NOPE = 128
QK_ROPE = 64
QK_HEAD = QK_NOPE + QK_ROPE
V_HEAD = 128
KV_RANK = 512
MLA_Q = MLA_HEADS * QK_HEAD
MLA_V = MLA_HEADS * V_HEAD
ROPE_THETA = 10000.0

LANES = 128
QK_CAT = 256
KR_OFF = 0
BETA_OFF = QK_ROPE
ALPHA_OFF = QK_ROPE + DN_HEADS
VMEM_LIMIT = 56 * 1024 * 1024

F32 = jnp.float32
BF16 = jnp.bfloat16
HI = lax.Precision.HIGHEST


def _dot(a, b):
    return jnp.dot(a, b, preferred_element_type=F32)


def _dot_nt(a, b, precision=None):
    return lax.dot_general(a, b, (((1,), (1,)), ((), ())), preferred_element_type=F32, precision=precision)


def _dot_tn(a, b):
    return lax.dot_general(a, b, (((0,), (0,)), ((), ())), preferred_element_type=F32)


def _cparams(sem):
    return pltpu.CompilerParams(dimension_semantics=sem, vmem_limit_bytes=VMEM_LIMIT)


def _resident(shape):
    nd = len(shape)
    return pl.BlockSpec(shape, lambda *_: (0,) * nd, pipeline_mode=pl.Buffered(1))


def _ada_kernel(c_ref, w_ref, b_ref, o_ref):
    o_ref[...] = jnp.dot(c_ref[...], w_ref[...], preferred_element_type=F32, precision=HI) + b_ref[...]


def _ada(c_all, w_ada, b_ada):
    rows = c_all.shape[0]
    tn = 512
    return pl.pallas_call(
        _ada_kernel,
        grid=(3 * D_MODEL // tn,),
        in_specs=[pl.BlockSpec((rows, D_MODEL), lambda j: (0, 0)),
                  pl.BlockSpec((D_MODEL, tn), lambda j: (0, j)),
                  pl.BlockSpec((1, tn), lambda j: (0, j))],
        out_specs=pl.BlockSpec((rows, tn), lambda j: (0, j)),
        out_shape=jax.ShapeDtypeStruct((rows, 3 * D_MODEL), F32),
        compiler_params=_cparams(("arbitrary",)),
        name="ada",
    )(c_all, w_ada, b_ada)


_PROJ_GROUPS = (("qkv", DN_CONV_CH, BF16), ("z_a", DN_V, BF16), ("qn", MLA_HEADS * QK_NOPE, BF16),
                ("qr", MLA_HEADS * QK_ROPE, BF16), ("z_b", MLA_V, BF16), ("g_a", D_MODEL, BF16),
                ("g_b", D_MODEL, BF16), ("ckv", KV_RANK, F32), ("small", LANES, F32))
_PROJ_WIDTH = sum(w for _, w, _ in _PROJ_GROUPS)


def _pack_w_in(w_in):
    o = 0
    qkv = w_in[:, o:o + DN_CONV_CH]; o += DN_CONV_CH
    z_a = w_in[:, o:o + DN_V]; o += DN_V
    beta = w_in[:, o:o + DN_HEADS]; o += DN_HEADS
    alpha = w_in[:, o:o + DN_HEADS]; o += DN_HEADS
    q = w_in[:, o:o + MLA_Q].reshape(D_MODEL, MLA_HEADS, QK_HEAD); o += MLA_Q
    ckv = w_in[:, o:o + KV_RANK]; o += KV_RANK
    kr = w_in[:, o:o + QK_ROPE]; o += QK_ROPE
    z_b = w_in[:, o:o + MLA_V]; o += MLA_V
    g_a = w_in[:, o:o + D_MODEL]; o += D_MODEL
    g_b = w_in[:, o:o + D_MODEL]
    qn = q[:, :, :QK_NOPE].reshape(D_MODEL, MLA_HEADS * QK_NOPE)
    qr = q[:, :, QK_NOPE:].reshape(D_MODEL, MLA_HEADS * QK_ROPE)
    pad = jnp.zeros((D_MODEL, LANES - QK_ROPE - 2 * DN_HEADS), w_in.dtype)
    small = jnp.concatenate([kr, beta, alpha, pad], axis=1)
    return jnp.concatenate([qkv, z_a, qn, qr, z_b, g_a, g_b, ckv, small], axis=1).astype(BF16)


def _in_proj_kernel(x_ref, mod_ref, gain_ref, w_ref, *out_refs):
    bb, tm, d = x_ref.shape
    x = x_ref[...]
    ms = jnp.mean(x * x, axis=-1, keepdims=True)
    y = x * lax.rsqrt(ms + EPS) * gain_ref[...]
    shift = mod_ref[:, :, 0:d]
    scale = mod_ref[:, :, d:2 * d]
    h = (y * (1.0 + scale) + shift).astype(BF16).reshape(bb * tm, d)
    o = 0
    for (_, width, dtype), ref in zip(_PROJ_GROUPS, out_refs):
        ref[...] = _dot(h, w_ref[:, o:o + width]).astype(dtype).reshape(bb, tm, width)
        o += width


def _in_proj(x, mod3, gain, w_pack, bb, tm):
    B, T, _ = x.shape
    row = lambda b, t: (b, t, 0)
    return pl.pallas_call(
        _in_proj_kernel,
        grid=(B // bb, T // tm),
        in_specs=[pl.BlockSpec((bb, tm, D_MODEL), row),
                  pl.BlockSpec((bb, 1, 3 * D_MODEL), lambda b, t: (b, 0, 0)),
                  _resident((1, D_MODEL)),
                  _resident((D_MODEL, _PROJ_WIDTH))],
        out_specs=[pl.BlockSpec((bb, tm, w), row) for _, w, _ in _PROJ_GROUPS],
        out_shape=[jax.ShapeDtypeStruct((B, T, w), dt) for _, w, dt in _PROJ_GROUPS],
        compiler_params=_cparams(("arbitrary", "arbitrary")),
        name="in_proj",
    )(x, mod3, gain, w_pack)


def _softplus(x):
    return jnp.maximum(x, 0.0) + jnp.log(1.0 + jnp.exp(-jnp.abs(x)))


def _deltanet_kernel(qkv_ref, small_ref, za_ref, cs_ref, s0_ref, wconv_ref, alog_ref, dtb_ref, onorm_ref,
                     ua_ref, sfin_ref, cnew_ref, buf_ref, s_ref, *, C):
    n = pl.program_id(1)
    pad = 8

    @pl.when(n == 0)
    def _():
        buf_ref[0:pad, :] = cs_ref[0]
        s_ref[...] = s0_ref[0]

    buf_ref[pad:pad + C, :] = qkv_ref[0].astype(F32)
    conv = buf_ref[pad - 3:pad - 3 + C, :] * wconv_ref[0:1, :]
    for i in range(1, CONV_W):
        conv = conv + buf_ref[pad - 3 + i:pad - 3 + i + C, :] * wconv_ref[i:i + 1, :]
    qkv = conv * jax.nn.sigmoid(conv)
    hist = buf_ref[C:C + pad, :]
    buf_ref[0:pad, :] = hist

    @pl.when(n == pl.num_programs(1) - 1)
    def _():
        cnew_ref[0] = hist

    sm = small_ref[0]
    beta_all = jax.nn.sigmoid(sm)
    g_all = -jnp.exp(alog_ref[...]) * _softplus(sm + dtb_ref[...])
    ri = lax.broadcasted_iota(jnp.int32, (C, C), 0)
    ci = lax.broadcasted_iota(jnp.int32, (C, C), 1)
    tri_incl = ri >= ci
    tri_strict = ri > ci
    eye = (ri == ci).astype(F32)
    gcum = jnp.dot(tri_incl.astype(F32), g_all, preferred_element_type=F32, precision=HI)
    sel = (lax.broadcasted_iota(jnp.int32, (DN_HEADS, LANES), 1)
           == lax.broadcasted_iota(jnp.int32, (DN_HEADS, LANES), 0) + ALPHA_OFF).astype(F32)
    gcum_t = _dot_nt(sel, gcum, precision=HI)
    glast = gcum[C - 1:C, :]
    eg_all = jnp.exp(gcum)
    kdf_all = jnp.exp(glast - gcum)
    egl_all = jnp.exp(glast)

    pair_masks = []
    m = 1
    while m < C:
        pair_masks.append((ri // (2 * m) == ci // (2 * m)) & (ri // m != ci // m))
        m *= 2
    for h in range(DN_HEADS):
        lo, hi = h * DN_DK, (h + 1) * DN_DK
        qh = qkv[:, lo:hi]
        kh = qkv[:, DN_QK + lo:DN_QK + hi]
        vh = qkv[:, 2 * DN_QK + lo:2 * DN_QK + hi]
        qh = qh * lax.rsqrt(jnp.sum(qh * qh, axis=-1, keepdims=True) + EPS) * (DN_DK ** -0.5)
        kh = kh * lax.rsqrt(jnp.sum(kh * kh, axis=-1, keepdims=True) + EPS)
        bcol = beta_all[:, BETA_OFF + h:BETA_OFF + h + 1]
        gc = gcum[:, ALPHA_OFF + h:ALPHA_OFF + h + 1]
        eg = eg_all[:, ALPHA_OFF + h:ALPHA_OFF + h + 1]
        kdf = kdf_all[:, ALPHA_OFF + h:ALPHA_OFF + h + 1]
        egl = egl_all[:, ALPHA_OFF + h:ALPHA_OFF + h + 1]
        decay = jnp.exp(jnp.where(tri_incl, gc - gcum_t[h:h + 1, :], -1e30))
        kb = kh * bcol
        k16 = kh.astype(BF16)
        lmat = jnp.where(tri_strict, _dot_nt(kb.astype(BF16), k16) * decay, 0.0)
        pinv = eye - jnp.where(pair_masks[0], lmat, 0.0)
        for mask in pair_masks[1:]:
            p16 = pinv.astype(BF16)
            t = _dot(p16, jnp.where(mask, lmat, 0.0).astype(BF16))
            pinv = pinv - _dot(t.astype(BF16), p16)
        rhs = jnp.concatenate([kb * eg, vh * bcol], axis=1).astype(BF16)
        wu = _dot(pinv.astype(BF16), rhs)
        w = wu[:, :DN_DK]
        u = wu[:, DN_DK:]
        attn = _dot_nt(qh.astype(BF16), k16) * decay
        s = s_ref[h]
        s16 = s.astype(BF16)
        ws = _dot(jnp.concatenate([w, qh * eg], axis=0).astype(BF16), s16)
        vnew = u - ws[:C]
        v16 = vnew.astype(BF16)
        o = ws[C:] + _dot(attn.astype(BF16), v16)
        s_ref[h] = s * egl + _dot_tn((kh * kdf).astype(BF16), v16)
        o = o * lax.rsqrt(jnp.mean(o * o, axis=-1, keepdims=True) + EPS) * onorm_ref[...]
        z = za_ref[0, :, lo:hi].astype(F32)
        ua_ref[0, :, lo:hi] = (o * (z * jax.nn.sigmoid(z))).astype(BF16)

    @pl.when(n == pl.num_programs(1) - 1)
    def _():
        sfin_ref[0] = s_ref[...]


def _deltanet(qkv, small, z_a, conv_state8, s0, w_conv, alog_v, dtb_v, onorm):
    B, T, _ = qkv.shape
    C = min(CHUNK, T)
    chunk = lambda b, n: (b, n, 0)
    perb3 = lambda b, n: (b, 0, 0)
    return pl.pallas_call(
        functools.partial(_deltanet_kernel, C=C),
        grid=(B, T // C),
        in_specs=[pl.BlockSpec((1, C, DN_CONV_CH), chunk),
                  pl.BlockSpec((1, C, LANES), chunk),
                  pl.BlockSpec((1, C, DN_V), chunk),
                  pl.BlockSpec((1, 8, DN_CONV_CH), perb3),
                  pl.BlockSpec((1, DN_HEADS, DN_DK, DN_DV), lambda b, n: (b, 0, 0, 0)),
                  _resident((CONV_W, DN_CONV_CH)),
                  _resident((1, LANES)),
                  _resident((1, LANES)),
                  _resident((1, DN_DV))],
        out_specs=[pl.BlockSpec((1, C, DN_V), chunk),
                   pl.BlockSpec((1, DN_HEADS, DN_DK, DN_DV), lambda b, n: (b, 0, 0, 0)),
                   pl.BlockSpec((1, 8, DN_CONV_CH), perb3)],
        out_shape=[jax.ShapeDtypeStruct((B, T, DN_V), BF16),
                   jax.ShapeDtypeStruct((B, DN_HEADS, DN_DK, DN_DV), F32),
                   jax.ShapeDtypeStruct((B, 8, DN_CONV_CH), F32)],
        scratch_shapes=[pltpu.VMEM((C + 8, DN_CONV_CH), F32),
                        pltpu.VMEM((DN_HEADS, DN_DK, DN_DV), F32)],
        compiler_params=_cparams(("arbitrary", "arbitrary")),
        name="deltanet",
    )(qkv, small, z_a, conv_state8, s0, w_conv, alog_v, dtb_v, onorm)


def _head_rms(x, gain_row, width):
    outs = []
    for h in range(x.shape[1] // width):
        xh = x[:, h * width:(h + 1) * width]
        outs.append(xh * lax.rsqrt(jnp.mean(xh * xh, axis=-1, keepdims=True) + EPS) * gain_row)
    return outs


def _kv_up(ckv16, kr_pad16, wuk_ref, wuv_ref, knorm_ref, kcat_ref, v_ref, shape3):
    bb, tm = shape3
    kn = _head_rms(_dot(ckv16, wuk_ref[...]), knorm_ref[...], QK_NOPE)
    for h in range(MLA_HEADS):
        kcat_ref[:, :, h * QK_CAT:h * QK_CAT + QK_NOPE] = kn[h].astype(BF16).reshape(bb, tm, QK_NOPE)
        kcat_ref[:, :, h * QK_CAT + QK_NOPE:(h + 1) * QK_CAT] = kr_pad16.reshape(bb, tm, LANES)
    v_ref[...] = _dot(ckv16, wuv_ref[...]).astype(BF16).reshape(bb, tm, MLA_V)


def _rope_angles(rows, tm, t0, q_off, lanes):
    pos = (lax.broadcasted_iota(jnp.int32, (rows, lanes), 0) % tm + t0 + q_off).astype(F32)
    half = QK_ROPE // 2
    fidx = (lax.broadcasted_iota(jnp.int32, (rows, lanes), 1) % half).astype(F32)
    inv = jnp.exp(fidx * (-math.log(ROPE_THETA) / half))
    ang = pos * inv
    return jnp.cos(ang), jnp.sin(ang)


def _mla_prep_kernel(qn_ref, qr_ref, ckv_ref, small_ref, qng_ref, qrg_ref, kvg_ref, krg_ref, kng_ref,
                     wuk_ref, wuv_ref, qcat_ref, kcat_ref, v_ref, ckvn_ref, krn_ref, *, q_off):
    bb, tm, _ = qn_ref.shape
    rows = bb * tm
    t0 = pl.program_id(1) * tm
    scale = QK_HEAD ** -0.5
    cos, sin = _rope_angles(rows, tm, t0, q_off, LANES)
    lane = lax.broadcasted_iota(jnp.int32, (rows, LANES), 1)
    low_half = lane < QK_ROPE
    first = (lane % QK_ROPE) < (QK_ROPE // 2)

    def rope(y):
        rot = jnp.where(first, -pltpu.roll(y, LANES - QK_ROPE // 2, 1), pltpu.roll(y, QK_ROPE // 2, 1))
        return y * cos + rot * sin

    def rms64(x):
        xx = x * x
        s_lo = jnp.sum(jnp.where(low_half, xx, 0.0), axis=-1, keepdims=True)
        s_hi = jnp.sum(jnp.where(low_half, 0.0, xx), axis=-1, keepdims=True)
        return lax.rsqrt(jnp.where(low_half, s_lo, s_hi) * (1.0 / QK_ROPE) + EPS)

    qn = _head_rms(qn_ref[...].astype(F32).reshape(rows, MLA_HEADS * QK_NOPE), qng_ref[...] * scale, QK_NOPE)
    qr_all = qr_ref[...].astype(F32).reshape(rows, MLA_HEADS * QK_ROPE)
    for c in range(MLA_HEADS // 2):
        x = qr_all[:, c * LANES:(c + 1) * LANES]
        y = rope(x * rms64(x) * qrg_ref[...]) * scale
        even = jnp.where(low_half, y, 0.0)
        odd = jnp.where(low_half, pltpu.roll(y, QK_ROPE, 1), 0.0)
        for h, part in ((2 * c, even), (2 * c + 1, odd)):
            qcat_ref[:, :, h * QK_CAT:h * QK_CAT + QK_NOPE] = qn[h].astype(BF16).reshape(bb, tm, QK_NOPE)
            qcat_ref[:, :, h * QK_CAT + QK_NOPE:(h + 1) * QK_CAT] = part.astype(BF16).reshape(bb, tm, LANES)

    sm = small_ref[...].reshape(rows, LANES)
    kr = rope(sm * rms64(sm) * krg_ref[...])
    krn_ref[...] = kr[:, :QK_ROPE].reshape(bb, tm, QK_ROPE)
    kr_pad16 = jnp.where(low_half, kr, 0.0).astype(BF16)

    ckv = ckv_ref[...].reshape(rows, KV_RANK)
    ckvn = ckv * lax.rsqrt(jnp.mean(ckv * ckv, axis=-1, keepdims=True) + EPS) * kvg_ref[...]
    ckvn_ref[...] = ckvn.reshape(bb, tm, KV_RANK)
    _kv_up(ckvn.astype(BF16), kr_pad16, wuk_ref, wuv_ref, kng_ref, kcat_ref, v_ref, (bb, tm))


def _mla_prep(qn_raw, qr_raw, ckv_raw, small, qn_gain, qr_gain, kv_gain, kr_gain, kn_gain, w_uk16, w_uv16,
              bb, tm, q_off):
    B, T, _ = qn_raw.shape
    row = lambda b, t: (b, t, 0)
    widths_in = (MLA_HEADS * QK_NOPE, MLA_HEADS * QK_ROPE, KV_RANK, LANES)
    outs = ((MLA_HEADS * QK_CAT, BF16), (MLA_HEADS * QK_CAT, BF16), (MLA_V, BF16), (KV_RANK, F32), (QK_ROPE, F32))
    return pl.pallas_call(
        functools.partial(_mla_prep_kernel, q_off=q_off),
        grid=(B // bb, T // tm),
        in_specs=[pl.BlockSpec((bb, tm, w), row) for w in widths_in]
        + [_resident((1, QK_NOPE)), _resident((1, LANES)), _resident((1, KV_RANK)), _resident((1, LANES)),
           _resident((1, QK_NOPE)), _resident((KV_RANK, MLA_HEADS * QK_NOPE)), _resident((KV_RANK, MLA_V))],
        out_specs=[pl.BlockSpec((bb, tm, w), row) for w, _ in outs],
        out_shape=[jax.ShapeDtypeStruct((B, T, w), dt) for w, dt in outs],
        compiler_params=_cparams(("arbitrary", "arbitrary")),
        name="mla_prep",
    )(qn_raw, qr_raw, ckv_raw, small, qn_gain, qr_gain, kv_gain, kr_gain, kn_gain, w_uk16, w_uv16)


def _kv_up_kernel(ckv_ref, kr_ref, kng_ref, wuk_ref, wuv_ref, kcat_ref, v_ref):
    bb, tm, _ = ckv_ref.shape
    ckv16 = ckv_ref[...].reshape(bb * tm, KV_RANK).astype(BF16)
    kr_pad16 = kr_ref[...].reshape(bb * tm, LANES).astype(BF16)
    _kv_up(ckv16, kr_pad16, wuk_ref, wuv_ref, kng_ref, kcat_ref, v_ref, (bb, tm))


def _kv_up_cached(ckv, kr_pad, kn_gain, w_uk16, w_uv16, tm):
    B, T, _ = ckv.shape
    row = lambda b, t: (b, t, 0)
    return pl.pallas_call(
        _kv_up_kernel,
        grid=(B, T // tm),
        in_specs=[pl.BlockSpec((1, tm, KV_RANK), row), pl.BlockSpec((1, tm, LANES), row),
                  _resident((1, QK_NOPE)), _resident((KV_RANK, MLA_HEADS * QK_NOPE)),
                  _resident((KV_RANK, MLA_V))],
        out_specs=[pl.BlockSpec((1, tm, MLA_HEADS * QK_CAT), row), pl.BlockSpec((1, tm, MLA_V), row)],
        out_shape=[jax.ShapeDtypeStruct((B, T, MLA_HEADS * QK_CAT), BF16),
                   jax.ShapeDtypeStruct((B, T, MLA_V), BF16)],
        compiler_params=_cparams(("arbitrary", "arbitrary")),
        name="kv_up",
    )(ckv, kr_pad, kn_gain, w_uk16, w_uv16)


def _chunk_mask(qpos0, kpos0, tq, tk):
    qc = (lax.broadcasted_iota(jnp.int32, (tq, tk), 0) + qpos0) // CHUNK
    kc = (lax.broadcasted_iota(jnp.int32, (tq, tk), 1) + kpos0) // CHUNK
    return kc <= qc


def _attn_prompt_kernel(q_ref, k_ref, v_ref, o_ref, m_ref, l_ref, acc_ref, *, tile):
    i = pl.program_id(2)
    q = q_ref[0]
    m_ref[...] = jnp.full_like(m_ref, -1e30)
    l_ref[...] = jnp.zeros_like(l_ref)
    acc_ref[...] = jnp.zeros_like(acc_ref)

    def step(j, mask):
        start = pl.multiple_of(j * tile, tile)
        s = _dot_nt(q, k_ref[0, pl.ds(start, tile), :])
        if mask is not None:
            s = jnp.where(mask, s, -1e30)
        m_old = m_ref[...]
        m_new = jnp.maximum(m_old, jnp.max(s, axis=-1, keepdims=True))
        p = jnp.exp(s - m_new)
        alpha = jnp.exp(m_old - m_new)
        l_ref[...] = alpha * l_ref[...] + jnp.sum(p, axis=-1, keepdims=True)
        acc_ref[...] = alpha * acc_ref[...] + _dot(p.astype(BF16), v_ref[0, pl.ds(start, tile), :])
        m_ref[...] = m_new

    def body(j, carry):
        step(j, None)
        return carry

    lax.fori_loop(0, i, body, 0)
    step(i, _chunk_mask(0, 0, tile, tile))
    o_ref[0] = (acc_ref[...] / l_ref[...]).astype(BF16)


def _attn_prompt(qcat, kcat, v, tile):
    B, T, _ = v.shape
    return pl.pallas_call(
        functools.partial(_attn_prompt_kernel, tile=tile),
        grid=(B, MLA_HEADS, T // tile),
        in_specs=[pl.BlockSpec((1, tile, QK_CAT), lambda b, h, i: (b, i, h)),
                  pl.BlockSpec((1, T, QK_CAT), lambda b, h, i: (b, 0, h)),
                  pl.BlockSpec((1, T, V_HEAD), lambda b, h, i: (b, 0, h))],
        out_specs=pl.BlockSpec((1, tile, V_HEAD), lambda b, h, i: (b, i, h)),
        out_shape=jax.ShapeDtypeStruct((B, T, MLA_V), BF16),
        scratch_shapes=[pltpu.VMEM((tile, 1), F32), pltpu.VMEM((tile, 1), F32),
                        pltpu.VMEM((tile, V_HEAD), F32)],
        compiler_params=_cparams(("arbitrary", "arbitrary", "arbitrary")),
        name="attn_prompt",
    )(qcat, kcat, v)


def _attn_sample_kernel(q_ref, kp_ref, vp_ref, kn_ref, vn_ref, o_ref, *, past_len):
    q = q_ref[0]
    tq = q.shape[0]
    tp = kp_ref.shape[1]
    s1 = jnp.where(_chunk_mask(past_len, 0, tq, tp), _dot_nt(q, kp_ref[0]), -1e30)
    s2 = jnp.where(_chunk_mask(past_len, past_len, tq, tq), _dot_nt(q, kn_ref[0]), -1e30)
    m = jnp.maximum(jnp.max(s1, axis=-1, keepdims=True), jnp.max(s2, axis=-1, keepdims=True))
    p1 = jnp.exp(s1 - m)
    p2 = jnp.exp(s2 - m)
    l = jnp.sum(p1, axis=-1, keepdims=True) + jnp.sum(p2, axis=-1, keepdims=True)
    acc = _dot(p1.astype(BF16), vp_ref[0]) + _dot(p2.astype(BF16), vn_ref[0])
    o_ref[0] = (acc / l).astype(BF16)


def _attn_sample(qcat, kcat_past, v_past, kcat_new, v_new):
    B, T, _ = v_new.shape
    P = v_past.shape[1]
    bh = lambda b, h: (b, 0, h)
    return pl.pallas_call(
        functools.partial(_attn_sample_kernel, past_len=P),
        grid=(B, MLA_HEADS),
        in_specs=[pl.BlockSpec((1, T, QK_CAT), bh), pl.BlockSpec((1, P, QK_CAT), bh),
                  pl.BlockSpec((1, P, V_HEAD), bh), pl.BlockSpec((1, T, QK_CAT), bh),
                  pl.BlockSpec((1, T, V_HEAD), bh)],
        out_specs=pl.BlockSpec((1, T, V_HEAD), bh),
        out_shape=jax.ShapeDtypeStruct((B, T, MLA_V), BF16),
        compiler_params=_cparams(("arbitrary", "arbitrary")),
        name="attn_sample",
    )(qcat, kcat_past, v_past, kcat_new, v_new)


def _out_kernel(x_ref, mod_ref, ua_ref, ob_ref, zb_ref, ga_ref, gb_ref, wdn_ref, wmla_ref, wout_ref, y_ref):
    bb, tm, d = x_ref.shape
    rows = bb * tm
    zb = zb_ref[...].astype(F32)
    ub = (ob_ref[...].astype(F32) * (zb * jax.nn.sigmoid(zb))).astype(BF16).reshape(rows, d)
    ya = _dot(ua_ref[...].reshape(rows, d), wdn_ref[...])
    yb = _dot(ub, wmla_ref[...])
    ga = jax.nn.sigmoid(ga_ref[...].astype(F32)).reshape(rows, d)
    gb = jax.nn.sigmoid(gb_ref[...].astype(F32)).reshape(rows, d)
    merged = (ga * ya + gb * yb).astype(BF16)
    out = _dot(merged, wout_ref[...]).reshape(bb, tm, d)
    gate = mod_ref[:, :, 2 * d:3 * d]
    y_ref[...] = x_ref[...] + gate * out


def _out_proj(x, mod3, u_a, o_b, z_b, g_a, g_b, w_dn16, w_mla16, w_out16, bb, tm):
    B, T, _ = x.shape
    row = lambda b, t: (b, t, 0)
    act = pl.BlockSpec((bb, tm, D_MODEL), row)
    return pl.pallas_call(
        _out_kernel,
        grid=(B // bb, T // tm),
        in_specs=[act, pl.BlockSpec((bb, 1, 3 * D_MODEL), lambda b, t: (b, 0, 0)), act, act, act, act, act,
                  _resident((D_MODEL, D_MODEL)), _resident((D_MODEL, D_MODEL)), _resident((D_MODEL, D_MODEL))],
        out_specs=act,
        out_shape=jax.ShapeDtypeStruct((B, T, D_MODEL), F32),
        compiler_params=_cparams(("arbitrary", "arbitrary")),
        name="out_proj",
    )(x, mod3, u_a, o_b, z_b, g_a, g_b, w_dn16, w_mla16, w_out16)


def _lane_vec(v, off):
    return jnp.zeros((1, LANES), F32).at[0, off:off + v.shape[0]].set(v)


def _layer(x, mod, conv_state, s0, past, prm, bb, tm, q_off, attn_tile):
    B, T, _ = x.shape
    mod3 = mod.reshape(B, 1, 3 * D_MODEL)
    qkv, z_a, qn_raw, qr_raw, z_b, g_a, g_b, ckv_raw, small = _in_proj(
        x, mod3, prm["norm_gain"], prm["w_pack"], bb, tm)
    conv8 = jnp.pad(conv_state, ((0, 0), (8 - (CONV_W - 1), 0), (0, 0)))
    u_a, s_new, conv_new8 = _deltanet(qkv, small, z_a, conv8, s0, prm["w_conv"], prm["alog_v"], prm["dtb_v"],
                                      prm["dn_out_norm"])
    qcat, kcat, v, ckv_new, kr_new = _mla_prep(
        qn_raw, qr_raw, ckv_raw, small, prm["q_nope_norm"], prm["qr_gain"], prm["kv_norm"], prm["kr_gain"],
        prm["k_nope_norm"], prm["w_uk"], prm["w_uv"], bb, tm, q_off)
    if past is None:
        o_b = _attn_prompt(qcat, kcat, v, attn_tile)
    else:
        past_ckv, past_kr = past
        kr_pad = jnp.pad(past_kr, ((0, 0), (0, 0), (0, LANES - QK_ROPE)))
        kcat_p, v_p = _kv_up_cached(past_ckv, kr_pad, prm["k_nope_norm"], prm["w_uk"], prm["w_uv"], 512)
        o_b = _attn_sample(qcat, kcat_p, v_p, kcat, v)
    y = _out_proj(x, mod3, u_a, o_b, z_b, g_a, g_b, prm["w_o_dn"], prm["w_o_mla"], prm["w_out"], bb, tm)
    return y, conv_new8[:, 8 - (CONV_W - 1):], s_new, ckv_new, kr_new


def kernel(x_prompt, x_sample, c_prompt, c_sample, cache_ckv, cache_krope, state_delta, state_conv, norm_gain, w_ada, b_ada, w_in, w_conv, a_log, dt_bias, dn_out_norm, q_nope_norm, q_rope_norm, k_nope_norm, k_rope_norm, kv_norm, w_uk, w_uv, w_o_dn, w_o_mla, w_out):
    depth = w_in.shape[0]
    assert depth == 1, "single-layer configuration"
    l = 0
    B, T, _ = x_prompt.shape
    Bs, Ts, _ = x_sample.shape
    past_len = cache_ckv.shape[2]

    row = lambda v: v.reshape(1, -1).astype(F32)
    prm = dict(
        norm_gain=row(norm_gain[l]),
        w_pack=_pack_w_in(w_in[l]),
        w_conv=w_conv[l],
        alog_v=_lane_vec(a_log[l], ALPHA_OFF),
        dtb_v=_lane_vec(dt_bias[l], ALPHA_OFF),
        dn_out_norm=row(dn_out_norm[l]),
        q_nope_norm=row(q_nope_norm[l]),
        qr_gain=jnp.tile(row(q_rope_norm[l]), (1, LANES // QK_ROPE)),
        kv_norm=row(kv_norm[l]),
        kr_gain=_lane_vec(k_rope_norm[l], KR_OFF),
        k_nope_norm=row(k_nope_norm[l]),
        w_uk=w_uk[l].astype(BF16),
        w_uv=w_uv[l].astype(BF16),
        w_o_dn=w_o_dn[l].astype(BF16),
        w_o_mla=w_o_mla[l].astype(BF16),
        w_out=w_out[l].astype(BF16),
    )

    rows = B + Bs
    rows_pad = -(-rows // 8) * 8
    c_all = jnp.concatenate([c_prompt, c_sample, jnp.zeros((rows_pad - rows, D_MODEL), F32)], axis=0)
    mod = _ada(c_all, w_ada[l], b_ada[l].reshape(1, -1))

    zeros_conv = jnp.zeros((B, CONV_W - 1, DN_CONV_CH), F32)
    zeros_state = jnp.zeros((B, DN_HEADS, DN_DK, DN_DV), F32)
    yp, cvp, sdp, kvp, krp = _layer(x_prompt, mod[:B], zeros_conv, zeros_state, None, prm,
                                    bb=1, tm=256, q_off=0, attn_tile=512)
    ys, cvs, sds, kvs, krs = _layer(x_sample, mod[B:rows], state_conv[l], state_delta[l],
                                    (cache_ckv[l], cache_krope[l]), prm,
                                    bb=Bs, tm=Ts, q_off=past_len, attn_tile=None)
    st = lambda a: a[None]
    return (yp, ys, st(kvp), st(krp), st(sdp), st(cvp), st(kvs), st(krs), st(sds), st(cvs))
```

```python
import functools
import math

import jax
import jax.numpy as jnp
from jax import lax
from jax.experimental import pallas as pl
from jax.experimental.pallas import tpu as pltpu

D_MODEL = 1024
CHUNK = 64
EPS = 1e-6
DN_HEADS = 8
DN_DK = 128
DN_DV = 128
DN_QK = DN_HEADS * DN_DK
DN_V = DN_HEADS * DN_DV
DN_CONV_CH = 2 * DN_QK + DN_V
CONV_W = 4
MLA_HEADS = 8
QK_NOPE = 128
QK_ROPE = 64
QK_HEAD = QK_NOPE + QK_ROPE
V_HEAD = 128
KV_RANK = 512
MLA_Q = MLA_HEADS * QK_HEAD
MLA_V = MLA_HEADS * V_HEAD
ROPE_THETA = 10000.0

LANES = 128
QK_CAT = 256
KR_OFF = 0
BETA_OFF = QK_ROPE
ALPHA_OFF = QK_ROPE + DN_HEADS
VMEM_LIMIT = 56 * 1024 * 1024

F32 = jnp.float32
BF16 = jnp.bfloat16
HI = lax.Precision.HIGHEST


def _dot(a, b):
    return jnp.dot(a, b, preferred_element_type=F32)


def _dot_nt(a, b, precision=None):
    return lax.dot_general(a, b, (((1,), (1,)), ((), ())), preferred_element_type=F32, precision=precision)


def _dot_tn(a, b):
    return lax.dot_general(a, b, (((0,), (0,)), ((), ())), preferred_element_type=F32)


def _cparams(sem):
    return pltpu.CompilerParams(dimension_semantics=sem, vmem_limit_bytes=VMEM_LIMIT)


def _resident(shape):
    nd = len(shape)
    return pl.BlockSpec(shape, lambda *_: (0,) * nd, pipeline_mode=pl.Buffered(1))


def _ada_kernel(c_ref, w_ref, b_ref, o_ref):
    o_ref[...] = jnp.dot(c_ref[...], w_ref[...], preferred_element_type=F32, precision=HI) + b_ref[...]


def _ada(c_all, w_ada, b_ada):
    rows = c_all.shape[0]
    tn = 512
    return pl.pallas_call(
        _ada_kernel,
        grid=(3 * D_MODEL // tn,),
        in_specs=[pl.BlockSpec((rows, D_MODEL), lambda j: (0, 0)),
                  pl.BlockSpec((D_MODEL, tn), lambda j: (0, j)),
                  pl.BlockSpec((1, tn), lambda j: (0, j))],
        out_specs=pl.BlockSpec((rows, tn), lambda j: (0, j)),
        out_shape=jax.ShapeDtypeStruct((rows, 3 * D_MODEL), F32),
        compiler_params=_cparams(("arbitrary",)),
        name="ada",
    )(c_all, w_ada, b_ada)


_PROJ_GROUPS = (("qkv", DN_CONV_CH, BF16), ("z_a", DN_V, BF16), ("qn", MLA_HEADS * QK_NOPE, BF16),
                ("qr", MLA_HEADS * QK_ROPE, BF16), ("z_b", MLA_V, BF16), ("g_a", D_MODEL, BF16),
                ("g_b", D_MODEL, BF16), ("ckv", KV_RANK, F32), ("small", LANES, F32))
_PROJ_WIDTH = sum(w for _, w, _ in _PROJ_GROUPS)


def _pack_w_in(w_in):
    o = 0
    qkv = w_in[:, o:o + DN_CONV_CH]; o += DN_CONV_CH
    z_a = w_in[:, o:o + DN_V]; o += DN_V
    beta = w_in[:, o:o + DN_HEADS]; o += DN_HEADS
    alpha = w_in[:, o:o + DN_HEADS]; o += DN_HEADS
    q = w_in[:, o:o + MLA_Q].reshape(D_MODEL, MLA_HEADS, QK_HEAD); o += MLA_Q
    ckv = w_in[:, o:o + KV_RANK]; o += KV_RANK
    kr = w_in[:, o:o + QK_ROPE]; o += QK_ROPE
    z_b = w_in[:, o:o + MLA_V]; o += MLA_V
    g_a = w_in[:, o:o + D_MODEL]; o += D_MODEL
    g_b = w_in[:, o:o + D_MODEL]
    qn = q[:, :, :QK_NOPE].reshape(D_MODEL, MLA_HEADS * QK_NOPE)
    qr = q[:, :, QK_NOPE:].reshape(D_MODEL, MLA_HEADS * QK_ROPE)
    pad = jnp.zeros((D_MODEL, LANES - QK_ROPE - 2 * DN_HEADS), w_in.dtype)
    small = jnp.concatenate([kr, beta, alpha, pad], axis=1)
    return jnp.concatenate([qkv, z_a, qn, qr, z_b, g_a, g_b, ckv, small], axis=1).astype(BF16)


def _in_proj_kernel(x_ref, mod_ref, gain_ref, w_ref, *out_refs):
    bb, tm, d = x_ref.shape
    x = x_ref[...]
    ms = jnp.mean(x * x, axis=-1, keepdims=True)
    y = x * lax.rsqrt(ms + EPS) * gain_ref[...]
    shift = mod_ref[:, :, 0:d]
    scale = mod_ref[:, :, d:2 * d]
    h = (y * (1.0 + scale) + shift).astype(BF16).reshape(bb * tm, d)
    o = 0
    for (_, width, dtype), ref in zip(_PROJ_GROUPS, out_refs):
        ref[...] = _dot(h, w_ref[:, o:o + width]).astype(dtype).reshape(bb, tm, width)
        o += width


def _in_proj(x, mod3, gain, w_pack, bb, tm):
    B, T, _ = x.shape
    row = lambda b, t: (b, t, 0)
    return pl.pallas_call(
        _in_proj_kernel,
        grid=(B // bb, T // tm),
        in_specs=[pl.BlockSpec((bb, tm, D_MODEL), row),
                  pl.BlockSpec((bb, 1, 3 * D_MODEL), lambda b, t: (b, 0, 0)),
                  _resident((1, D_MODEL)),
                  _resident((D_MODEL, _PROJ_WIDTH))],
        out_specs=[pl.BlockSpec((bb, tm, w), row) for _, w, _ in _PROJ_GROUPS],
        out_shape=[jax.ShapeDtypeStruct((B, T, w), dt) for _, w, dt in _PROJ_GROUPS],
        compiler_params=_cparams(("arbitrary", "arbitrary")),
        name="in_proj",
    )(x, mod3, gain, w_pack)


def _softplus(x):
    return jnp.maximum(x, 0.0) + jnp.log(1.0 + jnp.exp(-jnp.abs(x)))


def _dn_prep_kernel(qkv_ref, prev_ref, cs_ref, small_ref, wconv_ref, alog_ref, dtb_ref,
                    wq_ref, u_ref, kd_ref, attn_ref, egl_ref, buf_ref, *, C):
    t = pl.program_id(1)
    tm = qkv_ref.shape[1]
    nc = tm // C
    pad = prev_ref.shape[1]

    buf_ref[0:pad, :] = jnp.where(t == 0, cs_ref[0], prev_ref[0].astype(F32))
    buf_ref[pad:pad + tm, :] = qkv_ref[0].astype(F32)
    conv = buf_ref[pad - 3:pad - 3 + tm, :] * wconv_ref[0:1, :]
    for i in range(1, CONV_W):
        conv = conv + buf_ref[pad - 3 + i:pad - 3 + i + tm, :] * wconv_ref[i:i + 1, :]
    act = conv * jax.nn.sigmoid(conv)

    sm = small_ref[0]
    beta_all = jax.nn.sigmoid(sm)
    g_all = -jnp.exp(alog_ref[...]) * _softplus(sm + dtb_ref[...])
    rt = lax.broadcasted_iota(jnp.int32, (tm, tm), 0)
    ct = lax.broadcasted_iota(jnp.int32, (tm, tm), 1)
    chunk_tri = ((rt // C == ct // C) & (rt >= ct)).astype(F32)
    gcum = jnp.dot(chunk_tri, g_all, preferred_element_type=F32, precision=HI)
    sel = (lax.broadcasted_iota(jnp.int32, (DN_HEADS, LANES), 1)
           == lax.broadcasted_iota(jnp.int32, (DN_HEADS, LANES), 0) + ALPHA_OFF).astype(F32)
    gcum_t = _dot_nt(sel, gcum, precision=HI)

    ri = lax.broadcasted_iota(jnp.int32, (C, C), 0)
    ci = lax.broadcasted_iota(jnp.int32, (C, C), 1)
    tri_incl = ri >= ci
    tri_strict = ri > ci
    eye = (ri == ci).astype(F32)
    pair_masks = []
    m = 1
    while m < C:
        pair_masks.append((ri // (2 * m) == ci // (2 * m)) & (ri // m != ci // m))
        m *= 2

    heads = range(DN_HEADS)
    qn, kn = [], []
    for h in heads:
        qh = act[:, h * DN_DK:(h + 1) * DN_DK]
        kh = act[:, DN_QK + h * DN_DK:DN_QK + (h + 1) * DN_DK]
        qn.append(qh * lax.rsqrt(jnp.sum(qh * qh, axis=-1, keepdims=True) + EPS) * (DN_DK ** -0.5))
        kn.append(kh * lax.rsqrt(jnp.sum(kh * kh, axis=-1, keepdims=True) + EPS))

    attn_ref[...] = jnp.zeros_like(attn_ref)
    items = [(c, h) for c in range(nc) for h in heads]
    rows = lambda c: slice(c * C, (c + 1) * C)
    decay, rhs, qk = [], [], []
    for c, h in items:
        r = rows(c)
        gc = gcum[r, ALPHA_OFF + h:ALPHA_OFF + h + 1]
        glast = gcum[c * C + C - 1:(c + 1) * C, ALPHA_OFF + h:ALPHA_OFF + h + 1]
        eg = jnp.exp(gc)
        bcol = beta_all[r, BETA_OFF + h:BETA_OFF + h + 1]
        decay.append(jnp.exp(jnp.where(tri_incl, gc - gcum_t[h:h + 1, r], -1e30)))
        kh = kn[h][r]
        kbh = kh * bcol
        vh = act[r, 2 * DN_QK + h * DN_DV:2 * DN_QK + (h + 1) * DN_DV]
        rhs.append(jnp.concatenate([kbh * eg, vh * bcol], axis=1).astype(BF16))
        lo = h * DN_DK
        wq_ref[0, c, C:2 * C, lo:lo + DN_DK] = (qn[h][r] * eg).astype(BF16)
        kd_ref[0, r, lo:lo + DN_DK] = (kh * jnp.exp(glast - gc)).astype(BF16)
        qk.append(_dot_nt(jnp.concatenate([kbh, qn[h][r]], axis=0).astype(BF16), kh.astype(BF16)))
    for c in range(nc):
        egl_ref[0, c] = jnp.exp(jnp.broadcast_to(gcum_t[:, c * C + C - 1:(c + 1) * C], (DN_HEADS, LANES)))

    lmat = []
    for i, (c, h) in enumerate(items):
        lmat.append(jnp.where(tri_strict, qk[i][:C] * decay[i], 0.0))
        lo = h * DN_DK
        attn_ref[0, rows(c), lo:lo + C] = (qk[i][C:] * decay[i]).astype(BF16)

    pinv = [eye - jnp.where(pair_masks[0], l, 0.0) for l in lmat]
    for mask in pair_masks[1:]:
        p16 = [p.astype(BF16) for p in pinv]
        tmp = [_dot(p16[i], jnp.where(mask, lmat[i], 0.0).astype(BF16)).astype(BF16) for i in range(len(items))]
        pinv = [pinv[i] - _dot(tmp[i], p16[i]) for i in range(len(items))]

    for i, (c, h) in enumerate(items):
        wu = _dot(pinv[i].astype(BF16), rhs[i])
        lo = h * DN_DK
        wq_ref[0, c, 0:C, lo:lo + DN_DK] = wu[:, :DN_DK].astype(BF16)
        u_ref[0, rows(c), lo:lo + DN_DV] = wu[:, DN_DK:]


def _dn_scan_kernel(wq_ref, u_ref, kd_ref, attn_ref, egl_ref, za_ref, s0_ref, onorm_ref,
                    ua_ref, sfin_ref, s_ref, *, C):
    n = pl.program_id(1)
    bg, G = wq_ref.shape[0], wq_ref.shape[1]

    @pl.when(n == 0)
    def _():
        s_ref[...] = s0_ref[...]

    chains = [(b, h) for b in range(bg) for h in range(DN_HEADS)]
    for g in range(G):
        r = slice(g * C, (g + 1) * C)
        s_old = [s_ref[b, h] for b, h in chains]
        s16 = [s.astype(BF16) for s in s_old]
        ws = [_dot(wq_ref[b, g, :, h * DN_DK:(h + 1) * DN_DK], s16[i]) for i, (b, h) in enumerate(chains)]
        v16 = [(u_ref[b, r, h * DN_DV:(h + 1) * DN_DV] - ws[i][:C]).astype(BF16)
               for i, (b, h) in enumerate(chains)]
        for i, (b, h) in enumerate(chains):
            lo = h * DN_DK
            s_ref[b, h] = s_old[i] * egl_ref[b, g, h:h + 1, :] + _dot_tn(kd_ref[b, r, lo:lo + DN_DK], v16[i])
        for i, (b, h) in enumerate(chains):
            lo = h * DN_DV
            o = ws[i][C:] + _dot(attn_ref[b, r, lo:lo + C], v16[i])
            o = o * lax.rsqrt(jnp.mean(o * o, axis=-1, keepdims=True) + EPS) * onorm_ref[...]
            z = za_ref[b, r, lo:lo + DN_DV].astype(F32)
            ua_ref[b, r, lo:lo + DN_DV] = (o * (z * jax.nn.sigmoid(z))).astype(BF16)

    @pl.when(n == pl.num_programs(1) - 1)
    def _():
        sfin_ref[...] = s_ref[...]


def _deltanet(qkv, small, z_a, conv_state, s0, w_conv, alog_v, dtb_v, onorm, tm, bg, G):
    B, T, _ = qkv.shape
    C = min(CHUNK, T)
    N = T // C
    nc = tm // C
    hist_rows = 16
    cs = jnp.pad(conv_state, ((0, 0), (hist_rows - (CONV_W - 1), 0), (0, 0)))
    tile = lambda b, t: (b, t, 0)
    prev = lambda b, t: (b, jnp.maximum(t * (tm // hist_rows) - 1, 0), 0)
    wq, u, kd, attn, egl = pl.pallas_call(
        functools.partial(_dn_prep_kernel, C=C),
        grid=(B, T // tm),
        in_specs=[pl.BlockSpec((1, tm, DN_CONV_CH), tile),
                  pl.BlockSpec((1, hist_rows, DN_CONV_CH), prev),
                  pl.BlockSpec((1, hist_rows, DN_CONV_CH), lambda b, t: (b, 0, 0)),
                  pl.BlockSpec((1, tm, LANES), tile),
                  _resident((CONV_W, DN_CONV_CH)),
                  _resident((1, LANES)),
                  _resident((1, LANES))],
        out_specs=[pl.BlockSpec((1, nc, 2 * C, DN_QK), lambda b, t: (b, t, 0, 0)),
                   pl.BlockSpec((1, tm, DN_V), tile),
                   pl.BlockSpec((1, tm, DN_QK), tile),
                   pl.BlockSpec((1, tm, DN_V), tile),
                   pl.BlockSpec((1, nc, DN_HEADS, LANES), lambda b, t: (b, t, 0, 0))],
        out_shape=[jax.ShapeDtypeStruct((B, N, 2 * C, DN_QK), BF16),
                   jax.ShapeDtypeStruct((B, T, DN_V), F32),
                   jax.ShapeDtypeStruct((B, T, DN_QK), BF16),
                   jax.ShapeDtypeStruct((B, T, DN_V), BF16),
                   jax.ShapeDtypeStruct((B, N, DN_HEADS, LANES), F32)],
        scratch_shapes=[pltpu.VMEM((tm + hist_rows, DN_CONV_CH), F32)],
        compiler_params=_cparams(("arbitrary", "arbitrary")),
        name="dn_prep",
    )(qkv, qkv, cs, small, w_conv, alog_v, dtb_v)

    grp = lambda b, n: (b, n, 0)
    grp4 = lambda b, n: (b, n, 0, 0)
    state = pl.BlockSpec((bg, DN_HEADS, DN_DK, DN_DV), lambda b, n: (b, 0, 0, 0))
    u_a, s_new = pl.pallas_call(
        functools.partial(_dn_scan_kernel, C=C),
        grid=(B // bg, N // G),
        in_specs=[pl.BlockSpec((bg, G, 2 * C, DN_QK), grp4),
                  pl.BlockSpec((bg, G * C, DN_V), grp),
                  pl.BlockSpec((bg, G * C, DN_QK), grp),
                  pl.BlockSpec((bg, G * C, DN_V), grp),
                  pl.BlockSpec((bg, G, DN_HEADS, LANES), grp4),
                  pl.BlockSpec((bg, G * C, DN_V), grp),
                  state,
                  _resident((1, DN_DV))],
        out_specs=[pl.BlockSpec((bg, G * C, DN_V), grp), state],
        out_shape=[jax.ShapeDtypeStruct((B, T, DN_V), BF16),
                   jax.ShapeDtypeStruct((B, DN_HEADS, DN_DK, DN_DV), F32)],
        scratch_shapes=[pltpu.VMEM((bg, DN_HEADS, DN_DK, DN_DV), F32)],
        compiler_params=_cparams(("arbitrary", "arbitrary")),
        name="dn_scan",
    )(wq, u, kd, attn, egl, z_a, s0, onorm)
    conv_new = qkv[:, T - (CONV_W - 1):, :].astype(F32)
    return u_a, s_new, conv_new


def _head_rms(x, gain_row, width):
    outs = []
    for h in range(x.shape[1] // width):
        xh = x[:, h * width:(h + 1) * width]
        outs.append(xh * lax.rsqrt(jnp.mean(xh * xh, axis=-1, keepdims=True) + EPS) * gain_row)
    return outs


def _kv_up(ckv16, kr_pad16, wuk_ref, wuv_ref, knorm_ref, kcat_ref, v_ref, shape3):
    bb, tm = shape3
    kn = _head_rms(_dot(ckv16, wuk_ref[...]), knorm_ref[...], QK_NOPE)
    for h in range(MLA_HEADS):
        kcat_ref[:, :, h * QK_CAT:h * QK_CAT + QK_NOPE] = kn[h].astype(BF16).reshape(bb, tm, QK_NOPE)
        kcat_ref[:, :, h * QK_CAT + QK_NOPE:(h + 1) * QK_CAT] = kr_pad16.reshape(bb, tm, LANES)
    v_ref[...] = _dot(ckv16, wuv_ref[...]).astype(BF16).reshape(bb, tm, MLA_V)


def _rope_angles(rows, tm, t0, q_off, lanes):
    pos = (lax.broadcasted_iota(jnp.int32, (rows, lanes), 0) % tm + t0 + q_off).astype(F32)
    half = QK_ROPE // 2
    fidx = (lax.broadcasted_iota(jnp.int32, (rows, lanes), 1) % half).astype(F32)
    inv = jnp.exp(fidx * (-math.log(ROPE_THETA) / half))
    ang = pos * inv
    return jnp.cos(ang), jnp.sin(ang)


def _mla_prep_kernel(qn_ref, qr_ref, ckv_ref, small_ref, qng_ref, qrg_ref, kvg_ref, krg_ref, kng_ref,
                     wuk_ref, wuv_ref, qcat_ref, kcat_ref, v_ref, ckvn_ref, krn_ref, *, q_off):
    bb, tm, _ = qn_ref.shape
    rows = bb * tm
    t0 = pl.program_id(1) * tm
    scale = QK_HEAD ** -0.5
    cos, sin = _rope_angles(rows, tm, t0, q_off, LANES)
    lane = lax.broadcasted_iota(jnp.int32, (rows, LANES), 1)
    low_half = lane < QK_ROPE
    first = (lane % QK_ROPE) < (QK_ROPE // 2)

    def rope(y):
        rot = jnp.where(first, -pltpu.roll(y, LANES - QK_ROPE // 2, 1), pltpu.roll(y, QK_ROPE // 2, 1))
        return y * cos + rot * sin

    def rms64(x):
        xx = x * x
        s_lo = jnp.sum(jnp.where(low_half, xx, 0.0), axis=-1, keepdims=True)
        s_hi = jnp.sum(jnp.where(low_half, 0.0, xx), axis=-1, keepdims=True)
        return lax.rsqrt(jnp.where(low_half, s_lo, s_hi) * (1.0 / QK_ROPE) + EPS)

    qn = _head_rms(qn_ref[...].astype(F32).reshape(rows, MLA_HEADS * QK_NOPE), qng_ref[...] * scale, QK_NOPE)
    qr_all = qr_ref[...].astype(F32).reshape(rows, MLA_HEADS * QK_ROPE)
    for c in range(MLA_HEADS // 2):
        x = qr_all[:, c * LANES:(c + 1) * LANES]
        y = rope(x * rms64(x) * qrg_ref[...]) * scale
        even = jnp.where(low_half, y, 0.0)
        odd = jnp.where(low_half, pltpu.roll(y, QK_ROPE, 1), 0.0)
        for h, part in ((2 * c, even), (2 * c + 1, odd)):
            qcat_ref[:, :, h * QK_CAT:h * QK_CAT + QK_NOPE] = qn[h].astype(BF16).reshape(bb, tm, QK_NOPE)
            qcat_ref[:, :, h * QK_CAT + QK_NOPE:(h + 1) * QK_CAT] = part.astype(BF16).reshape(bb, tm, LANES)

    sm = small_ref[...].reshape(rows, LANES)
    kr = rope(sm * rms64(sm) * krg_ref[...])
    krn_ref[...] = kr[:, :QK_ROPE].reshape(bb, tm, QK_ROPE)
    kr_pad16 = jnp.where(low_half, kr, 0.0).astype(BF16)

    ckv = ckv_ref[...].reshape(rows, KV_RANK)
    ckvn = ckv * lax.rsqrt(jnp.mean(ckv * ckv, axis=-1, keepdims=True) + EPS) * kvg_ref[...]
    ckvn_ref[...] = ckvn.reshape(bb, tm, KV_RANK)
    _kv_up(ckvn.astype(BF16), kr_pad16, wuk_ref, wuv_ref, kng_ref, kcat_ref, v_ref, (bb, tm))


def _mla_prep(qn_raw, qr_raw, ckv_raw, small, qn_gain, qr_gain, kv_gain, kr_gain, kn_gain, w_uk16, w_uv16,
              bb, tm, q_off):
    B, T, _ = qn_raw.shape
    row = lambda b, t: (b, t, 0)
    widths_in = (MLA_HEADS * QK_NOPE, MLA_HEADS * QK_ROPE, KV_RANK, LANES)
    outs = ((MLA_HEADS * QK_CAT, BF16), (MLA_HEADS * QK_CAT, BF16), (MLA_V, BF16), (KV_RANK, F32), (QK_ROPE, F32))
    return pl.pallas_call(
        functools.partial(_mla_prep_kernel, q_off=q_off),
        grid=(B // bb, T // tm),
        in_specs=[pl.BlockSpec((bb, tm, w), row) for w in widths_in]
        + [_resident((1, QK_NOPE)), _resident((1, LANES)), _resident((1, KV_RANK)), _resident((1, LANES)),
           _resident((1, QK_NOPE)), _resident((KV_RANK, MLA_HEADS * QK_NOPE)), _resident((KV_RANK, MLA_V))],
        out_specs=[pl.BlockSpec((bb, tm, w), row) for w, _ in outs],
        out_shape=[jax.ShapeDtypeStruct((B, T, w), dt) for w, dt in outs],
        compiler_params=_cparams(("arbitrary", "arbitrary")),
        name="mla_prep",
    )(qn_raw, qr_raw, ckv_raw, small, qn_gain, qr_gain, kv_gain, kr_gain, kn_gain, w_uk16, w_uv16)


def _kv_up_kernel(ckv_ref, kr_ref, kng_ref, wuk_ref, wuv_ref, kcat_ref, v_ref):
    bb, tm, _ = ckv_ref.shape
    ckv16 = ckv_ref[...].reshape(bb * tm, KV_RANK).astype(BF16)
    kr_pad16 = kr_ref[...].reshape(bb * tm, LANES).astype(BF16)
    _kv_up(ckv16, kr_pad16, wuk_ref, wuv_ref, kng_ref, kcat_ref, v_ref, (bb, tm))


def _kv_up_cached(ckv, kr_pad, kn_gain, w_uk16, w_uv16, tm):
    B, T, _ = ckv.shape
    row = lambda b, t: (b, t, 0)
    return pl.pallas_call(
        _kv_up_kernel,
        grid=(B, T // tm),
        in_specs=[pl.BlockSpec((1, tm, KV_RANK), row), pl.BlockSpec((1, tm, LANES), row),
                  _resident((1, QK_NOPE)), _resident((KV_RANK, MLA_HEADS * QK_NOPE)),
                  _resident((KV_RANK, MLA_V))],
        out_specs=[pl.BlockSpec((1, tm, MLA_HEADS * QK_CAT), row), pl.BlockSpec((1, tm, MLA_V), row)],
        out_shape=[jax.ShapeDtypeStruct((B, T, MLA_HEADS * QK_CAT), BF16),
                   jax.ShapeDtypeStruct((B, T, MLA_V), BF16)],
        compiler_params=_cparams(("arbitrary", "arbitrary")),
        name="kv_up",
    )(ckv, kr_pad, kn_gain, w_uk16, w_uv16)


def _chunk_mask(qpos0, kpos0, tq, tk):
    qc = (lax.broadcasted_iota(jnp.int32, (tq, tk), 0) + qpos0) // CHUNK
    kc = (lax.broadcasted_iota(jnp.int32, (tq, tk), 1) + kpos0) // CHUNK
    return kc <= qc


def _attn_prompt_kernel(q_ref, k_ref, v_ref, o_ref, m_ref, l_ref, acc_ref, *, tile):
    i = pl.program_id(2)
    q = q_ref[0]
    m_ref[...] = jnp.full_like(m_ref, -1e30)
    l_ref[...] = jnp.zeros_like(l_ref)
    acc_ref[...] = jnp.zeros_like(acc_ref)

    def step(j, mask):
        start = pl.multiple_of(j * tile, tile)
        s = _dot_nt(q, k_ref[0, pl.ds(start, tile), :])
        if mask is not None:
            s = jnp.where(mask, s, -1e30)
        m_old = m_ref[...]
        m_new = jnp.maximum(m_old, jnp.max(s, axis=-1, keepdims=True))
        p = jnp.exp(s - m_new)
        alpha = jnp.exp(m_old - m_new)
        l_ref[...] = alpha * l_ref[...] + jnp.sum(p, axis=-1, keepdims=True)
        acc_ref[...] = alpha * acc_ref[...] + _dot(p.astype(BF16), v_ref[0, pl.ds(start, tile), :])
        m_ref[...] = m_new

    def body(j, carry):
        step(j, None)
        return carry

    lax.fori_loop(0, i, body, 0)
    step(i, _chunk_mask(0, 0, tile, tile))
    o_ref[0] = (acc_ref[...] / l_ref[...]).astype(BF16)


def _attn_prompt(qcat, kcat, v, tile):
    B, T, _ = v.shape
    return pl.pallas_call(
        functools.partial(_attn_prompt_kernel, tile=tile),
        grid=(B, MLA_HEADS, T // tile),
        in_specs=[pl.BlockSpec((1, tile, QK_CAT), lambda b, h, i: (b, i, h)),
                  pl.BlockSpec((1, T, QK_CAT), lambda b, h, i: (b, 0, h)),
                  pl.BlockSpec((1, T, V_HEAD), lambda b, h, i: (b, 0, h))],
        out_specs=pl.BlockSpec((1, tile, V_HEAD), lambda b, h, i: (b, i, h)),
        out_shape=jax.ShapeDtypeStruct((B, T, MLA_V), BF16),
        scratch_shapes=[pltpu.VMEM((tile, 1), F32), pltpu.VMEM((tile, 1), F32),
                        pltpu.VMEM((tile, V_HEAD), F32)],
        compiler_params=_cparams(("arbitrary", "arbitrary", "arbitrary")),
        name="attn_prompt",
    )(qcat, kcat, v)


def _attn_sample_kernel(q_ref, kp_ref, vp_ref, kn_ref, vn_ref, o_ref, *, past_len):
    q = q_ref[0]
    tq = q.shape[0]
    tp = kp_ref.shape[1]
    s1 = jnp.where(_chunk_mask(past_len, 0, tq, tp), _dot_nt(q, kp_ref[0]), -1e30)
    s2 = jnp.where(_chunk_mask(past_len, past_len, tq, tq), _dot_nt(q, kn_ref[0]), -1e30)
    m = jnp.maximum(jnp.max(s1, axis=-1, keepdims=True), jnp.max(s2, axis=-1, keepdims=True))
    p1 = jnp.exp(s1 - m)
    p2 = jnp.exp(s2 - m)
    l = jnp.sum(p1, axis=-1, keepdims=True) + jnp.sum(p2, axis=-1, keepdims=True)
    acc = _dot(p1.astype(BF16), vp_ref[0]) + _dot(p2.astype(BF16), vn_ref[0])
    o_ref[0] = (acc / l).astype(BF16)


def _attn_sample(qcat, kcat_past, v_past, kcat_new, v_new):
    B, T, _ = v_new.shape
    P = v_past.shape[1]
    bh = lambda b, h: (b, 0, h)
    return pl.pallas_call(
        functools.partial(_attn_sample_kernel, past_len=P),
        grid=(B, MLA_HEADS),
        in_specs=[pl.BlockSpec((1, T, QK_CAT), bh), pl.BlockSpec((1, P, QK_CAT), bh),
                  pl.BlockSpec((1, P, V_HEAD), bh), pl.BlockSpec((1, T, QK_CAT), bh),
                  pl.BlockSpec((1, T, V_HEAD), bh)],
        out_specs=pl.BlockSpec((1, T, V_HEAD), bh),
        out_shape=jax.ShapeDtypeStruct((B, T, MLA_V), BF16),
        compiler_params=_cparams(("arbitrary", "arbitrary")),
        name="attn_sample",
    )(qcat, kcat_past, v_past, kcat_new, v_new)


def _out_kernel(x_ref, mod_ref, ua_ref, ob_ref, zb_ref, ga_ref, gb_ref, wdn_ref, wmla_ref, wout_ref, y_ref):
    bb, tm, d = x_ref.shape
    rows = bb * tm
    zb = zb_ref[...].astype(F32)
    ub = (ob_ref[...].astype(F32) * (zb * jax.nn.sigmoid(zb))).astype(BF16).reshape(rows, d)
    ya = _dot(ua_ref[...].reshape(rows, d), wdn_ref[...])
    yb = _dot(ub, wmla_ref[...])
    ga = jax.nn.sigmoid(ga_ref[...].astype(F32)).reshape(rows, d)
    gb = jax.nn.sigmoid(gb_ref[...].astype(F32)).reshape(rows, d)
    merged = (ga * ya + gb * yb).astype(BF16)
    out = _dot(merged, wout_ref[...]).reshape(bb, tm, d)
    gate = mod_ref[:, :, 2 * d:3 * d]
    y_ref[...] = x_ref[...] + gate * out


def _out_proj(x, mod3, u_a, o_b, z_b, g_a, g_b, w_dn16, w_mla16, w_out16, bb, tm):
    B, T, _ = x.shape
    row = lambda b, t: (b, t, 0)
    act = pl.BlockSpec((bb, tm, D_MODEL), row)
    return pl.pallas_call(
        _out_kernel,
        grid=(B // bb, T // tm),
        in_specs=[act, pl.BlockSpec((bb, 1, 3 * D_MODEL), lambda b, t: (b, 0, 0)), act, act, act, act, act,
                  _resident((D_MODEL, D_MODEL)), _resident((D_MODEL, D_MODEL)), _resident((D_MODEL, D_MODEL))],
        out_specs=act,
        out_shape=jax.ShapeDtypeStruct((B, T, D_MODEL), F32),
        compiler_params=_cparams(("arbitrary", "arbitrary")),
        name="out_proj",
    )(x, mod3, u_a, o_b, z_b, g_a, g_b, w_dn16, w_mla16, w_out16)


def _lane_vec(v, off):
    return jnp.zeros((1, LANES), F32).at[0, off:off + v.shape[0]].set(v)


def _layer(x, mod, conv_state, s0, past, prm, bb, tm, q_off, attn_tile, dn_tiles):
    B, T, _ = x.shape
    mod3 = mod.reshape(B, 1, 3 * D_MODEL)
    qkv, z_a, qn_raw, qr_raw, z_b, g_a, g_b, ckv_raw, small = _in_proj(
        x, mod3, prm["norm_gain"], prm["w_pack"], bb, tm)
    u_a, s_new, conv_new = _deltanet(qkv, small, z_a, conv_state, s0, prm["w_conv"], prm["alog_v"],
                                     prm["dtb_v"], prm["dn_out_norm"], **dn_tiles)
    qcat, kcat, v, ckv_new, kr_new = _mla_prep(
        qn_raw, qr_raw, ckv_raw, small, prm["q_nope_norm"], prm["qr_gain"], prm["kv_norm"], prm["kr_gain"],
        prm["k_nope_norm"], prm["w_uk"], prm["w_uv"], bb, tm, q_off)
    if past is None:
        o_b = _attn_prompt(qcat, kcat, v, attn_tile)
    else:
        past_ckv, past_kr = past
        kr_pad = jnp.pad(past_kr, ((0, 0), (0, 0), (0, LANES - QK_ROPE)))
        kcat_p, v_p = _kv_up_cached(past_ckv, kr_pad, prm["k_nope_norm"], prm["w_uk"], prm["w_uv"], 512)
        o_b = _attn_sample(qcat, kcat_p, v_p, kcat, v)
    y = _out_proj(x, mod3, u_a, o_b, z_b, g_a, g_b, prm["w_o_dn"], prm["w_o_mla"], prm["w_out"], bb, tm)
    return y, conv_new, s_new, ckv_new, kr_new


def kernel(x_prompt, x_sample, c_prompt, c_sample, cache_ckv, cache_krope, state_delta, state_conv, norm_gain, w_ada, b_ada, w_in, w_conv, a_log, dt_bias, dn_out_norm, q_nope_norm, q_rope_norm, k_nope_norm, k_rope_norm, kv_norm, w_uk, w_uv, w_o_dn, w_o_mla, w_out):
    depth = w_in.shape[0]
    assert depth == 1, "single-layer configuration"
    l = 0
    B, T, _ = x_prompt.shape
    Bs, Ts, _ = x_sample.shape
    past_len = cache_ckv.shape[2]

    row = lambda v: v.reshape(1, -1).astype(F32)
    prm = dict(
        norm_gain=row(norm_gain[l]),
        w_pack=_pack_w_in(w_in[l]),
        w_conv=w_conv[l],
        alog_v=_lane_vec(a_log[l], ALPHA_OFF),
        dtb_v=_lane_vec(dt_bias[l], ALPHA_OFF),
        dn_out_norm=row(dn_out_norm[l]),
        q_nope_norm=row(q_nope_norm[l]),
        qr_gain=jnp.tile(row(q_rope_norm[l]), (1, LANES // QK_ROPE)),
        kv_norm=row(kv_norm[l]),
        kr_gain=_lane_vec(k_rope_norm[l], KR_OFF),
        k_nope_norm=row(k_nope_norm[l]),
        w_uk=w_uk[l].astype(BF16),
        w_uv=w_uv[l].astype(BF16),
        w_o_dn=w_o_dn[l].astype(BF16),
        w_o_mla=w_o_mla[l].astype(BF16),
        w_out=w_out[l].astype(BF16),
    )

    rows = B + Bs
    rows_pad = -(-rows // 8) * 8
    c_all = jnp.concatenate([c_prompt, c_sample, jnp.zeros((rows_pad - rows, D_MODEL), F32)], axis=0)
    mod = _ada(c_all, w_ada[l], b_ada[l].reshape(1, -1))

    zeros_conv = jnp.zeros((B, CONV_W - 1, DN_CONV_CH), F32)
    zeros_state = jnp.zeros((B, DN_HEADS, DN_DK, DN_DV), F32)
    yp, cvp, sdp, kvp, krp = _layer(x_prompt, mod[:B], zeros_conv, zeros_state, None, prm,
                                    bb=1, tm=256, q_off=0, attn_tile=512,
                                    dn_tiles=dict(tm=128, bg=B, G=2))
    ys, cvs, sds, kvs, krs = _layer(x_sample, mod[B:rows], state_conv[l], state_delta[l],
                                    (cache_ckv[l], cache_krope[l]), prm,
                                    bb=Bs, tm=Ts, q_off=past_len, attn_tile=None,
                                    dn_tiles=dict(tm=Ts, bg=2, G=1))
    st = lambda a: a[None]
    return (yp, ys, st(kvp), st(krp), st(sdp), st(cvp), st(kvs), st(krs), st(sds), st(cvs))
```

```python
import functools
import math

import jax
import jax.numpy as jnp
from jax import lax
from jax.experimental import pallas as pl
from jax.experimental.pallas import tpu as pltpu

D_MODEL = 1024
CHUNK = 64
EPS = 1e-6
DN_HEADS = 8
DN_DK = 128
DN_DV = 128
DN_QK = DN_HEADS * DN_DK
DN_V = DN_HEADS * DN_DV
DN_CONV_CH = 2 * DN_QK + DN_V
CONV_W = 4
MLA_HEADS = 8
QK_NOPE = 128
QK_ROPE = 64
QK_HEAD = QK_NOPE + QK_ROPE
V_HEAD = 128
KV_RANK = 512
MLA_Q = MLA_HEADS * QK_HEAD
MLA_V = MLA_HEADS * V_HEAD
ROPE_THETA = 10000.0

LANES = 128
QK_CAT = 256
ATTN_SUB = 128
KR_OFF = 0
BETA_OFF = QK_ROPE
ALPHA_OFF = QK_ROPE + DN_HEADS
VMEM_LIMIT = 56 * 1024 * 1024

F32 = jnp.float32
BF16 = jnp.bfloat16
HI = lax.Precision.HIGHEST


def _dot(a, b):
    return jnp.dot(a, b, preferred_element_type=F32)


def _dot_nt(a, b, precision=None):
    return lax.dot_general(a, b, (((1,), (1,)), ((), ())), preferred_element_type=F32, precision=precision)


def _dot_tn(a, b):
    return lax.dot_general(a, b, (((0,), (0,)), ((), ())), preferred_element_type=F32)


def _cparams(sem):
    return pltpu.CompilerParams(dimension_semantics=sem, vmem_limit_bytes=VMEM_LIMIT)


def _resident(shape):
    nd = len(shape)
    return pl.BlockSpec(shape, lambda *_: (0,) * nd, pipeline_mode=pl.Buffered(1))


def _ada_kernel(c_ref, w_ref, b_ref, o_ref):
    o_ref[...] = jnp.dot(c_ref[...], w_ref[...], preferred_element_type=F32, precision=HI) + b_ref[...]


def _ada(c_all, w_ada, b_ada):
    rows = c_all.shape[0]
    tn = 512
    return pl.pallas_call(
        _ada_kernel,
        grid=(3 * D_MODEL // tn,),
        in_specs=[pl.BlockSpec((rows, D_MODEL), lambda j: (0, 0)),
                  pl.BlockSpec((D_MODEL, tn), lambda j: (0, j)),
                  pl.BlockSpec((1, tn), lambda j: (0, j))],
        out_specs=pl.BlockSpec((rows, tn), lambda j: (0, j)),
        out_shape=jax.ShapeDtypeStruct((rows, 3 * D_MODEL), F32),
        compiler_params=_cparams(("arbitrary",)),
        name="ada",
    )(c_all, w_ada, b_ada)


_PROJ_GROUPS = (("qkv", DN_CONV_CH, BF16), ("z_a", DN_V, BF16), ("qn", MLA_HEADS * QK_NOPE, BF16),
                ("qr", MLA_HEADS * QK_ROPE, BF16), ("z_b", MLA_V, BF16), ("g_a", D_MODEL, BF16),
                ("g_b", D_MODEL, BF16), ("ckv", KV_RANK, F32), ("small", LANES, F32))
_PROJ_WIDTH = sum(w for _, w, _ in _PROJ_GROUPS)


def _pack_w_in(w_in):
    o = 0
    qkv = w_in[:, o:o + DN_CONV_CH]; o += DN_CONV_CH
    z_a = w_in[:, o:o + DN_V]; o += DN_V
    beta = w_in[:, o:o + DN_HEADS]; o += DN_HEADS
    alpha = w_in[:, o:o + DN_HEADS]; o += DN_HEADS
    q = w_in[:, o:o + MLA_Q].reshape(D_MODEL, MLA_HEADS, QK_HEAD); o += MLA_Q
    ckv = w_in[:, o:o + KV_RANK]; o += KV_RANK
    kr = w_in[:, o:o + QK_ROPE]; o += QK_ROPE
    z_b = w_in[:, o:o + MLA_V]; o += MLA_V
    g_a = w_in[:, o:o + D_MODEL]; o += D_MODEL
    g_b = w_in[:, o:o + D_MODEL]
    qn = q[:, :, :QK_NOPE].reshape(D_MODEL, MLA_HEADS * QK_NOPE)
    qr = q[:, :, QK_NOPE:].reshape(D_MODEL, MLA_HEADS * QK_ROPE)
    pad = jnp.zeros((D_MODEL, LANES - QK_ROPE - 2 * DN_HEADS), w_in.dtype)
    small = jnp.concatenate([kr, beta, alpha, pad], axis=1)
    return jnp.concatenate([qkv, z_a, qn, qr, z_b, g_a, g_b, ckv, small], axis=1).astype(BF16)


def _in_proj_kernel(x_ref, mod_ref, gain_ref, w_ref, *out_refs):
    bb, tm, d = x_ref.shape
    x = x_ref[...]
    ms = jnp.mean(x * x, axis=-1, keepdims=True)
    y = x * lax.rsqrt(ms + EPS) * gain_ref[...]
    shift = mod_ref[:, :, 0:d]
    scale = mod_ref[:, :, d:2 * d]
    h = (y * (1.0 + scale) + shift).astype(BF16).reshape(bb * tm, d)
    o = 0
    for (_, width, dtype), ref in zip(_PROJ_GROUPS, out_refs):
        ref[...] = _dot(h, w_ref[:, o:o + width]).astype(dtype).reshape(bb, tm, width)
        o += width


def _in_proj(x, mod3, gain, w_pack, bb, tm):
    B, T, _ = x.shape
    row = lambda b, t: (b, t, 0)
    return pl.pallas_call(
        _in_proj_kernel,
        grid=(B // bb, T // tm),
        in_specs=[pl.BlockSpec((bb, tm, D_MODEL), row),
                  pl.BlockSpec((bb, 1, 3 * D_MODEL), lambda b, t: (b, 0, 0)),
                  _resident((1, D_MODEL)),
                  _resident((D_MODEL, _PROJ_WIDTH))],
        out_specs=[pl.BlockSpec((bb, tm, w), row) for _, w, _ in _PROJ_GROUPS],
        out_shape=[jax.ShapeDtypeStruct((B, T, w), dt) for _, w, dt in _PROJ_GROUPS],
        compiler_params=_cparams(("arbitrary", "arbitrary")),
        name="in_proj",
    )(x, mod3, gain, w_pack)


def _softplus(x):
    return jnp.maximum(x, 0.0) + jnp.log(1.0 + jnp.exp(-jnp.abs(x)))


def _dn_prep_kernel(qkv_ref, prev_ref, cs_ref, small_ref, wconv_ref, alog_ref, dtb_ref,
                    wq_ref, u_ref, kd_ref, attn_ref, egl_ref, buf_ref, *, C):
    t = pl.program_id(1)
    tm = qkv_ref.shape[1]
    nc = tm // C
    pad = prev_ref.shape[1]

    buf_ref[0:pad, :] = jnp.where(t == 0, cs_ref[0], prev_ref[0].astype(F32))
    buf_ref[pad:pad + tm, :] = qkv_ref[0].astype(F32)
    conv = buf_ref[pad - 3:pad - 3 + tm, :] * wconv_ref[0:1, :]
    for i in range(1, CONV_W):
        conv = conv + buf_ref[pad - 3 + i:pad - 3 + i + tm, :] * wconv_ref[i:i + 1, :]
    act = conv * jax.nn.sigmoid(conv)

    sm = small_ref[0]
    beta_all = jax.nn.sigmoid(sm)
    g_all = -jnp.exp(alog_ref[...]) * _softplus(sm + dtb_ref[...])
    rt = lax.broadcasted_iota(jnp.int32, (tm, tm), 0)
    ct = lax.broadcasted_iota(jnp.int32, (tm, tm), 1)
    chunk_tri = ((rt // C == ct // C) & (rt >= ct)).astype(F32)
    gcum = jnp.dot(chunk_tri, g_all, preferred_element_type=F32, precision=HI)
    sel = (lax.broadcasted_iota(jnp.int32, (DN_HEADS, LANES), 1)
           == lax.broadcasted_iota(jnp.int32, (DN_HEADS, LANES), 0) + ALPHA_OFF).astype(F32)
    gcum_t = _dot_nt(sel, gcum, precision=HI)

    ri = lax.broadcasted_iota(jnp.int32, (C, C), 0)
    ci = lax.broadcasted_iota(jnp.int32, (C, C), 1)
    tri_incl = ri >= ci
    tri_strict = ri > ci
    eye = (ri == ci).astype(F32)
    pair_masks = []
    m = 1
    while m < C:
        pair_masks.append((ri // (2 * m) == ci // (2 * m)) & (ri // m != ci // m))
        m *= 2

    heads = range(DN_HEADS)
    qn, kn = [], []
    for h in heads:
        qh = act[:, h * DN_DK:(h + 1) * DN_DK]
        kh = act[:, DN_QK + h * DN_DK:DN_QK + (h + 1) * DN_DK]
        qn.append(qh * lax.rsqrt(jnp.sum(qh * qh, axis=-1, keepdims=True) + EPS) * (DN_DK ** -0.5))
        kn.append(kh * lax.rsqrt(jnp.sum(kh * kh, axis=-1, keepdims=True) + EPS))

    attn_ref[...] = jnp.zeros_like(attn_ref)
    items = [(c, h) for c in range(nc) for h in heads]
    rows = lambda c: slice(c * C, (c + 1) * C)
    decay, rhs, qk = [], [], []
    for c, h in items:
        r = rows(c)
        gc = gcum[r, ALPHA_OFF + h:ALPHA_OFF + h + 1]
        glast = gcum[c * C + C - 1:(c + 1) * C, ALPHA_OFF + h:ALPHA_OFF + h + 1]
        eg = jnp.exp(gc)
        bcol = beta_all[r, BETA_OFF + h:BETA_OFF + h + 1]
        decay.append(jnp.exp(jnp.where(tri_incl, gc - gcum_t[h:h + 1, r], -1e30)))
        kh = kn[h][r]
        kbh = kh * bcol
        vh = act[r, 2 * DN_QK + h * DN_DV:2 * DN_QK + (h + 1) * DN_DV]
        rhs.append(jnp.concatenate([kbh * eg, vh * bcol], axis=1).astype(BF16))
        lo = h * DN_DK
        wq_ref[0, c, C:2 * C, lo:lo + DN_DK] = (qn[h][r] * eg).astype(BF16)
        kd_ref[0, r, lo:lo + DN_DK] = (kh * jnp.exp(glast - gc)).astype(BF16)
        qk.append(_dot_nt(jnp.concatenate([kbh, qn[h][r]], axis=0).astype(BF16), kh.astype(BF16)))
    for c in range(nc):
        egl_ref[0, c] = jnp.exp(jnp.broadcast_to(gcum_t[:, c * C + C - 1:(c + 1) * C], (DN_HEADS, LANES)))

    lmat = []
    for i, (c, h) in enumerate(items):
        lmat.append(jnp.where(tri_strict, qk[i][:C] * decay[i], 0.0))
        lo = h * DN_DK
        attn_ref[0, rows(c), lo:lo + C] = (qk[i][C:] * decay[i]).astype(BF16)

    pinv = [eye - jnp.where(pair_masks[0], l, 0.0) for l in lmat]
    for mask in pair_masks[1:]:
        p16 = [p.astype(BF16) for p in pinv]
        tmp = [_dot(p16[i], jnp.where(mask, lmat[i], 0.0).astype(BF16)).astype(BF16) for i in range(len(items))]
        pinv = [pinv[i] - _dot(tmp[i], p16[i]) for i in range(len(items))]

    for i, (c, h) in enumerate(items):
        wu = _dot(pinv[i].astype(BF16), rhs[i])
        lo = h * DN_DK
        wq_ref[0, c, 0:C, lo:lo + DN_DK] = wu[:, :DN_DK].astype(BF16)
        u_ref[0, rows(c), lo:lo + DN_DV] = wu[:, DN_DK:]


def _dn_scan_kernel(wq_ref, u_ref, kd_ref, attn_ref, egl_ref, za_ref, s0_ref, onorm_ref,
                    ua_ref, sfin_ref, s_ref, *, C):
    n = pl.program_id(1)
    bg, G = wq_ref.shape[0], wq_ref.shape[1]

    @pl.when(n == 0)
    def _():
        s_ref[...] = s0_ref[...]

    chains = [(b, h) for b in range(bg) for h in range(DN_HEADS)]
    for g in range(G):
        r = slice(g * C, (g + 1) * C)
        s_old = [s_ref[b, h] for b, h in chains]
        s16 = [s.astype(BF16) for s in s_old]
        ws = [_dot(wq_ref[b, g, :, h * DN_DK:(h + 1) * DN_DK], s16[i]) for i, (b, h) in enumerate(chains)]
        v16 = [(u_ref[b, r, h * DN_DV:(h + 1) * DN_DV] - ws[i][:C]).astype(BF16)
               for i, (b, h) in enumerate(chains)]
        for i, (b, h) in enumerate(chains):
            lo = h * DN_DK
            s_ref[b, h] = s_old[i] * egl_ref[b, g, h:h + 1, :] + _dot_tn(kd_ref[b, r, lo:lo + DN_DK], v16[i])
        for i, (b, h) in enumerate(chains):
            lo = h * DN_DV
            o = ws[i][C:] + _dot(attn_ref[b, r, lo:lo + C], v16[i])
            o = o * lax.rsqrt(jnp.mean(o * o, axis=-1, keepdims=True) + EPS) * onorm_ref[...]
            z = za_ref[b, r, lo:lo + DN_DV].astype(F32)
            ua_ref[b, r, lo:lo + DN_DV] = (o * (z * jax.nn.sigmoid(z))).astype(BF16)

    @pl.when(n == pl.num_programs(1) - 1)
    def _():
        sfin_ref[...] = s_ref[...]


def _deltanet(qkv, small, z_a, conv_state, s0, w_conv, alog_v, dtb_v, onorm, tm, bg, G):
    B, T, _ = qkv.shape
    C = min(CHUNK, T)
    N = T // C
    nc = tm // C
    hist_rows = 16
    cs = jnp.pad(conv_state, ((0, 0), (hist_rows - (CONV_W - 1), 0), (0, 0)))
    tile = lambda b, t: (b, t, 0)
    prev = lambda b, t: (b, jnp.maximum(t * (tm // hist_rows) - 1, 0), 0)
    wq, u, kd, attn, egl = pl.pallas_call(
        functools.partial(_dn_prep_kernel, C=C),
        grid=(B, T // tm),
        in_specs=[pl.BlockSpec((1, tm, DN_CONV_CH), tile),
                  pl.BlockSpec((1, hist_rows, DN_CONV_CH), prev),
                  pl.BlockSpec((1, hist_rows, DN_CONV_CH), lambda b, t: (b, 0, 0)),
                  pl.BlockSpec((1, tm, LANES), tile),
                  _resident((CONV_W, DN_CONV_CH)),
                  _resident((1, LANES)),
                  _resident((1, LANES))],
        out_specs=[pl.BlockSpec((1, nc, 2 * C, DN_QK), lambda b, t: (b, t, 0, 0)),
                   pl.BlockSpec((1, tm, DN_V), tile),
                   pl.BlockSpec((1, tm, DN_QK), tile),
                   pl.BlockSpec((1, tm, DN_V), tile),
                   pl.BlockSpec((1, nc, DN_HEADS, LANES), lambda b, t: (b, t, 0, 0))],
        out_shape=[jax.ShapeDtypeStruct((B, N, 2 * C, DN_QK), BF16),
                   jax.ShapeDtypeStruct((B, T, DN_V), F32),
                   jax.ShapeDtypeStruct((B, T, DN_QK), BF16),
                   jax.ShapeDtypeStruct((B, T, DN_V), BF16),
                   jax.ShapeDtypeStruct((B, N, DN_HEADS, LANES), F32)],
        scratch_shapes=[pltpu.VMEM((tm + hist_rows, DN_CONV_CH), F32)],
        compiler_params=_cparams(("arbitrary", "arbitrary")),
        name="dn_prep",
    )(qkv, qkv, cs, small, w_conv, alog_v, dtb_v)

    grp = lambda b, n: (b, n, 0)
    grp4 = lambda b, n: (b, n, 0, 0)
    state = pl.BlockSpec((bg, DN_HEADS, DN_DK, DN_DV), lambda b, n: (b, 0, 0, 0))
    u_a, s_new = pl.pallas_call(
        functools.partial(_dn_scan_kernel, C=C),
        grid=(B // bg, N // G),
        in_specs=[pl.BlockSpec((bg, G, 2 * C, DN_QK), grp4),
                  pl.BlockSpec((bg, G * C, DN_V), grp),
                  pl.BlockSpec((bg, G * C, DN_QK), grp),
                  pl.BlockSpec((bg, G * C, DN_V), grp),
                  pl.BlockSpec((bg, G, DN_HEADS, LANES), grp4),
                  pl.BlockSpec((bg, G * C, DN_V), grp),
                  state,
                  _resident((1, DN_DV))],
        out_specs=[pl.BlockSpec((bg, G * C, DN_V), grp), state],
        out_shape=[jax.ShapeDtypeStruct((B, T, DN_V), BF16),
                   jax.ShapeDtypeStruct((B, DN_HEADS, DN_DK, DN_DV), F32)],
        scratch_shapes=[pltpu.VMEM((bg, DN_HEADS, DN_DK, DN_DV), F32)],
        compiler_params=_cparams(("arbitrary", "arbitrary")),
        name="dn_scan",
    )(wq, u, kd, attn, egl, z_a, s0, onorm)
    conv_new = qkv[:, T - (CONV_W - 1):, :].astype(F32)
    return u_a, s_new, conv_new


def _head_rms(x, gain_row, width):
    outs = []
    for h in range(x.shape[1] // width):
        xh = x[:, h * width:(h + 1) * width]
        outs.append(xh * lax.rsqrt(jnp.mean(xh * xh, axis=-1, keepdims=True) + EPS) * gain_row)
    return outs


def _kv_up(ckv16, kr_pad16, wuk_ref, wuv_ref, knorm_ref, kcat_ref, v_ref, shape3):
    bb, tm = shape3
    kn = _head_rms(_dot(ckv16, wuk_ref[...]), knorm_ref[...], QK_NOPE)
    for h in range(MLA_HEADS):
        kcat_ref[:, :, h * QK_CAT:h * QK_CAT + QK_NOPE] = kn[h].astype(BF16).reshape(bb, tm, QK_NOPE)
        kcat_ref[:, :, h * QK_CAT + QK_NOPE:(h + 1) * QK_CAT] = kr_pad16.reshape(bb, tm, LANES)
    v_ref[...] = _dot(ckv16, wuv_ref[...]).astype(BF16).reshape(bb, tm, MLA_V)


def _rope_angles(rows, tm, t0, q_off, lanes):
    pos = (lax.broadcasted_iota(jnp.int32, (rows, lanes), 0) % tm + t0 + q_off).astype(F32)
    half = QK_ROPE // 2
    fidx = (lax.broadcasted_iota(jnp.int32, (rows, lanes), 1) % half).astype(F32)
    inv = jnp.exp(fidx * (-math.log(ROPE_THETA) / half))
    ang = pos * inv
    return jnp.cos(ang), jnp.sin(ang)


def _mla_prep_kernel(qn_ref, qr_ref, ckv_ref, small_ref, qng_ref, qrg_ref, kvg_ref, krg_ref, kng_ref,
                     wuk_ref, wuv_ref, qcat_ref, kcat_ref, v_ref, ckvn_ref, krn_ref, *, q_off):
    bb, tm, _ = qn_ref.shape
    rows = bb * tm
    t0 = pl.program_id(1) * tm
    scale = QK_HEAD ** -0.5 * math.log2(math.e)
    cos, sin = _rope_angles(rows, tm, t0, q_off, LANES)
    lane = lax.broadcasted_iota(jnp.int32, (rows, LANES), 1)
    low_half = lane < QK_ROPE
    first = (lane % QK_ROPE) < (QK_ROPE // 2)

    def rope(y):
        rot = jnp.where(first, -pltpu.roll(y, LANES - QK_ROPE // 2, 1), pltpu.roll(y, QK_ROPE // 2, 1))
        return y * cos + rot * sin

    def rms64(x):
        xx = x * x
        s_lo = jnp.sum(jnp.where(low_half, xx, 0.0), axis=-1, keepdims=True)
        s_hi = jnp.sum(jnp.where(low_half, 0.0, xx), axis=-1, keepdims=True)
        return lax.rsqrt(jnp.where(low_half, s_lo, s_hi) * (1.0 / QK_ROPE) + EPS)

    qn = _head_rms(qn_ref[...].astype(F32).reshape(rows, MLA_HEADS * QK_NOPE), qng_ref[...] * scale, QK_NOPE)
    qr_all = qr_ref[...].astype(F32).reshape(rows, MLA_HEADS * QK_ROPE)
    for c in range(MLA_HEADS // 2):
        x = qr_all[:, c * LANES:(c + 1) * LANES]
        y = rope(x * rms64(x) * qrg_ref[...]) * scale
        even = jnp.where(low_half, y, 0.0)
        odd = jnp.where(low_half, pltpu.roll(y, QK_ROPE, 1), 0.0)
        for h, part in ((2 * c, even), (2 * c + 1, odd)):
            qcat_ref[:, :, h * QK_CAT:h * QK_CAT + QK_NOPE] = qn[h].astype(BF16).reshape(bb, tm, QK_NOPE)
            qcat_ref[:, :, h * QK_CAT + QK_NOPE:(h + 1) * QK_CAT] = part.astype(BF16).reshape(bb, tm, LANES)

    sm = small_ref[...].reshape(rows, LANES)
    kr = rope(sm * rms64(sm) * krg_ref[...])
    krn_ref[...] = kr[:, :QK_ROPE].reshape(bb, tm, QK_ROPE)
    kr_pad16 = jnp.where(low_half, kr, 0.0).astype(BF16)

    ckv = ckv_ref[...].reshape(rows, KV_RANK)
    ckvn = ckv * lax.rsqrt(jnp.mean(ckv * ckv, axis=-1, keepdims=True) + EPS) * kvg_ref[...]
    ckvn_ref[...] = ckvn.reshape(bb, tm, KV_RANK)
    _kv_up(ckvn.astype(BF16), kr_pad16, wuk_ref, wuv_ref, kng_ref, kcat_ref, v_ref, (bb, tm))


def _mla_prep(qn_raw, qr_raw, ckv_raw, small, qn_gain, qr_gain, kv_gain, kr_gain, kn_gain, w_uk16, w_uv16,
              bb, tm, q_off):
    B, T, _ = qn_raw.shape
    row = lambda b, t: (b, t, 0)
    widths_in = (MLA_HEADS * QK_NOPE, MLA_HEADS * QK_ROPE, KV_RANK, LANES)
    outs = ((MLA_HEADS * QK_CAT, BF16), (MLA_HEADS * QK_CAT, BF16), (MLA_V, BF16), (KV_RANK, F32), (QK_ROPE, F32))
    return pl.pallas_call(
        functools.partial(_mla_prep_kernel, q_off=q_off),
        grid=(B // bb, T // tm),
        in_specs=[pl.BlockSpec((bb, tm, w), row) for w in widths_in]
        + [_resident((1, QK_NOPE)), _resident((1, LANES)), _resident((1, KV_RANK)), _resident((1, LANES)),
           _resident((1, QK_NOPE)), _resident((KV_RANK, MLA_HEADS * QK_NOPE)), _resident((KV_RANK, MLA_V))],
        out_specs=[pl.BlockSpec((bb, tm, w), row) for w, _ in outs],
        out_shape=[jax.ShapeDtypeStruct((B, T, w), dt) for w, dt in outs],
        compiler_params=_cparams(("arbitrary", "arbitrary")),
        name="mla_prep",
    )(qn_raw, qr_raw, ckv_raw, small, qn_gain, qr_gain, kv_gain, kr_gain, kn_gain, w_uk16, w_uv16)


def _kv_up_kernel(ckv_ref, kr_ref, kng_ref, wuk_ref, wuv_ref, kcat_ref, v_ref):
    bb, tm, _ = ckv_ref.shape
    ckv16 = ckv_ref[...].reshape(bb * tm, KV_RANK).astype(BF16)
    kr_pad16 = kr_ref[...].reshape(bb * tm, LANES).astype(BF16)
    _kv_up(ckv16, kr_pad16, wuk_ref, wuv_ref, kng_ref, kcat_ref, v_ref, (bb, tm))


def _kv_up_cached(ckv, kr_pad, kn_gain, w_uk16, w_uv16, tm):
    B, T, _ = ckv.shape
    row = lambda b, t: (b, t, 0)
    return pl.pallas_call(
        _kv_up_kernel,
        grid=(B, T // tm),
        in_specs=[pl.BlockSpec((1, tm, KV_RANK), row), pl.BlockSpec((1, tm, LANES), row),
                  _resident((1, QK_NOPE)), _resident((KV_RANK, MLA_HEADS * QK_NOPE)),
                  _resident((KV_RANK, MLA_V))],
        out_specs=[pl.BlockSpec((1, tm, MLA_HEADS * QK_CAT), row), pl.BlockSpec((1, tm, MLA_V), row)],
        out_shape=[jax.ShapeDtypeStruct((B, T, MLA_HEADS * QK_CAT), BF16),
                   jax.ShapeDtypeStruct((B, T, MLA_V), BF16)],
        compiler_params=_cparams(("arbitrary", "arbitrary")),
        name="kv_up",
    )(ckv, kr_pad, kn_gain, w_uk16, w_uv16)


def _chunk_mask(qpos0, kpos0, tq, tk):
    qc = (lax.broadcasted_iota(jnp.int32, (tq, tk), 0) + qpos0) // CHUNK
    kc = (lax.broadcasted_iota(jnp.int32, (tq, tk), 1) + kpos0) // CHUNK
    return kc <= qc


def _attn_prompt_kernel(q_ref, k_ref, v_ref, o_ref, m_ref, l_ref, acc_ref, *, tile, sub):
    i = pl.program_id(2)
    nsub = tile // sub
    m_ref[...] = jnp.full_like(m_ref, -1e30)
    l_ref[...] = jnp.zeros_like(l_ref)
    acc_ref[...] = jnp.zeros_like(acc_ref)

    def step(j, masked):
        start = pl.multiple_of(j * tile, tile)
        k = k_ref[0, pl.ds(start, tile), :]
        v = v_ref[0, pl.ds(start, tile), :]
        s = [_dot_nt(q_ref[0, r * sub:(r + 1) * sub, :], k) for r in range(nsub)]
        for r in range(nsub):
            rows = slice(r * sub, (r + 1) * sub)
            sr = s[r]
            if masked:
                sr = jnp.where(_chunk_mask(r * sub, 0, sub, tile), sr, -1e30)
            m_old = m_ref[rows, :]
            m_new = jnp.maximum(m_old, jnp.max(sr, axis=-1, keepdims=True))
            alpha = jnp.exp2(m_old - m_new)
            p = jnp.exp2(sr - jnp.tile(m_new, (1, tile // LANES)))
            psum = p[:, 0:LANES]
            for c in range(1, tile // LANES):
                psum = psum + p[:, c * LANES:(c + 1) * LANES]
            l_ref[rows, :] = alpha * l_ref[rows, :] + psum
            acc_ref[rows, :] = alpha * acc_ref[rows, :] + _dot(p.astype(BF16), v)
            m_ref[rows, :] = m_new

    def body(j, carry):
        step(j, False)
        return carry

    lax.fori_loop(0, i, body, 0)
    step(i, True)
    l = jnp.sum(l_ref[...], axis=-1, keepdims=True)
    o_ref[0] = (acc_ref[...] / l).astype(BF16)


def _attn_prompt(qcat, kcat, v, tile):
    B, T, _ = v.shape
    return pl.pallas_call(
        functools.partial(_attn_prompt_kernel, tile=tile, sub=ATTN_SUB),
        grid=(B, MLA_HEADS, T // tile),
        in_specs=[pl.BlockSpec((1, tile, QK_CAT), lambda b, h, i: (b, i, h)),
                  pl.BlockSpec((1, T, QK_CAT), lambda b, h, i: (b, 0, h)),
                  pl.BlockSpec((1, T, V_HEAD), lambda b, h, i: (b, 0, h))],
        out_specs=pl.BlockSpec((1, tile, V_HEAD), lambda b, h, i: (b, i, h)),
        out_shape=jax.ShapeDtypeStruct((B, T, MLA_V), BF16),
        scratch_shapes=[pltpu.VMEM((tile, LANES), F32), pltpu.VMEM((tile, LANES), F32),
                        pltpu.VMEM((tile, V_HEAD), F32)],
        compiler_params=_cparams(("arbitrary", "arbitrary", "arbitrary")),
        name="attn_prompt",
    )(qcat, kcat, v)


def _attn_sample_kernel(q_ref, kp_ref, vp_ref, kn_ref, vn_ref, o_ref, *, past_len):
    q = q_ref[0]
    tq = q.shape[0]
    tp = kp_ref.shape[1]
    s1 = jnp.where(_chunk_mask(past_len, 0, tq, tp), _dot_nt(q, kp_ref[0]), -1e30)
    s2 = jnp.where(_chunk_mask(past_len, past_len, tq, tq), _dot_nt(q, kn_ref[0]), -1e30)
    m = jnp.maximum(jnp.max(s1, axis=-1, keepdims=True), jnp.max(s2, axis=-1, keepdims=True))
    p1 = jnp.exp2(s1 - m)
    p2 = jnp.exp2(s2 - m)
    l = jnp.sum(p1, axis=-1, keepdims=True) + jnp.sum(p2, axis=-1, keepdims=True)
    acc = _dot(p1.astype(BF16), vp_ref[0]) + _dot(p2.astype(BF16), vn_ref[0])
    o_ref[0] = (acc / l).astype(BF16)


def _attn_sample(qcat, kcat_past, v_past, kcat_new, v_new):
    B, T, _ = v_new.shape
    P = v_past.shape[1]
    bh = lambda b, h: (b, 0, h)
    return pl.pallas_call(
        functools.partial(_attn_sample_kernel, past_len=P),
        grid=(B, MLA_HEADS),
        in_specs=[pl.BlockSpec((1, T, QK_CAT), bh), pl.BlockSpec((1, P, QK_CAT), bh),
                  pl.BlockSpec((1, P, V_HEAD), bh), pl.BlockSpec((1, T, QK_CAT), bh),
                  pl.BlockSpec((1, T, V_HEAD), bh)],
        out_specs=pl.BlockSpec((1, T, V_HEAD), bh),
        out_shape=jax.ShapeDtypeStruct((B, T, MLA_V), BF16),
        compiler_params=_cparams(("arbitrary", "arbitrary")),
        name="attn_sample",
    )(qcat, kcat_past, v_past, kcat_new, v_new)


def _out_kernel(x_ref, mod_ref, ua_ref, ob_ref, zb_ref, ga_ref, gb_ref, wdn_ref, wmla_ref, wout_ref, y_ref):
    bb, tm, d = x_ref.shape
    rows = bb * tm
    zb = zb_ref[...].astype(F32)
    ub = (ob_ref[...].astype(F32) * (zb * jax.nn.sigmoid(zb))).astype(BF16).reshape(rows, d)
    ya = _dot(ua_ref[...].reshape(rows, d), wdn_ref[...])
    yb = _dot(ub, wmla_ref[...])
    ga = jax.nn.sigmoid(ga_ref[...].astype(F32)).reshape(rows, d)
    gb = jax.nn.sigmoid(gb_ref[...].astype(F32)).reshape(rows, d)
    merged = (ga * ya + gb * yb).astype(BF16)
    out = _dot(merged, wout_ref[...]).reshape(bb, tm, d)
    gate = mod_ref[:, :, 2 * d:3 * d]
    y_ref[...] = x_ref[...] + gate * out


def _out_proj(x, mod3, u_a, o_b, z_b, g_a, g_b, w_dn16, w_mla16, w_out16, bb, tm):
    B, T, _ = x.shape
    row = lambda b, t: (b, t, 0)
    act = pl.BlockSpec((bb, tm, D_MODEL), row)
    return pl.pallas_call(
        _out_kernel,
        grid=(B // bb, T // tm),
        in_specs=[act, pl.BlockSpec((bb, 1, 3 * D_MODEL), lambda b, t: (b, 0, 0)), act, act, act, act, act,
                  _resident((D_MODEL, D_MODEL)), _resident((D_MODEL, D_MODEL)), _resident((D_MODEL, D_MODEL))],
        out_specs=act,
        out_shape=jax.ShapeDtypeStruct((B, T, D_MODEL), F32),
        compiler_params=_cparams(("arbitrary", "arbitrary")),
        name="out_proj",
    )(x, mod3, u_a, o_b, z_b, g_a, g_b, w_dn16, w_mla16, w_out16)


def _lane_vec(v, off):
    return jnp.zeros((1, LANES), F32).at[0, off:off + v.shape[0]].set(v)


def _layer(x, mod, conv_state, s0, past, prm, bb, tm, q_off, attn_tile, dn_tiles):
    B, T, _ = x.shape
    mod3 = mod.reshape(B, 1, 3 * D_MODEL)
    qkv, z_a, qn_raw, qr_raw, z_b, g_a, g_b, ckv_raw, small = _in_proj(
        x, mod3, prm["norm_gain"], prm["w_pack"], bb, tm)
    u_a, s_new, conv_new = _deltanet(qkv, small, z_a, conv_state, s0, prm["w_conv"], prm["alog_v"],
                                     prm["dtb_v"], prm["dn_out_norm"], **dn_tiles)
    qcat, kcat, v, ckv_new, kr_new = _mla_prep(
        qn_raw, qr_raw, ckv_raw, small, prm["q_nope_norm"], prm["qr_gain"], prm["kv_norm"], prm["kr_gain"],
        prm["k_nope_norm"], prm["w_uk"], prm["w_uv"], bb, tm, q_off)
    if past is None:
        o_b = _attn_prompt(qcat, kcat, v, attn_tile)
    else:
        past_ckv, past_kr = past
        kr_pad = jnp.pad(past_kr, ((0, 0), (0, 0), (0, LANES - QK_ROPE)))
        kcat_p, v_p = _kv_up_cached(past_ckv, kr_pad, prm["k_nope_norm"], prm["w_uk"], prm["w_uv"], 512)
        o_b = _attn_sample(qcat, kcat_p, v_p, kcat, v)
    y = _out_proj(x, mod3, u_a, o_b, z_b, g_a, g_b, prm["w_o_dn"], prm["w_o_mla"], prm["w_out"], bb, tm)
    return y, conv_new, s_new, ckv_new, kr_new


def kernel(x_prompt, x_sample, c_prompt, c_sample, cache_ckv, cache_krope, state_delta, state_conv, norm_gain, w_ada, b_ada, w_in, w_conv, a_log, dt_bias, dn_out_norm, q_nope_norm, q_rope_norm, k_nope_norm, k_rope_norm, kv_norm, w_uk, w_uv, w_o_dn, w_o_mla, w_out):
    depth = w_in.shape[0]
    assert depth == 1, "single-layer configuration"
    l = 0
    B, T, _ = x_prompt.shape
    Bs, Ts, _ = x_sample.shape
    past_len = cache_ckv.shape[2]

    row = lambda v: v.reshape(1, -1).astype(F32)
    prm = dict(
        norm_gain=row(norm_gain[l]),
        w_pack=_pack_w_in(w_in[l]),
        w_conv=w_conv[l],
        alog_v=_lane_vec(a_log[l], ALPHA_OFF),
        dtb_v=_lane_vec(dt_bias[l], ALPHA_OFF),
        dn_out_norm=row(dn_out_norm[l]),
        q_nope_norm=row(q_nope_norm[l]),
        qr_gain=jnp.tile(row(q_rope_norm[l]), (1, LANES // QK_ROPE)),
        kv_norm=row(kv_norm[l]),
        kr_gain=_lane_vec(k_rope_norm[l], KR_OFF),
        k_nope_norm=row(k_nope_norm[l]),
        w_uk=w_uk[l].astype(BF16),
        w_uv=w_uv[l].astype(BF16),
        w_o_dn=w_o_dn[l].astype(BF16),
        w_o_mla=w_o_mla[l].astype(BF16),
        w_out=w_out[l].astype(BF16),
    )

    rows = B + Bs
    rows_pad = -(-rows // 8) * 8
    c_all = jnp.concatenate([c_prompt, c_sample, jnp.zeros((rows_pad - rows, D_MODEL), F32)], axis=0)
    mod = _ada(c_all, w_ada[l], b_ada[l].reshape(1, -1))

    zeros_conv = jnp.zeros((B, CONV_W - 1, DN_CONV_CH), F32)
    zeros_state = jnp.zeros((B, DN_HEADS, DN_DK, DN_DV), F32)
    yp, cvp, sdp, kvp, krp = _layer(x_prompt, mod[:B], zeros_conv, zeros_state, None, prm,
                                    bb=1, tm=256, q_off=0, attn_tile=512,
                                    dn_tiles=dict(tm=128, bg=B, G=2))
    ys, cvs, sds, kvs, krs = _layer(x_sample, mod[B:rows], state_conv[l], state_delta[l],
                                    (cache_ckv[l], cache_krope[l]), prm,
                                    bb=Bs, tm=Ts, q_off=past_len, attn_tile=None,
                                    dn_tiles=dict(tm=Ts, bg=2, G=1))
    st = lambda a: a[None]
    return (yp, ys, st(kvp), st(krp), st(sdp), st(cvp), st(kvs), st(krs), st(sds), st(cvs))
```

```python
import functools
import math

import jax
import jax.numpy as jnp
from jax import lax
from jax.experimental import pallas as pl
from jax.experimental.pallas import tpu as pltpu

D_MODEL = 1024
CHUNK = 64
EPS = 1e-6
DN_HEADS = 8
DN_DK = 128
DN_DV = 128
DN_QK = DN_HEADS * DN_DK
DN_V = DN_HEADS * DN_DV
DN_CONV_CH = 2 * DN_QK + DN_V
CONV_W = 4
MLA_HEADS = 8
QK_NOPE = 128
QK_ROPE = 64
QK_HEAD = QK_NOPE + QK_ROPE
V_HEAD = 128
KV_RANK = 512
MLA_Q = MLA_HEADS * QK_HEAD
MLA_V = MLA_HEADS * V_HEAD
ROPE_THETA = 10000.0

LANES = 128
QK_CAT = 256
ATTN_SUB = 128
ATTN_LOOKAHEAD = 4
KR_OFF = 0
BETA_OFF = QK_ROPE
ALPHA_OFF = QK_ROPE + DN_HEADS
VMEM_LIMIT = 56 * 1024 * 1024

F32 = jnp.float32
BF16 = jnp.bfloat16
HI = lax.Precision.HIGHEST


def _dot(a, b):
    return jnp.dot(a, b, preferred_element_type=F32)


def _dot_nt(a, b, precision=None):
    return lax.dot_general(a, b, (((1,), (1,)), ((), ())), preferred_element_type=F32, precision=precision)


def _dot_tn(a, b):
    return lax.dot_general(a, b, (((0,), (0,)), ((), ())), preferred_element_type=F32)


def _cparams(sem):
    return pltpu.CompilerParams(dimension_semantics=sem, vmem_limit_bytes=VMEM_LIMIT)


def _resident(shape):
    nd = len(shape)
    return pl.BlockSpec(shape, lambda *_: (0,) * nd, pipeline_mode=pl.Buffered(1))


def _ada_kernel(c_ref, w_ref, b_ref, o_ref):
    o_ref[...] = jnp.dot(c_ref[...], w_ref[...], preferred_element_type=F32, precision=HI) + b_ref[...]


def _ada(c_all, w_ada, b_ada):
    rows = c_all.shape[0]
    tn = 512
    return pl.pallas_call(
        _ada_kernel,
        grid=(3 * D_MODEL // tn,),
        in_specs=[pl.BlockSpec((rows, D_MODEL), lambda j: (0, 0)),
                  pl.BlockSpec((D_MODEL, tn), lambda j: (0, j)),
                  pl.BlockSpec((1, tn), lambda j: (0, j))],
        out_specs=pl.BlockSpec((rows, tn), lambda j: (0, j)),
        out_shape=jax.ShapeDtypeStruct((rows, 3 * D_MODEL), F32),
        compiler_params=_cparams(("arbitrary",)),
        name="ada",
    )(c_all, w_ada, b_ada)


_PROJ_GROUPS = (("qkv", DN_CONV_CH, BF16), ("z_a", DN_V, BF16), ("qn", MLA_HEADS * QK_NOPE, BF16),
                ("qr", MLA_HEADS * QK_ROPE, BF16), ("z_b", MLA_V, BF16), ("g_a", D_MODEL, BF16),
                ("g_b", D_MODEL, BF16), ("ckv", KV_RANK, F32), ("small", LANES, F32))
_PROJ_WIDTH = sum(w for _, w, _ in _PROJ_GROUPS)


def _pack_w_in(w_in):
    o = 0
    qkv = w_in[:, o:o + DN_CONV_CH]; o += DN_CONV_CH
    z_a = w_in[:, o:o + DN_V]; o += DN_V
    beta = w_in[:, o:o + DN_HEADS]; o += DN_HEADS
    alpha = w_in[:, o:o + DN_HEADS]; o += DN_HEADS
    q = w_in[:, o:o + MLA_Q].reshape(D_MODEL, MLA_HEADS, QK_HEAD); o += MLA_Q
    ckv = w_in[:, o:o + KV_RANK]; o += KV_RANK
    kr = w_in[:, o:o + QK_ROPE]; o += QK_ROPE
    z_b = w_in[:, o:o + MLA_V]; o += MLA_V
    g_a = w_in[:, o:o + D_MODEL]; o += D_MODEL
    g_b = w_in[:, o:o + D_MODEL]
    qn = q[:, :, :QK_NOPE].reshape(D_MODEL, MLA_HEADS * QK_NOPE)
    qr = q[:, :, QK_NOPE:].reshape(D_MODEL, MLA_HEADS * QK_ROPE)
    pad = jnp.zeros((D_MODEL, LANES - QK_ROPE - 2 * DN_HEADS), w_in.dtype)
    small = jnp.concatenate([kr, beta, alpha, pad], axis=1)
    return jnp.concatenate([qkv, z_a, qn, qr, z_b, g_a, g_b, ckv, small], axis=1).astype(BF16)


def _in_proj_kernel(x_ref, mod_ref, gain_ref, w_ref, *out_refs):
    bb, tm, d = x_ref.shape
    x = x_ref[...]
    ms = jnp.mean(x * x, axis=-1, keepdims=True)
    y = x * lax.rsqrt(ms + EPS) * gain_ref[...]
    shift = mod_ref[:, :, 0:d]
    scale = mod_ref[:, :, d:2 * d]
    h = (y * (1.0 + scale) + shift).astype(BF16).reshape(bb * tm, d)
    o = 0
    for (_, width, dtype), ref in zip(_PROJ_GROUPS, out_refs):
        ref[...] = _dot(h, w_ref[:, o:o + width]).astype(dtype).reshape(bb, tm, width)
        o += width


def _in_proj(x, mod3, gain, w_pack, bb, tm):
    B, T, _ = x.shape
    row = lambda b, t: (b, t, 0)
    return pl.pallas_call(
        _in_proj_kernel,
        grid=(B // bb, T // tm),
        in_specs=[pl.BlockSpec((bb, tm, D_MODEL), row),
                  pl.BlockSpec((bb, 1, 3 * D_MODEL), lambda b, t: (b, 0, 0)),
                  _resident((1, D_MODEL)),
                  _resident((D_MODEL, _PROJ_WIDTH))],
        out_specs=[pl.BlockSpec((bb, tm, w), row) for _, w, _ in _PROJ_GROUPS],
        out_shape=[jax.ShapeDtypeStruct((B, T, w), dt) for _, w, dt in _PROJ_GROUPS],
        compiler_params=_cparams(("arbitrary", "arbitrary")),
        name="in_proj",
    )(x, mod3, gain, w_pack)


def _softplus(x):
    return jnp.maximum(x, 0.0) + jnp.log(1.0 + jnp.exp(-jnp.abs(x)))


def _dn_prep_kernel(qkv_ref, prev_ref, cs_ref, small_ref, wconv_ref, alog_ref, dtb_ref,
                    wq_ref, u_ref, kd_ref, attn_ref, egl_ref, buf_ref, *, C):
    t = pl.program_id(1)
    tm = qkv_ref.shape[1]
    nc = tm // C
    pad = prev_ref.shape[1]

    buf_ref[0:pad, :] = jnp.where(t == 0, cs_ref[0], prev_ref[0].astype(F32))
    buf_ref[pad:pad + tm, :] = qkv_ref[0].astype(F32)
    conv = buf_ref[pad - 3:pad - 3 + tm, :] * wconv_ref[0:1, :]
    for i in range(1, CONV_W):
        conv = conv + buf_ref[pad - 3 + i:pad - 3 + i + tm, :] * wconv_ref[i:i + 1, :]
    act = conv * jax.nn.sigmoid(conv)

    sm = small_ref[0]
    beta_all = jax.nn.sigmoid(sm)
    g_all = -jnp.exp(alog_ref[...]) * _softplus(sm + dtb_ref[...])
    rt = lax.broadcasted_iota(jnp.int32, (tm, tm), 0)
    ct = lax.broadcasted_iota(jnp.int32, (tm, tm), 1)
    chunk_tri = ((rt // C == ct // C) & (rt >= ct)).astype(F32)
    gcum = jnp.dot(chunk_tri, g_all, preferred_element_type=F32, precision=HI)
    sel = (lax.broadcasted_iota(jnp.int32, (DN_HEADS, LANES), 1)
           == lax.broadcasted_iota(jnp.int32, (DN_HEADS, LANES), 0) + ALPHA_OFF).astype(F32)
    gcum_t = _dot_nt(sel, gcum, precision=HI)

    ri = lax.broadcasted_iota(jnp.int32, (C, C), 0)
    ci = lax.broadcasted_iota(jnp.int32, (C, C), 1)
    tri_incl = ri >= ci
    tri_strict = ri > ci
    eye = (ri == ci).astype(F32)
    pair_masks = []
    m = 1
    while m < C:
        pair_masks.append((ri // (2 * m) == ci // (2 * m)) & (ri // m != ci // m))
        m *= 2

    heads = range(DN_HEADS)
    qn, kn = [], []
    for h in heads:
        qh = act[:, h * DN_DK:(h + 1) * DN_DK]
        kh = act[:, DN_QK + h * DN_DK:DN_QK + (h + 1) * DN_DK]
        qn.append(qh * lax.rsqrt(jnp.sum(qh * qh, axis=-1, keepdims=True) + EPS) * (DN_DK ** -0.5))
        kn.append(kh * lax.rsqrt(jnp.sum(kh * kh, axis=-1, keepdims=True) + EPS))

    attn_ref[...] = jnp.zeros_like(attn_ref)
    items = [(c, h) for c in range(nc) for h in heads]
    rows = lambda c: slice(c * C, (c + 1) * C)
    decay, rhs, qk = [], [], []
    for c, h in items:
        r = rows(c)
        gc = gcum[r, ALPHA_OFF + h:ALPHA_OFF + h + 1]
        glast = gcum[c * C + C - 1:(c + 1) * C, ALPHA_OFF + h:ALPHA_OFF + h + 1]
        eg = jnp.exp(gc)
        bcol = beta_all[r, BETA_OFF + h:BETA_OFF + h + 1]
        decay.append(jnp.exp(jnp.where(tri_incl, gc - gcum_t[h:h + 1, r], -1e30)))
        kh = kn[h][r]
        kbh = kh * bcol
        vh = act[r, 2 * DN_QK + h * DN_DV:2 * DN_QK + (h + 1) * DN_DV]
        rhs.append(jnp.concatenate([kbh * eg, vh * bcol], axis=1).astype(BF16))
        lo = h * DN_DK
        wq_ref[0, c, C:2 * C, lo:lo + DN_DK] = (qn[h][r] * eg).astype(BF16)
        kd_ref[0, r, lo:lo + DN_DK] = (kh * jnp.exp(glast - gc)).astype(BF16)
        qk.append(_dot_nt(jnp.concatenate([kbh, qn[h][r]], axis=0).astype(BF16), kh.astype(BF16)))
    for c in range(nc):
        egl_ref[0, c] = jnp.exp(jnp.broadcast_to(gcum_t[:, c * C + C - 1:(c + 1) * C], (DN_HEADS, LANES)))

    lmat = []
    for i, (c, h) in enumerate(items):
        lmat.append(jnp.where(tri_strict, qk[i][:C] * decay[i], 0.0))
        lo = h * DN_DK
        attn_ref[0, rows(c), lo:lo + C] = (qk[i][C:] * decay[i]).astype(BF16)

    pinv = [eye - jnp.where(pair_masks[0], l, 0.0) for l in lmat]
    for mask in pair_masks[1:]:
        p16 = [p.astype(BF16) for p in pinv]
        tmp = [_dot(p16[i], jnp.where(mask, lmat[i], 0.0).astype(BF16)).astype(BF16) for i in range(len(items))]
        pinv = [pinv[i] - _dot(tmp[i], p16[i]) for i in range(len(items))]

    for i, (c, h) in enumerate(items):
        wu = _dot(pinv[i].astype(BF16), rhs[i])
        lo = h * DN_DK
        wq_ref[0, c, 0:C, lo:lo + DN_DK] = wu[:, :DN_DK].astype(BF16)
        u_ref[0, rows(c), lo:lo + DN_DV] = wu[:, DN_DK:]


def _dn_scan_kernel(wq_ref, u_ref, kd_ref, attn_ref, egl_ref, za_ref, s0_ref, onorm_ref,
                    ua_ref, sfin_ref, s_ref, *, C):
    n = pl.program_id(1)
    bg, G = wq_ref.shape[0], wq_ref.shape[1]

    @pl.when(n == 0)
    def _():
        s_ref[...] = s0_ref[...]

    chains = [(b, h) for b in range(bg) for h in range(DN_HEADS)]
    for g in range(G):
        r = slice(g * C, (g + 1) * C)
        s_old = [s_ref[b, h] for b, h in chains]
        s16 = [s.astype(BF16) for s in s_old]
        ws = [_dot(wq_ref[b, g, :, h * DN_DK:(h + 1) * DN_DK], s16[i]) for i, (b, h) in enumerate(chains)]
        v16 = [(u_ref[b, r, h * DN_DV:(h + 1) * DN_DV] - ws[i][:C]).astype(BF16)
               for i, (b, h) in enumerate(chains)]
        for i, (b, h) in enumerate(chains):
            lo = h * DN_DK
            s_ref[b, h] = s_old[i] * egl_ref[b, g, h:h + 1, :] + _dot_tn(kd_ref[b, r, lo:lo + DN_DK], v16[i])
        for i, (b, h) in enumerate(chains):
            lo = h * DN_DV
            o = ws[i][C:] + _dot(attn_ref[b, r, lo:lo + C], v16[i])
            o = o * lax.rsqrt(jnp.mean(o * o, axis=-1, keepdims=True) + EPS) * onorm_ref[...]
            z = za_ref[b, r, lo:lo + DN_DV].astype(F32)
            ua_ref[b, r, lo:lo + DN_DV] = (o * (z * jax.nn.sigmoid(z))).astype(BF16)

    @pl.when(n == pl.num_programs(1) - 1)
    def _():
        sfin_ref[...] = s_ref[...]


def _deltanet(qkv, small, z_a, conv_state, s0, w_conv, alog_v, dtb_v, onorm, tm, bg, G):
    B, T, _ = qkv.shape
    C = min(CHUNK, T)
    N = T // C
    nc = tm // C
    hist_rows = 16
    cs = jnp.pad(conv_state, ((0, 0), (hist_rows - (CONV_W - 1), 0), (0, 0)))
    tile = lambda b, t: (b, t, 0)
    prev = lambda b, t: (b, jnp.maximum(t * (tm // hist_rows) - 1, 0), 0)
    wq, u, kd, attn, egl = pl.pallas_call(
        functools.partial(_dn_prep_kernel, C=C),
        grid=(B, T // tm),
        in_specs=[pl.BlockSpec((1, tm, DN_CONV_CH), tile),
                  pl.BlockSpec((1, hist_rows, DN_CONV_CH), prev),
                  pl.BlockSpec((1, hist_rows, DN_CONV_CH), lambda b, t: (b, 0, 0)),
                  pl.BlockSpec((1, tm, LANES), tile),
                  _resident((CONV_W, DN_CONV_CH)),
                  _resident((1, LANES)),
                  _resident((1, LANES))],
        out_specs=[pl.BlockSpec((1, nc, 2 * C, DN_QK), lambda b, t: (b, t, 0, 0)),
                   pl.BlockSpec((1, tm, DN_V), tile),
                   pl.BlockSpec((1, tm, DN_QK), tile),
                   pl.BlockSpec((1, tm, DN_V), tile),
                   pl.BlockSpec((1, nc, DN_HEADS, LANES), lambda b, t: (b, t, 0, 0))],
        out_shape=[jax.ShapeDtypeStruct((B, N, 2 * C, DN_QK), BF16),
                   jax.ShapeDtypeStruct((B, T, DN_V), F32),
                   jax.ShapeDtypeStruct((B, T, DN_QK), BF16),
                   jax.ShapeDtypeStruct((B, T, DN_V), BF16),
                   jax.ShapeDtypeStruct((B, N, DN_HEADS, LANES), F32)],
        scratch_shapes=[pltpu.VMEM((tm + hist_rows, DN_CONV_CH), F32)],
        compiler_params=_cparams(("arbitrary", "arbitrary")),
        name="dn_prep",
    )(qkv, qkv, cs, small, w_conv, alog_v, dtb_v)

    grp = lambda b, n: (b, n, 0)
    grp4 = lambda b, n: (b, n, 0, 0)
    state = pl.BlockSpec((bg, DN_HEADS, DN_DK, DN_DV), lambda b, n: (b, 0, 0, 0))
    u_a, s_new = pl.pallas_call(
        functools.partial(_dn_scan_kernel, C=C),
        grid=(B // bg, N // G),
        in_specs=[pl.BlockSpec((bg, G, 2 * C, DN_QK), grp4),
                  pl.BlockSpec((bg, G * C, DN_V), grp),
                  pl.BlockSpec((bg, G * C, DN_QK), grp),
                  pl.BlockSpec((bg, G * C, DN_V), grp),
                  pl.BlockSpec((bg, G, DN_HEADS, LANES), grp4),
                  pl.BlockSpec((bg, G * C, DN_V), grp),
                  state,
                  _resident((1, DN_DV))],
        out_specs=[pl.BlockSpec((bg, G * C, DN_V), grp), state],
        out_shape=[jax.ShapeDtypeStruct((B, T, DN_V), BF16),
                   jax.ShapeDtypeStruct((B, DN_HEADS, DN_DK, DN_DV), F32)],
        scratch_shapes=[pltpu.VMEM((bg, DN_HEADS, DN_DK, DN_DV), F32)],
        compiler_params=_cparams(("arbitrary", "arbitrary")),
        name="dn_scan",
    )(wq, u, kd, attn, egl, z_a, s0, onorm)
    conv_new = qkv[:, T - (CONV_W - 1):, :].astype(F32)
    return u_a, s_new, conv_new


def _head_rms(x, gain_row, width):
    outs = []
    for h in range(x.shape[1] // width):
        xh = x[:, h * width:(h + 1) * width]
        outs.append(xh * lax.rsqrt(jnp.mean(xh * xh, axis=-1, keepdims=True) + EPS) * gain_row)
    return outs


def _kv_up(ckv16, kr_pad16, wuk_ref, wuv_ref, knorm_ref, kcat_ref, v_ref, shape3):
    bb, tm = shape3
    kn = _head_rms(_dot(ckv16, wuk_ref[...]), knorm_ref[...], QK_NOPE)
    for h in range(MLA_HEADS):
        kcat_ref[:, :, h * QK_CAT:h * QK_CAT + QK_NOPE] = kn[h].astype(BF16).reshape(bb, tm, QK_NOPE)
        kcat_ref[:, :, h * QK_CAT + QK_NOPE:(h + 1) * QK_CAT] = kr_pad16.reshape(bb, tm, LANES)
    v_ref[...] = _dot(ckv16, wuv_ref[...]).astype(BF16).reshape(bb, tm, MLA_V)


def _rope_angles(rows, tm, t0, q_off, lanes):
    pos = (lax.broadcasted_iota(jnp.int32, (rows, lanes), 0) % tm + t0 + q_off).astype(F32)
    half = QK_ROPE // 2
    fidx = (lax.broadcasted_iota(jnp.int32, (rows, lanes), 1) % half).astype(F32)
    inv = jnp.exp(fidx * (-math.log(ROPE_THETA) / half))
    ang = pos * inv
    return jnp.cos(ang), jnp.sin(ang)


def _mla_prep_kernel(qn_ref, qr_ref, ckv_ref, small_ref, qng_ref, qrg_ref, kvg_ref, krg_ref, kng_ref,
                     wuk_ref, wuv_ref, qcat_ref, kcat_ref, v_ref, ckvn_ref, krn_ref, *, q_off):
    bb, tm, _ = qn_ref.shape
    rows = bb * tm
    t0 = pl.program_id(1) * tm
    scale = QK_HEAD ** -0.5 * math.log2(math.e)
    cos, sin = _rope_angles(rows, tm, t0, q_off, LANES)
    lane = lax.broadcasted_iota(jnp.int32, (rows, LANES), 1)
    low_half = lane < QK_ROPE
    first = (lane % QK_ROPE) < (QK_ROPE // 2)

    def rope(y):
        rot = jnp.where(first, -pltpu.roll(y, LANES - QK_ROPE // 2, 1), pltpu.roll(y, QK_ROPE // 2, 1))
        return y * cos + rot * sin

    def rms64(x):
        xx = x * x
        s_lo = jnp.sum(jnp.where(low_half, xx, 0.0), axis=-1, keepdims=True)
        s_hi = jnp.sum(jnp.where(low_half, 0.0, xx), axis=-1, keepdims=True)
        return lax.rsqrt(jnp.where(low_half, s_lo, s_hi) * (1.0 / QK_ROPE) + EPS)

    qn = _head_rms(qn_ref[...].astype(F32).reshape(rows, MLA_HEADS * QK_NOPE), qng_ref[...] * scale, QK_NOPE)
    qr_all = qr_ref[...].astype(F32).reshape(rows, MLA_HEADS * QK_ROPE)
    for c in range(MLA_HEADS // 2):
        x = qr_all[:, c * LANES:(c + 1) * LANES]
        y = rope(x * rms64(x) * qrg_ref[...]) * scale
        even = jnp.where(low_half, y, 0.0)
        odd = jnp.where(low_half, pltpu.roll(y, QK_ROPE, 1), 0.0)
        for h, part in ((2 * c, even), (2 * c + 1, odd)):
            qcat_ref[:, :, h * QK_CAT:h * QK_CAT + QK_NOPE] = qn[h].astype(BF16).reshape(bb, tm, QK_NOPE)
            qcat_ref[:, :, h * QK_CAT + QK_NOPE:(h + 1) * QK_CAT] = part.astype(BF16).reshape(bb, tm, LANES)

    sm = small_ref[...].reshape(rows, LANES)
    kr = rope(sm * rms64(sm) * krg_ref[...])
    krn_ref[...] = kr[:, :QK_ROPE].reshape(bb, tm, QK_ROPE)
    kr_pad16 = jnp.where(low_half, kr, 0.0).astype(BF16)

    ckv = ckv_ref[...].reshape(rows, KV_RANK)
    ckvn = ckv * lax.rsqrt(jnp.mean(ckv * ckv, axis=-1, keepdims=True) + EPS) * kvg_ref[...]
    ckvn_ref[...] = ckvn.reshape(bb, tm, KV_RANK)
    _kv_up(ckvn.astype(BF16), kr_pad16, wuk_ref, wuv_ref, kng_ref, kcat_ref, v_ref, (bb, tm))


def _mla_prep(qn_raw, qr_raw, ckv_raw, small, qn_gain, qr_gain, kv_gain, kr_gain, kn_gain, w_uk16, w_uv16,
              bb, tm, q_off):
    B, T, _ = qn_raw.shape
    row = lambda b, t: (b, t, 0)
    widths_in = (MLA_HEADS * QK_NOPE, MLA_HEADS * QK_ROPE, KV_RANK, LANES)
    outs = ((MLA_HEADS * QK_CAT, BF16), (MLA_HEADS * QK_CAT, BF16), (MLA_V, BF16), (KV_RANK, F32), (QK_ROPE, F32))
    return pl.pallas_call(
        functools.partial(_mla_prep_kernel, q_off=q_off),
        grid=(B // bb, T // tm),
        in_specs=[pl.BlockSpec((bb, tm, w), row) for w in widths_in]
        + [_resident((1, QK_NOPE)), _resident((1, LANES)), _resident((1, KV_RANK)), _resident((1, LANES)),
           _resident((1, QK_NOPE)), _resident((KV_RANK, MLA_HEADS * QK_NOPE)), _resident((KV_RANK, MLA_V))],
        out_specs=[pl.BlockSpec((bb, tm, w), row) for w, _ in outs],
        out_shape=[jax.ShapeDtypeStruct((B, T, w), dt) for w, dt in outs],
        compiler_params=_cparams(("arbitrary", "arbitrary")),
        name="mla_prep",
    )(qn_raw, qr_raw, ckv_raw, small, qn_gain, qr_gain, kv_gain, kr_gain, kn_gain, w_uk16, w_uv16)


def _kv_up_kernel(ckv_ref, kr_ref, kng_ref, wuk_ref, wuv_ref, kcat_ref, v_ref):
    bb, tm, _ = ckv_ref.shape
    ckv16 = ckv_ref[...].reshape(bb * tm, KV_RANK).astype(BF16)
    kr_pad16 = kr_ref[...].reshape(bb * tm, LANES).astype(BF16)
    _kv_up(ckv16, kr_pad16, wuk_ref, wuv_ref, kng_ref, kcat_ref, v_ref, (bb, tm))


def _kv_up_cached(ckv, kr_pad, kn_gain, w_uk16, w_uv16, tm):
    B, T, _ = ckv.shape
    row = lambda b, t: (b, t, 0)
    return pl.pallas_call(
        _kv_up_kernel,
        grid=(B, T // tm),
        in_specs=[pl.BlockSpec((1, tm, KV_RANK), row), pl.BlockSpec((1, tm, LANES), row),
                  _resident((1, QK_NOPE)), _resident((KV_RANK, MLA_HEADS * QK_NOPE)),
                  _resident((KV_RANK, MLA_V))],
        out_specs=[pl.BlockSpec((1, tm, MLA_HEADS * QK_CAT), row), pl.BlockSpec((1, tm, MLA_V), row)],
        out_shape=[jax.ShapeDtypeStruct((B, T, MLA_HEADS * QK_CAT), BF16),
                   jax.ShapeDtypeStruct((B, T, MLA_V), BF16)],
        compiler_params=_cparams(("arbitrary", "arbitrary")),
        name="kv_up",
    )(ckv, kr_pad, kn_gain, w_uk16, w_uv16)


def _chunk_mask(qpos0, kpos0, tq, tk):
    qc = (lax.broadcasted_iota(jnp.int32, (tq, tk), 0) + qpos0) // CHUNK
    kc = (lax.broadcasted_iota(jnp.int32, (tq, tk), 1) + kpos0) // CHUNK
    return kc <= qc


def _attn_prompt_kernel(q_ref, k_ref, v_ref, o_ref, m_ref, l_ref, acc_ref, *, tq, tk, sub):
    i = pl.program_id(2)
    nsub = tq // sub
    ratio = tq // tk
    m_ref[...] = jnp.full_like(m_ref, -1e30)
    l_ref[...] = jnp.zeros_like(l_ref)
    acc_ref[...] = jnp.zeros_like(acc_ref)

    def scores(r, k):
        return _dot_nt(q_ref[0, r * sub:(r + 1) * sub, :], k)

    def softmax_pv(r, sr, v, mask):
        rows = slice(r * sub, (r + 1) * sub)
        if mask is not None:
            sr = jnp.where(mask, sr, -1e30)
        m_old = m_ref[rows, :]
        m_new = jnp.maximum(m_old, jnp.max(sr, axis=-1, keepdims=True))
        alpha = jnp.exp2(m_old - m_new)
        p = jnp.exp2(sr - jnp.tile(m_new, (1, tk // LANES)))
        psum = p[:, 0:LANES]
        for c in range(1, tk // LANES):
            psum = psum + p[:, c * LANES:(c + 1) * LANES]
        l_ref[rows, :] = alpha * l_ref[rows, :] + psum
        acc_ref[rows, :] = alpha * acc_ref[rows, :] + _dot(p.astype(BF16), v)
        m_ref[rows, :] = m_new

    def run(j0, items):
        kv = {}
        for d in sorted({d for d, _, _ in items}):
            start = pl.multiple_of((j0 + d) * tk, tk)
            kv[d] = (k_ref[0, pl.ds(start, tk), :], v_ref[0, pl.ds(start, tk), :])
        s = {n: scores(items[n][1], kv[items[n][0]][0]) for n in range(min(ATTN_LOOKAHEAD, len(items)))}
        for n, (d, r, mask) in enumerate(items):
            ahead = n + ATTN_LOOKAHEAD
            if ahead < len(items):
                s[ahead] = scores(items[ahead][1], kv[items[ahead][0]][0])
            softmax_pv(r, s.pop(n), kv[d][1], mask)

    full = [(d, r, None) for d in range(ratio) for r in range(nsub)]

    def body(jj, carry):
        run(jj * ratio, full)
        return carry

    lax.fori_loop(0, i, body, 0)
    diag = []
    for d in range(ratio):
        for r in range(nsub):
            q_lo, q_hi = (r * sub) // CHUNK, (r * sub + sub - 1) // CHUNK
            k_lo, k_hi = (d * tk) // CHUNK, (d * tk + tk - 1) // CHUNK
            if k_lo > q_hi:
                continue
            diag.append((d, r, None if k_hi <= q_lo else _chunk_mask(r * sub, d * tk, sub, tk)))
    run(i * ratio, diag)
    l = jnp.sum(l_ref[...], axis=-1, keepdims=True)
    o_ref[0] = (acc_ref[...] / l).astype(BF16)


def _attn_prompt(qcat, kcat, v, tq, tk):
    B, T, _ = v.shape
    return pl.pallas_call(
        functools.partial(_attn_prompt_kernel, tq=tq, tk=tk, sub=ATTN_SUB),
        grid=(B, MLA_HEADS, T // tq),
        in_specs=[pl.BlockSpec((1, tq, QK_CAT), lambda b, h, i: (b, i, h)),
                  pl.BlockSpec((1, T, QK_CAT), lambda b, h, i: (b, 0, h)),
                  pl.BlockSpec((1, T, V_HEAD), lambda b, h, i: (b, 0, h))],
        out_specs=pl.BlockSpec((1, tq, V_HEAD), lambda b, h, i: (b, i, h)),
        out_shape=jax.ShapeDtypeStruct((B, T, MLA_V), BF16),
        scratch_shapes=[pltpu.VMEM((tq, LANES), F32), pltpu.VMEM((tq, LANES), F32),
                        pltpu.VMEM((tq, V_HEAD), F32)],
        compiler_params=_cparams(("arbitrary", "arbitrary", "arbitrary")),
        name="attn_prompt",
    )(qcat, kcat, v)


def _attn_sample_kernel(q_ref, kp_ref, vp_ref, kn_ref, vn_ref, o_ref, *, past_len):
    q = q_ref[0]
    tq = q.shape[0]
    tp = kp_ref.shape[1]
    s1 = jnp.where(_chunk_mask(past_len, 0, tq, tp), _dot_nt(q, kp_ref[0]), -1e30)
    s2 = jnp.where(_chunk_mask(past_len, past_len, tq, tq), _dot_nt(q, kn_ref[0]), -1e30)
    m = jnp.maximum(jnp.max(s1, axis=-1, keepdims=True), jnp.max(s2, axis=-1, keepdims=True))
    p1 = jnp.exp2(s1 - m)
    p2 = jnp.exp2(s2 - m)
    l = jnp.sum(p1, axis=-1, keepdims=True) + jnp.sum(p2, axis=-1, keepdims=True)
    acc = _dot(p1.astype(BF16), vp_ref[0]) + _dot(p2.astype(BF16), vn_ref[0])
    o_ref[0] = (acc / l).astype(BF16)


def _attn_sample(qcat, kcat_past, v_past, kcat_new, v_new):
    B, T, _ = v_new.shape
    P = v_past.shape[1]
    bh = lambda b, h: (b, 0, h)
    return pl.pallas_call(
        functools.partial(_attn_sample_kernel, past_len=P),
        grid=(B, MLA_HEADS),
        in_specs=[pl.BlockSpec((1, T, QK_CAT), bh), pl.BlockSpec((1, P, QK_CAT), bh),
                  pl.BlockSpec((1, P, V_HEAD), bh), pl.BlockSpec((1, T, QK_CAT), bh),
                  pl.BlockSpec((1, T, V_HEAD), bh)],
        out_specs=pl.BlockSpec((1, T, V_HEAD), bh),
        out_shape=jax.ShapeDtypeStruct((B, T, MLA_V), BF16),
        compiler_params=_cparams(("arbitrary", "arbitrary")),
        name="attn_sample",
    )(qcat, kcat_past, v_past, kcat_new, v_new)


def _out_kernel(x_ref, mod_ref, ua_ref, ob_ref, zb_ref, ga_ref, gb_ref, wdn_ref, wmla_ref, wout_ref, y_ref):
    bb, tm, d = x_ref.shape
    rows = bb * tm
    zb = zb_ref[...].astype(F32)
    ub = (ob_ref[...].astype(F32) * (zb * jax.nn.sigmoid(zb))).astype(BF16).reshape(rows, d)
    ya = _dot(ua_ref[...].reshape(rows, d), wdn_ref[...])
    yb = _dot(ub, wmla_ref[...])
    ga = jax.nn.sigmoid(ga_ref[...].astype(F32)).reshape(rows, d)
    gb = jax.nn.sigmoid(gb_ref[...].astype(F32)).reshape(rows, d)
    merged = (ga * ya + gb * yb).astype(BF16)
    out = _dot(merged, wout_ref[...]).reshape(bb, tm, d)
    gate = mod_ref[:, :, 2 * d:3 * d]
    y_ref[...] = x_ref[...] + gate * out


def _out_proj(x, mod3, u_a, o_b, z_b, g_a, g_b, w_dn16, w_mla16, w_out16, bb, tm):
    B, T, _ = x.shape
    row = lambda b, t: (b, t, 0)
    act = pl.BlockSpec((bb, tm, D_MODEL), row)
    return pl.pallas_call(
        _out_kernel,
        grid=(B // bb, T // tm),
        in_specs=[act, pl.BlockSpec((bb, 1, 3 * D_MODEL), lambda b, t: (b, 0, 0)), act, act, act, act, act,
                  _resident((D_MODEL, D_MODEL)), _resident((D_MODEL, D_MODEL)), _resident((D_MODEL, D_MODEL))],
        out_specs=act,
        out_shape=jax.ShapeDtypeStruct((B, T, D_MODEL), F32),
        compiler_params=_cparams(("arbitrary", "arbitrary")),
        name="out_proj",
    )(x, mod3, u_a, o_b, z_b, g_a, g_b, w_dn16, w_mla16, w_out16)


def _lane_vec(v, off):
    return jnp.zeros((1, LANES), F32).at[0, off:off + v.shape[0]].set(v)


def _layer(x, mod, conv_state, s0, past, prm, bb, tm, q_off, attn_tile, dn_tiles):
    B, T, _ = x.shape
    mod3 = mod.reshape(B, 1, 3 * D_MODEL)
    qkv, z_a, qn_raw, qr_raw, z_b, g_a, g_b, ckv_raw, small = _in_proj(
        x, mod3, prm["norm_gain"], prm["w_pack"], bb, tm)
    u_a, s_new, conv_new = _deltanet(qkv, small, z_a, conv_state, s0, prm["w_conv"], prm["alog_v"],
                                     prm["dtb_v"], prm["dn_out_norm"], **dn_tiles)
    qcat, kcat, v, ckv_new, kr_new = _mla_prep(
        qn_raw, qr_raw, ckv_raw, small, prm["q_nope_norm"], prm["qr_gain"], prm["kv_norm"], prm["kr_gain"],
        prm["k_nope_norm"], prm["w_uk"], prm["w_uv"], bb, tm, q_off)
    if past is None:
        o_b = _attn_prompt(qcat, kcat, v, *attn_tile)
    else:
        past_ckv, past_kr = past
        kr_pad = jnp.pad(past_kr, ((0, 0), (0, 0), (0, LANES - QK_ROPE)))
        kcat_p, v_p = _kv_up_cached(past_ckv, kr_pad, prm["k_nope_norm"], prm["w_uk"], prm["w_uv"], 512)
        o_b = _attn_sample(qcat, kcat_p, v_p, kcat, v)
    y = _out_proj(x, mod3, u_a, o_b, z_b, g_a, g_b, prm["w_o_dn"], prm["w_o_mla"], prm["w_out"], bb, tm)
    return y, conv_new, s_new, ckv_new, kr_new


def kernel(x_prompt, x_sample, c_prompt, c_sample, cache_ckv, cache_krope, state_delta, state_conv, norm_gain, w_ada, b_ada, w_in, w_conv, a_log, dt_bias, dn_out_norm, q_nope_norm, q_rope_norm, k_nope_norm, k_rope_norm, kv_norm, w_uk, w_uv, w_o_dn, w_o_mla, w_out):
    depth = w_in.shape[0]
    assert depth == 1, "single-layer configuration"
    l = 0
    B, T, _ = x_prompt.shape
    Bs, Ts, _ = x_sample.shape
    past_len = cache_ckv.shape[2]

    row = lambda v: v.reshape(1, -1).astype(F32)
    prm = dict(
        norm_gain=row(norm_gain[l]),
        w_pack=_pack_w_in(w_in[l]),
        w_conv=w_conv[l],
        alog_v=_lane_vec(a_log[l], ALPHA_OFF),
        dtb_v=_lane_vec(dt_bias[l], ALPHA_OFF),
        dn_out_norm=row(dn_out_norm[l]),
        q_nope_norm=row(q_nope_norm[l]),
        qr_gain=jnp.tile(row(q_rope_norm[l]), (1, LANES // QK_ROPE)),
        kv_norm=row(kv_norm[l]),
        kr_gain=_lane_vec(k_rope_norm[l], KR_OFF),
        k_nope_norm=row(k_nope_norm[l]),
        w_uk=w_uk[l].astype(BF16),
        w_uv=w_uv[l].astype(BF16),
        w_o_dn=w_o_dn[l].astype(BF16),
        w_o_mla=w_o_mla[l].astype(BF16),
        w_out=w_out[l].astype(BF16),
    )

    rows = B + Bs
    rows_pad = -(-rows // 8) * 8
    c_all = jnp.concatenate([c_prompt, c_sample, jnp.zeros((rows_pad - rows, D_MODEL), F32)], axis=0)
    mod = _ada(c_all, w_ada[l], b_ada[l].reshape(1, -1))

    zeros_conv = jnp.zeros((B, CONV_W - 1, DN_CONV_CH), F32)
    zeros_state = jnp.zeros((B, DN_HEADS, DN_DK, DN_DV), F32)
    yp, cvp, sdp, kvp, krp = _layer(x_prompt, mod[:B], zeros_conv, zeros_state, None, prm,
                                    bb=1, tm=256, q_off=0, attn_tile=(1024, 512),
                                    dn_tiles=dict(tm=128, bg=B, G=2))
    ys, cvs, sds, kvs, krs = _layer(x_sample, mod[B:rows], state_conv[l], state_delta[l],
                                    (cache_ckv[l], cache_krope[l]), prm,
                                    bb=Bs, tm=Ts, q_off=past_len, attn_tile=None,
                                    dn_tiles=dict(tm=Ts, bg=2, G=1))
    st = lambda a: a[None]
    return (yp, ys, st(kvp), st(krp), st(sdp), st(cvp), st(kvs), st(krs), st(sds), st(cvs))
```

```python
import functools
import math

import jax
import jax.numpy as jnp
from jax import lax
from jax.experimental import pallas as pl
from jax.experimental.pallas import tpu as pltpu

D_MODEL = 1024
CHUNK = 64
EPS = 1e-6
DN_HEADS = 8
DN_DK = 128
DN_DV = 128
DN_QK = DN_HEADS * DN_DK
DN_V = DN_HEADS * DN_DV
DN_CONV_CH = 2 * DN_QK + DN_V
CONV_W = 4
MLA_HEADS = 8
QK_NOPE = 128
QK_ROPE = 64
QK_HEAD = QK_NOPE + QK_ROPE
V_HEAD = 128
KV_RANK = 512
MLA_Q = MLA_HEADS * QK_HEAD
MLA_V = MLA_HEADS * V_HEAD
ROPE_THETA = 10000.0

LANES = 128
QK_CAT = 256
ATTN_SUB = 256
ATTN_LOOKAHEAD = 2
KR_OFF = 0
BETA_OFF = QK_ROPE
ALPHA_OFF = QK_ROPE + DN_HEADS
VMEM_LIMIT = 56 * 1024 * 1024

F32 = jnp.float32
BF16 = jnp.bfloat16
HI = lax.Precision.HIGHEST


def _dot(a, b):
    return jnp.dot(a, b, preferred_element_type=F32)


def _dot_nt(a, b, precision=None):
    return lax.dot_general(a, b, (((1,), (1,)), ((), ())), preferred_element_type=F32, precision=precision)


def _dot_tn(a, b):
    return lax.dot_general(a, b, (((0,), (0,)), ((), ())), preferred_element_type=F32)


def _cparams(sem):
    return pltpu.CompilerParams(dimension_semantics=sem, vmem_limit_bytes=VMEM_LIMIT)


def _resident(shape):
    nd = len(shape)
    return pl.BlockSpec(shape, lambda *_: (0,) * nd, pipeline_mode=pl.Buffered(1))


def _ada_kernel(c_ref, w_ref, b_ref, o_ref):
    o_ref[...] = jnp.dot(c_ref[...], w_ref[...], preferred_element_type=F32, precision=HI) + b_ref[...]


def _ada(c_all, w_ada, b_ada):
    rows = c_all.shape[0]
    tn = 512
    return pl.pallas_call(
        _ada_kernel,
        grid=(3 * D_MODEL // tn,),
        in_specs=[pl.BlockSpec((rows, D_MODEL), lambda j: (0, 0)),
                  pl.BlockSpec((D_MODEL, tn), lambda j: (0, j)),
                  pl.BlockSpec((1, tn), lambda j: (0, j))],
        out_specs=pl.BlockSpec((rows, tn), lambda j: (0, j)),
        out_shape=jax.ShapeDtypeStruct((rows, 3 * D_MODEL), F32),
        compiler_params=_cparams(("arbitrary",)),
        name="ada",
    )(c_all, w_ada, b_ada)


_PROJ_GROUPS = (("qkv", DN_CONV_CH, BF16), ("z_a", DN_V, BF16), ("qn", MLA_HEADS * QK_NOPE, BF16),
                ("qr", MLA_HEADS * QK_ROPE, BF16), ("z_b", MLA_V, BF16), ("g_a", D_MODEL, BF16),
                ("g_b", D_MODEL, BF16), ("ckv", KV_RANK, F32), ("small", LANES, F32))


def _pack_w_in(w_in):
    o = 0
    qkv = w_in[:, o:o + DN_CONV_CH]; o += DN_CONV_CH
    z_a = w_in[:, o:o + DN_V]; o += DN_V
    beta = w_in[:, o:o + DN_HEADS]; o += DN_HEADS
    alpha = w_in[:, o:o + DN_HEADS]; o += DN_HEADS
    q = w_in[:, o:o + MLA_Q].reshape(D_MODEL, MLA_HEADS, QK_HEAD); o += MLA_Q
    ckv = w_in[:, o:o + KV_RANK]; o += KV_RANK
    kr = w_in[:, o:o + QK_ROPE]; o += QK_ROPE
    z_b = w_in[:, o:o + MLA_V]; o += MLA_V
    g_a = w_in[:, o:o + D_MODEL]; o += D_MODEL
    g_b = w_in[:, o:o + D_MODEL]
    qn = q[:, :, :QK_NOPE].reshape(D_MODEL, MLA_HEADS * QK_NOPE)
    qr = q[:, :, QK_NOPE:].reshape(D_MODEL, MLA_HEADS * QK_ROPE)
    pad = jnp.zeros((D_MODEL, LANES - QK_ROPE - 2 * DN_HEADS), w_in.dtype)
    small = jnp.concatenate([kr, beta, alpha, pad], axis=1)
    return tuple(w.astype(BF16) for w in (qkv, z_a, qn, qr, z_b, g_a, g_b, ckv, small))


def _in_proj_kernel(x_ref, mod_ref, gain_ref, *refs):
    bb, tm, d = x_ref.shape
    x = x_ref[...]
    ms = jnp.mean(x * x, axis=-1, keepdims=True)
    y = x * lax.rsqrt(ms + EPS) * gain_ref[...]
    shift = mod_ref[:, :, 0:d]
    scale = mod_ref[:, :, d:2 * d]
    h = (y * (1.0 + scale) + shift).astype(BF16).reshape(bb * tm, d)
    n = len(_PROJ_GROUPS)
    for (_, width, dtype), w_ref, o_ref in zip(_PROJ_GROUPS, refs[:n], refs[n:]):
        o_ref[...] = _dot(h, w_ref[...]).astype(dtype).reshape(bb, tm, width)


def _in_proj(x, mod3, gain, w_pack, bb, tm):
    B, T, _ = x.shape
    row = lambda b, t: (b, t, 0)
    return pl.pallas_call(
        _in_proj_kernel,
        grid=(B // bb, T // tm),
        in_specs=[pl.BlockSpec((bb, tm, D_MODEL), row),
                  pl.BlockSpec((bb, 1, 3 * D_MODEL), lambda b, t: (b, 0, 0)),
                  _resident((1, D_MODEL))]
        + [_resident((D_MODEL, w)) for _, w, _ in _PROJ_GROUPS],
        out_specs=[pl.BlockSpec((bb, tm, w), row) for _, w, _ in _PROJ_GROUPS],
        out_shape=[jax.ShapeDtypeStruct((B, T, w), dt) for _, w, dt in _PROJ_GROUPS],
        compiler_params=_cparams(("arbitrary", "arbitrary")),
        name="in_proj",
    )(x, mod3, gain, *w_pack)


def _softplus(x):
    return jnp.maximum(x, 0.0) + jnp.log(1.0 + jnp.exp(-jnp.abs(x)))


def _dn_prep_kernel(qkv_ref, prev_ref, cs_ref, small_ref, wconv_ref, alog_ref, dtb_ref,
                    wq_ref, u_ref, kd_ref, attn_ref, egl_ref, buf_ref, *, C):
    t = pl.program_id(1)
    tm = qkv_ref.shape[1]
    nc = tm // C
    pad = prev_ref.shape[1]

    buf_ref[0:pad, :] = jnp.where(t == 0, cs_ref[0], prev_ref[0].astype(F32))
    buf_ref[pad:pad + tm, :] = qkv_ref[0].astype(F32)
    conv = buf_ref[pad - 3:pad - 3 + tm, :] * wconv_ref[0:1, :]
    for i in range(1, CONV_W):
        conv = conv + buf_ref[pad - 3 + i:pad - 3 + i + tm, :] * wconv_ref[i:i + 1, :]
    act = conv * jax.nn.sigmoid(conv)

    sm = small_ref[0]
    beta_all = jax.nn.sigmoid(sm)
    g_all = -jnp.exp(alog_ref[...]) * _softplus(sm + dtb_ref[...])
    rt = lax.broadcasted_iota(jnp.int32, (tm, tm), 0)
    ct = lax.broadcasted_iota(jnp.int32, (tm, tm), 1)
    chunk_tri = ((rt // C == ct // C) & (rt >= ct)).astype(F32)
    gcum = jnp.dot(chunk_tri, g_all, preferred_element_type=F32, precision=HI)
    sel = (lax.broadcasted_iota(jnp.int32, (DN_HEADS, LANES), 1)
           == lax.broadcasted_iota(jnp.int32, (DN_HEADS, LANES), 0) + ALPHA_OFF).astype(F32)
    gcum_t = _dot_nt(sel, gcum, precision=HI)

    ri = lax.broadcasted_iota(jnp.int32, (C, C), 0)
    ci = lax.broadcasted_iota(jnp.int32, (C, C), 1)
    tri_incl = ri >= ci
    tri_strict = ri > ci
    eye = (ri == ci).astype(F32)
    pair_masks = []
    m = 1
    while m < C:
        pair_masks.append((ri // (2 * m) == ci // (2 * m)) & (ri // m != ci // m))
        m *= 2

    heads = range(DN_HEADS)
    qn, kn = [], []
    for h in heads:
        qh = act[:, h * DN_DK:(h + 1) * DN_DK]
        kh = act[:, DN_QK + h * DN_DK:DN_QK + (h + 1) * DN_DK]
        qn.append(qh * lax.rsqrt(jnp.sum(qh * qh, axis=-1, keepdims=True) + EPS) * (DN_DK ** -0.5))
        kn.append(kh * lax.rsqrt(jnp.sum(kh * kh, axis=-1, keepdims=True) + EPS))

    attn_ref[...] = jnp.zeros_like(attn_ref)
    items = [(c, h) for c in range(nc) for h in heads]
    rows = lambda c: slice(c * C, (c + 1) * C)
    decay, rhs, qk = [], [], []
    for c, h in items:
        r = rows(c)
        gc = gcum[r, ALPHA_OFF + h:ALPHA_OFF + h + 1]
        glast = gcum[c * C + C - 1:(c + 1) * C, ALPHA_OFF + h:ALPHA_OFF + h + 1]
        eg = jnp.exp(gc)
        bcol = beta_all[r, BETA_OFF + h:BETA_OFF + h + 1]
        decay.append(jnp.exp(jnp.where(tri_incl, gc - gcum_t[h:h + 1, r], -1e30)))
        kh = kn[h][r]
        kbh = kh * bcol
        vh = act[r, 2 * DN_QK + h * DN_DV:2 * DN_QK + (h + 1) * DN_DV]
        rhs.append(jnp.concatenate([kbh * eg, vh * bcol], axis=1).astype(BF16))
        lo = h * DN_DK
        wq_ref[0, c, C:2 * C, lo:lo + DN_DK] = (qn[h][r] * eg).astype(BF16)
        kd_ref[0, r, lo:lo + DN_DK] = (kh * jnp.exp(glast - gc)).astype(BF16)
        qk.append(_dot_nt(jnp.concatenate([kbh, qn[h][r]], axis=0).astype(BF16), kh.astype(BF16)))
    for c in range(nc):
        egl_ref[0, c] = jnp.exp(jnp.broadcast_to(gcum_t[:, c * C + C - 1:(c + 1) * C], (DN_HEADS, LANES)))

    lmat = []
    for i, (c, h) in enumerate(items):
        lmat.append(jnp.where(tri_strict, qk[i][:C] * decay[i], 0.0))
        lo = h * DN_DK
        attn_ref[0, rows(c), lo:lo + C] = (qk[i][C:] * decay[i]).astype(BF16)

    pinv = [eye - jnp.where(pair_masks[0], l, 0.0) for l in lmat]
    for mask in pair_masks[1:]:
        p16 = [p.astype(BF16) for p in pinv]
        tmp = [_dot(p16[i], jnp.where(mask, lmat[i], 0.0).astype(BF16)).astype(BF16) for i in range(len(items))]
        pinv = [pinv[i] - _dot(tmp[i], p16[i]) for i in range(len(items))]

    for i, (c, h) in enumerate(items):
        wu = _dot(pinv[i].astype(BF16), rhs[i])
        lo = h * DN_DK
        wq_ref[0, c, 0:C, lo:lo + DN_DK] = wu[:, :DN_DK].astype(BF16)
        u_ref[0, rows(c), lo:lo + DN_DV] = wu[:, DN_DK:]


def _dn_scan_kernel(wq_ref, u_ref, kd_ref, attn_ref, egl_ref, za_ref, s0_ref, onorm_ref,
                    ua_ref, sfin_ref, s_ref, *, C):
    n = pl.program_id(1)
    bg, G = wq_ref.shape[0], wq_ref.shape[1]

    @pl.when(n == 0)
    def _():
        s_ref[...] = s0_ref[...]

    chains = [(b, h) for b in range(bg) for h in range(DN_HEADS)]
    for g in range(G):
        r = slice(g * C, (g + 1) * C)
        s_old = [s_ref[b, h] for b, h in chains]
        s16 = [s.astype(BF16) for s in s_old]
        ws = [_dot(wq_ref[b, g, :, h * DN_DK:(h + 1) * DN_DK], s16[i]) for i, (b, h) in enumerate(chains)]
        v16 = [(u_ref[b, r, h * DN_DV:(h + 1) * DN_DV] - ws[i][:C]).astype(BF16)
               for i, (b, h) in enumerate(chains)]
        for i, (b, h) in enumerate(chains):
            lo = h * DN_DK
            s_ref[b, h] = s_old[i] * egl_ref[b, g, h:h + 1, :] + _dot_tn(kd_ref[b, r, lo:lo + DN_DK], v16[i])
        for i, (b, h) in enumerate(chains):
            lo = h * DN_DV
            o = ws[i][C:] + _dot(attn_ref[b, r, lo:lo + C], v16[i])
            o = o * lax.rsqrt(jnp.mean(o * o, axis=-1, keepdims=True) + EPS) * onorm_ref[...]
            z = za_ref[b, r, lo:lo + DN_DV].astype(F32)
            ua_ref[b, r, lo:lo + DN_DV] = (o * (z * jax.nn.sigmoid(z))).astype(BF16)

    @pl.when(n == pl.num_programs(1) - 1)
    def _():
        sfin_ref[...] = s_ref[...]


def _deltanet(qkv, small, z_a, conv_state, s0, w_conv, alog_v, dtb_v, onorm, tm, bg, G):
    B, T, _ = qkv.shape
    C = min(CHUNK, T)
    N = T // C
    nc = tm // C
    hist_rows = 16
    cs = jnp.pad(conv_state, ((0, 0), (hist_rows - (CONV_W - 1), 0), (0, 0)))
    tile = lambda b, t: (b, t, 0)
    prev = lambda b, t: (b, jnp.maximum(t * (tm // hist_rows) - 1, 0), 0)
    wq, u, kd, attn, egl = pl.pallas_call(
        functools.partial(_dn_prep_kernel, C=C),
        grid=(B, T // tm),
        in_specs=[pl.BlockSpec((1, tm, DN_CONV_CH), tile),
                  pl.BlockSpec((1, hist_rows, DN_CONV_CH), prev),
                  pl.BlockSpec((1, hist_rows, DN_CONV_CH), lambda b, t: (b, 0, 0)),
                  pl.BlockSpec((1, tm, LANES), tile),
                  _resident((CONV_W, DN_CONV_CH)),
                  _resident((1, LANES)),
                  _resident((1, LANES))],
        out_specs=[pl.BlockSpec((1, nc, 2 * C, DN_QK), lambda b, t: (b, t, 0, 0)),
                   pl.BlockSpec((1, tm, DN_V), tile),
                   pl.BlockSpec((1, tm, DN_QK), tile),
                   pl.BlockSpec((1, tm, DN_V), tile),
                   pl.BlockSpec((1, nc, DN_HEADS, LANES), lambda b, t: (b, t, 0, 0))],
        out_shape=[jax.ShapeDtypeStruct((B, N, 2 * C, DN_QK), BF16),
                   jax.ShapeDtypeStruct((B, T, DN_V), F32),
                   jax.ShapeDtypeStruct((B, T, DN_QK), BF16),
                   jax.ShapeDtypeStruct((B, T, DN_V), BF16),
                   jax.ShapeDtypeStruct((B, N, DN_HEADS, LANES), F32)],
        scratch_shapes=[pltpu.VMEM((tm + hist_rows, DN_CONV_CH), F32)],
        compiler_params=_cparams(("arbitrary", "arbitrary")),
        name="dn_prep",
    )(qkv, qkv, cs, small, w_conv, alog_v, dtb_v)

    grp = lambda b, n: (b, n, 0)
    grp4 = lambda b, n: (b, n, 0, 0)
    state = pl.BlockSpec((bg, DN_HEADS, DN_DK, DN_DV), lambda b, n: (b, 0, 0, 0))
    u_a, s_new = pl.pallas_call(
        functools.partial(_dn_scan_kernel, C=C),
        grid=(B // bg, N // G),
        in_specs=[pl.BlockSpec((bg, G, 2 * C, DN_QK), grp4),
                  pl.BlockSpec((bg, G * C, DN_V), grp),
                  pl.BlockSpec((bg, G * C, DN_QK), grp),
                  pl.BlockSpec((bg, G * C, DN_V), grp),
                  pl.BlockSpec((bg, G, DN_HEADS, LANES), grp4),
                  pl.BlockSpec((bg, G * C, DN_V), grp),
                  state,
                  _resident((1, DN_DV))],
        out_specs=[pl.BlockSpec((bg, G * C, DN_V), grp), state],
        out_shape=[jax.ShapeDtypeStruct((B, T, DN_V), BF16),
                   jax.ShapeDtypeStruct((B, DN_HEADS, DN_DK, DN_DV), F32)],
        scratch_shapes=[pltpu.VMEM((bg, DN_HEADS, DN_DK, DN_DV), F32)],
        compiler_params=_cparams(("arbitrary", "arbitrary")),
        name="dn_scan",
    )(wq, u, kd, attn, egl, z_a, s0, onorm)
    conv_new = qkv[:, T - (CONV_W - 1):, :].astype(F32)
    return u_a, s_new, conv_new


def _head_rms(x, gain_row, width):
    outs = []
    for h in range(x.shape[1] // width):
        xh = x[:, h * width:(h + 1) * width]
        outs.append(xh * lax.rsqrt(jnp.mean(xh * xh, axis=-1, keepdims=True) + EPS) * gain_row)
    return outs


def _kv_up(ckv16, kr_pad16, wuk_ref, wuv_ref, knorm_ref, kcat_ref, v_ref, shape3):
    bb, tm = shape3
    kn = _head_rms(_dot(ckv16, wuk_ref[...]), knorm_ref[...], QK_NOPE)
    for h in range(MLA_HEADS):
        kcat_ref[:, :, h * QK_CAT:h * QK_CAT + QK_NOPE] = kn[h].astype(BF16).reshape(bb, tm, QK_NOPE)
        kcat_ref[:, :, h * QK_CAT + QK_NOPE:(h + 1) * QK_CAT] = kr_pad16.reshape(bb, tm, LANES)
    v_ref[...] = _dot(ckv16, wuv_ref[...]).astype(BF16).reshape(bb, tm, MLA_V)


def _rope_angles(rows, tm, t0, q_off, lanes):
    pos = (lax.broadcasted_iota(jnp.int32, (rows, lanes), 0) % tm + t0 + q_off).astype(F32)
    half = QK_ROPE // 2
    fidx = (lax.broadcasted_iota(jnp.int32, (rows, lanes), 1) % half).astype(F32)
    inv = jnp.exp(fidx * (-math.log(ROPE_THETA) / half))
    ang = pos * inv
    return jnp.cos(ang), jnp.sin(ang)


def _mla_prep_kernel(qn_ref, qr_ref, ckv_ref, small_ref, qng_ref, qrg_ref, kvg_ref, krg_ref, kng_ref,
                     wuk_ref, wuv_ref, qcat_ref, kcat_ref, v_ref, ckvn_ref, krn_ref, *, q_off):
    bb, tm, _ = qn_ref.shape
    rows = bb * tm
    t0 = pl.program_id(1) * tm
    scale = QK_HEAD ** -0.5 * math.log2(math.e)
    cos, sin = _rope_angles(rows, tm, t0, q_off, LANES)
    lane = lax.broadcasted_iota(jnp.int32, (rows, LANES), 1)
    low_half = lane < QK_ROPE
    first = (lane % QK_ROPE) < (QK_ROPE // 2)

    def rope(y):
        rot = jnp.where(first, -pltpu.roll(y, LANES - QK_ROPE // 2, 1), pltpu.roll(y, QK_ROPE // 2, 1))
        return y * cos + rot * sin

    def rms64(x):
        xx = x * x
        s_lo = jnp.sum(jnp.where(low_half, xx, 0.0), axis=-1, keepdims=True)
        s_hi = jnp.sum(jnp.where(low_half, 0.0, xx), axis=-1, keepdims=True)
        return lax.rsqrt(jnp.where(low_half, s_lo, s_hi) * (1.0 / QK_ROPE) + EPS)

    qn = _head_rms(qn_ref[...].astype(F32).reshape(rows, MLA_HEADS * QK_NOPE), qng_ref[...] * scale, QK_NOPE)
    qr_all = qr_ref[...].astype(F32).reshape(rows, MLA_HEADS * QK_ROPE)
    for c in range(MLA_HEADS // 2):
        x = qr_all[:, c * LANES:(c + 1) * LANES]
        y = rope(x * rms64(x) * qrg_ref[...]) * scale
        even = jnp.where(low_half, y, 0.0)
        odd = jnp.where(low_half, pltpu.roll(y, QK_ROPE, 1), 0.0)
        for h, part in ((2 * c, even), (2 * c + 1, odd)):
            qcat_ref[:, :, h * QK_CAT:h * QK_CAT + QK_NOPE] = qn[h].astype(BF16).reshape(bb, tm, QK_NOPE)
            qcat_ref[:, :, h * QK_CAT + QK_NOPE:(h + 1) * QK_CAT] = part.astype(BF16).reshape(bb, tm, LANES)

    sm = small_ref[...].reshape(rows, LANES)
    kr = rope(sm * rms64(sm) * krg_ref[...])
    krn_ref[...] = kr[:, :QK_ROPE].reshape(bb, tm, QK_ROPE)
    kr_pad16 = jnp.where(low_half, kr, 0.0).astype(BF16)

    ckv = ckv_ref[...].reshape(rows, KV_RANK)
    ckvn = ckv * lax.rsqrt(jnp.mean(ckv * ckv, axis=-1, keepdims=True) + EPS) * kvg_ref[...]
    ckvn_ref[...] = ckvn.reshape(bb, tm, KV_RANK)
    _kv_up(ckvn.astype(BF16), kr_pad16, wuk_ref, wuv_ref, kng_ref, kcat_ref, v_ref, (bb, tm))


def _mla_prep(qn_raw, qr_raw, ckv_raw, small, qn_gain, qr_gain, kv_gain, kr_gain, kn_gain, w_uk16, w_uv16,
              bb, tm, q_off):
    B, T, _ = qn_raw.shape
    row = lambda b, t: (b, t, 0)
    widths_in = (MLA_HEADS * QK_NOPE, MLA_HEADS * QK_ROPE, KV_RANK, LANES)
    outs = ((MLA_HEADS * QK_CAT, BF16), (MLA_HEADS * QK_CAT, BF16), (MLA_V, BF16), (KV_RANK, F32), (QK_ROPE, F32))
    return pl.pallas_call(
        functools.partial(_mla_prep_kernel, q_off=q_off),
        grid=(B // bb, T // tm),
        in_specs=[pl.BlockSpec((bb, tm, w), row) for w in widths_in]
        + [_resident((1, QK_NOPE)), _resident((1, LANES)), _resident((1, KV_RANK)), _resident((1, LANES)),
           _resident((1, QK_NOPE)), _resident((KV_RANK, MLA_HEADS * QK_NOPE)), _resident((KV_RANK, MLA_V))],
        out_specs=[pl.BlockSpec((bb, tm, w), row) for w, _ in outs],
        out_shape=[jax.ShapeDtypeStruct((B, T, w), dt) for w, dt in outs],
        compiler_params=_cparams(("arbitrary", "arbitrary")),
        name="mla_prep",
    )(qn_raw, qr_raw, ckv_raw, small, qn_gain, qr_gain, kv_gain, kr_gain, kn_gain, w_uk16, w_uv16)


def _chunk_mask(qpos0, kpos0, tq, tk):
    qc = (lax.broadcasted_iota(jnp.int32, (tq, tk), 0) + qpos0) // CHUNK
    kc = (lax.broadcasted_iota(jnp.int32, (tq, tk), 1) + kpos0) // CHUNK
    return kc <= qc


def _attn_prompt_kernel(q_ref, k_ref, v_ref, o_ref, m_ref, l_ref, acc_ref, *, tq, tk, sub):
    i = pl.program_id(2)
    nsub = tq // sub
    ratio = tq // tk
    m_ref[...] = jnp.full_like(m_ref, -1e30)
    l_ref[...] = jnp.zeros_like(l_ref)
    acc_ref[...] = jnp.zeros_like(acc_ref)

    def scores(r, k):
        return _dot_nt(q_ref[0, r * sub:(r + 1) * sub, :], k)

    def softmax_pv(r, sr, v, mask):
        rows = slice(r * sub, (r + 1) * sub)
        if mask is not None:
            sr = jnp.where(mask, sr, -1e30)
        m_old = m_ref[rows, :]
        m_new = jnp.maximum(m_old, jnp.max(sr, axis=-1, keepdims=True))
        alpha = jnp.exp2(m_old - m_new)
        p = jnp.exp2(sr - jnp.tile(m_new, (1, tk // LANES)))
        psum = p[:, 0:LANES]
        for c in range(1, tk // LANES):
            psum = psum + p[:, c * LANES:(c + 1) * LANES]
        l_ref[rows, :] = alpha * l_ref[rows, :] + psum
        acc_ref[rows, :] = alpha * acc_ref[rows, :] + _dot(p.astype(BF16), v)
        m_ref[rows, :] = m_new

    def run(j0, items):
        kv = {}
        for d in sorted({d for d, _, _ in items}):
            start = pl.multiple_of((j0 + d) * tk, tk)
            kv[d] = (k_ref[0, pl.ds(start, tk), :], v_ref[0, pl.ds(start, tk), :])
        s = {n: scores(items[n][1], kv[items[n][0]][0]) for n in range(min(ATTN_LOOKAHEAD, len(items)))}
        for n, (d, r, mask) in enumerate(items):
            ahead = n + ATTN_LOOKAHEAD
            if ahead < len(items):
                s[ahead] = scores(items[ahead][1], kv[items[ahead][0]][0])
            softmax_pv(r, s.pop(n), kv[d][1], mask)

    full = [(d, r, None) for d in range(ratio) for r in range(nsub)]

    def body(jj, carry):
        run(jj * ratio, full)
        return carry

    lax.fori_loop(0, i, body, 0)
    diag = []
    for d in range(ratio):
        for r in range(nsub):
            q_lo, q_hi = (r * sub) // CHUNK, (r * sub + sub - 1) // CHUNK
            k_lo, k_hi = (d * tk) // CHUNK, (d * tk + tk - 1) // CHUNK
            if k_lo > q_hi:
                continue
            diag.append((d, r, None if k_hi <= q_lo else _chunk_mask(r * sub, d * tk, sub, tk)))
    run(i * ratio, diag)
    l = jnp.sum(l_ref[...], axis=-1, keepdims=True)
    o_ref[0] = (acc_ref[...] / l).astype(BF16)


def _attn_prompt(qcat, kcat, v, tq, tk):
    B, T, _ = v.shape
    return pl.pallas_call(
        functools.partial(_attn_prompt_kernel, tq=tq, tk=tk, sub=ATTN_SUB),
        grid=(B, MLA_HEADS, T // tq),
        in_specs=[pl.BlockSpec((1, tq, QK_CAT), lambda b, h, i: (b, i, h)),
                  pl.BlockSpec((1, T, QK_CAT), lambda b, h, i: (b, 0, h)),
                  pl.BlockSpec((1, T, V_HEAD), lambda b, h, i: (b, 0, h))],
        out_specs=pl.BlockSpec((1, tq, V_HEAD), lambda b, h, i: (b, i, h)),
        out_shape=jax.ShapeDtypeStruct((B, T, MLA_V), BF16),
        scratch_shapes=[pltpu.VMEM((tq, LANES), F32), pltpu.VMEM((tq, LANES), F32),
                        pltpu.VMEM((tq, V_HEAD), F32)],
        compiler_params=_cparams(("arbitrary", "arbitrary", "arbitrary")),
        name="attn_prompt",
    )(qcat, kcat, v)


def _attn_sample_kernel(q_ref, kn_ref, vn_ref, ckv_ref, kr_ref, kng_ref, wuk_ref, wuv_ref, o_ref,
                        m_ref, l_ref, acc_ref, *, tk):
    T = q_ref.shape[1]
    P = ckv_ref.shape[1]
    R = MLA_HEADS * T
    q = q_ref[0]
    row_head = lax.broadcasted_iota(jnp.int32, (R, 1), 0) // T
    q_chunk = (lax.broadcasted_iota(jnp.int32, (R, 1), 0) % T + P) // CHUNK

    def stacked(x, width):
        lane_head = lax.broadcasted_iota(jnp.int32, (R, x.shape[1]), 1) // width
        return jnp.where(lane_head == row_head, jnp.concatenate([x] * MLA_HEADS, axis=0), jnp.zeros((), x.dtype))

    q_nope = stacked(jnp.concatenate([q[:, h * QK_CAT:h * QK_CAT + QK_NOPE] for h in range(MLA_HEADS)], axis=1),
                     QK_NOPE)
    q_rope = jnp.concatenate([q[:, h * QK_CAT + QK_NOPE:h * QK_CAT + QK_HEAD] for h in range(MLA_HEADS)],
                             axis=0)

    def update(s, k_chunk):
        s = jnp.where(k_chunk <= q_chunk, s, -1e30)
        m_old = m_ref[...]
        m_new = jnp.maximum(m_old, jnp.max(s, axis=-1, keepdims=True))
        alpha = jnp.exp2(m_old - m_new)
        p = jnp.exp2(s - m_new)
        l_ref[...] = alpha * l_ref[...] + jnp.sum(p, axis=-1, keepdims=True)
        m_ref[...] = m_new
        return alpha, p.astype(BF16)

    m_ref[...] = jnp.full_like(m_ref, -1e30)
    l_ref[...] = jnp.zeros_like(l_ref)
    acc_ref[...] = jnp.zeros_like(acc_ref)

    def body(c, carry):
        start = pl.multiple_of(c * tk, tk)
        ckv16 = ckv_ref[0, pl.ds(start, tk), :].astype(BF16)
        kn = _head_rms(_dot(ckv16, wuk_ref[...]), kng_ref[...], QK_NOPE)
        kn16 = jnp.concatenate([x.astype(BF16) for x in kn], axis=1)
        s = _dot_nt(q_nope, kn16) + _dot_nt(q_rope, kr_ref[0, pl.ds(start, tk), :].astype(BF16))
        k_chunk = (lax.broadcasted_iota(jnp.int32, (1, tk), 1) + start) // CHUNK
        alpha, p16 = update(s, k_chunk)
        acc_ref[...] = alpha * acc_ref[...] + _dot(p16, ckv16)
        return carry

    lax.fori_loop(0, P // tk, body, 0)

    s_new = _dot_nt(stacked(q, QK_CAT), kn_ref[0])
    alpha, p16 = update(s_new, (lax.broadcasted_iota(jnp.int32, (1, T), 1) + P) // CHUNK)
    pc16 = (alpha * acc_ref[...]).astype(BF16)
    inv_l = 1.0 / l_ref[...]
    for h in range(MLA_HEADS):
        rows = slice(h * T, (h + 1) * T)
        lanes = slice(h * V_HEAD, (h + 1) * V_HEAD)
        o = _dot(pc16[rows], wuv_ref[:, lanes]) + _dot(p16[rows], vn_ref[0, :, lanes])
        o_ref[0, :, lanes] = (o * inv_l[rows]).astype(BF16)


def _attn_sample(qcat, kcat_new, v_new, past_ckv, past_kr, kn_gain, w_uk16, w_uv16, tk):
    B, T, _ = v_new.shape
    P = past_ckv.shape[1]
    R = MLA_HEADS * T
    perb = lambda b: (b, 0, 0)
    return pl.pallas_call(
        functools.partial(_attn_sample_kernel, tk=tk),
        grid=(B,),
        in_specs=[pl.BlockSpec((1, T, MLA_HEADS * QK_CAT), perb),
                  pl.BlockSpec((1, T, MLA_HEADS * QK_CAT), perb),
                  pl.BlockSpec((1, T, MLA_V), perb),
                  pl.BlockSpec((1, P, KV_RANK), perb),
                  pl.BlockSpec((1, P, QK_ROPE), perb),
                  _resident((1, QK_NOPE)),
                  _resident((KV_RANK, MLA_HEADS * QK_NOPE)),
                  _resident((KV_RANK, MLA_V))],
        out_specs=pl.BlockSpec((1, T, MLA_V), perb),
        out_shape=jax.ShapeDtypeStruct((B, T, MLA_V), BF16),
        scratch_shapes=[pltpu.VMEM((R, 1), F32), pltpu.VMEM((R, 1), F32), pltpu.VMEM((R, KV_RANK), F32)],
        compiler_params=_cparams(("arbitrary",)),
        name="attn_sample",
    )(qcat, kcat_new, v_new, past_ckv, past_kr, kn_gain, w_uk16, w_uv16)


def _out_kernel(x_ref, mod_ref, ua_ref, ob_ref, zb_ref, ga_ref, gb_ref, wdn_ref, wmla_ref, wout_ref, y_ref):
    bb, tm, d = x_ref.shape
    rows = bb * tm
    zb = zb_ref[...].astype(F32)
    ub = (ob_ref[...].astype(F32) * (zb * jax.nn.sigmoid(zb))).astype(BF16).reshape(rows, d)
    ya = _dot(ua_ref[...].reshape(rows, d), wdn_ref[...])
    yb = _dot(ub, wmla_ref[...])
    ga = jax.nn.sigmoid(ga_ref[...].astype(F32)).reshape(rows, d)
    gb = jax.nn.sigmoid(gb_ref[...].astype(F32)).reshape(rows, d)
    merged = (ga * ya + gb * yb).astype(BF16)
    out = _dot(merged, wout_ref[...]).reshape(bb, tm, d)
    gate = mod_ref[:, :, 2 * d:3 * d]
    y_ref[...] = x_ref[...] + gate * out


def _out_proj(x, mod3, u_a, o_b, z_b, g_a, g_b, w_dn16, w_mla16, w_out16, bb, tm):
    B, T, _ = x.shape
    row = lambda b, t: (b, t, 0)
    act = pl.BlockSpec((bb, tm, D_MODEL), row)
    return pl.pallas_call(
        _out_kernel,
        grid=(B // bb, T // tm),
        in_specs=[act, pl.BlockSpec((bb, 1, 3 * D_MODEL), lambda b, t: (b, 0, 0)), act, act, act, act, act,
                  _resident((D_MODEL, D_MODEL)), _resident((D_MODEL, D_MODEL)), _resident((D_MODEL, D_MODEL))],
        out_specs=act,
        out_shape=jax.ShapeDtypeStruct((B, T, D_MODEL), F32),
        compiler_params=_cparams(("arbitrary", "arbitrary")),
        name="out_proj",
    )(x, mod3, u_a, o_b, z_b, g_a, g_b, w_dn16, w_mla16, w_out16)


def _lane_vec(v, off):
    return jnp.zeros((1, LANES), F32).at[0, off:off + v.shape[0]].set(v)


def _layer(x, mod, conv_state, s0, past, prm, bb, tm, q_off, attn_tile, dn_tiles):
    B, T, _ = x.shape
    mod3 = mod.reshape(B, 1, 3 * D_MODEL)
    qkv, z_a, qn_raw, qr_raw, z_b, g_a, g_b, ckv_raw, small = _in_proj(
        x, mod3, prm["norm_gain"], prm["w_pack"], bb, tm)
    u_a, s_new, conv_new = _deltanet(qkv, small, z_a, conv_state, s0, prm["w_conv"], prm["alog_v"],
                                     prm["dtb_v"], prm["dn_out_norm"], **dn_tiles)
    qcat, kcat, v, ckv_new, kr_new = _mla_prep(
        qn_raw, qr_raw, ckv_raw, small, prm["q_nope_norm"], prm["qr_gain"], prm["kv_norm"], prm["kr_gain"],
        prm["k_nope_norm"], prm["w_uk"], prm["w_uv"], bb, tm, q_off)
    if past is None:
        o_b = _attn_prompt(qcat, kcat, v, *attn_tile)
    else:
        past_ckv, past_kr = past
        o_b = _attn_sample(qcat, kcat, v, past_ckv, past_kr, prm["k_nope_norm"], prm["w_uk"], prm["w_uv"], 512)
    y = _out_proj(x, mod3, u_a, o_b, z_b, g_a, g_b, prm["w_o_dn"], prm["w_o_mla"], prm["w_out"], bb, tm)
    return y, conv_new, s_new, ckv_new, kr_new


def kernel(x_prompt, x_sample, c_prompt, c_sample, cache_ckv, cache_krope, state_delta, state_conv, norm_gain, w_ada, b_ada, w_in, w_conv, a_log, dt_bias, dn_out_norm, q_nope_norm, q_rope_norm, k_nope_norm, k_rope_norm, kv_norm, w_uk, w_uv, w_o_dn, w_o_mla, w_out):
    depth = w_in.shape[0]
    assert depth == 1, "single-layer configuration"
    l = 0
    B, T, _ = x_prompt.shape
    Bs, Ts, _ = x_sample.shape
    past_len = cache_ckv.shape[2]

    row = lambda v: v.reshape(1, -1).astype(F32)
    prm = dict(
        norm_gain=row(norm_gain[l]),
        w_pack=_pack_w_in(w_in[l]),
        w_conv=w_conv[l],
        alog_v=_lane_vec(a_log[l], ALPHA_OFF),
        dtb_v=_lane_vec(dt_bias[l], ALPHA_OFF),
        dn_out_norm=row(dn_out_norm[l]),
        q_nope_norm=row(q_nope_norm[l]),
        qr_gain=jnp.tile(row(q_rope_norm[l]), (1, LANES // QK_ROPE)),
        kv_norm=row(kv_norm[l]),
        kr_gain=_lane_vec(k_rope_norm[l], KR_OFF),
        k_nope_norm=row(k_nope_norm[l]),
        w_uk=w_uk[l].astype(BF16),
        w_uv=w_uv[l].astype(BF16),
        w_o_dn=w_o_dn[l].astype(BF16),
        w_o_mla=w_o_mla[l].astype(BF16),
        w_out=w_out[l].astype(BF16),
    )

    rows = B + Bs
    rows_pad = -(-rows // 8) * 8
    c_all = jnp.concatenate([c_prompt, c_sample, jnp.zeros((rows_pad - rows, D_MODEL), F32)], axis=0)
    mod = _ada(c_all, w_ada[l], b_ada[l].reshape(1, -1))

    zeros_conv = jnp.zeros((B, CONV_W - 1, DN_CONV_CH), F32)
    zeros_state = jnp.zeros((B, DN_HEADS, DN_DK, DN_DV), F32)
    yp, cvp, sdp, kvp, krp = _layer(x_prompt, mod[:B], zeros_conv, zeros_state, None, prm,
                                    bb=1, tm=256, q_off=0, attn_tile=(1024, 512),
                                    dn_tiles=dict(tm=128, bg=B, G=2))
    ys, cvs, sds, kvs, krs = _layer(x_sample, mod[B:rows], state_conv[l], state_delta[l],
                                    (cache_ckv[l], cache_krope[l]), prm,
                                    bb=Bs, tm=Ts, q_off=past_len, attn_tile=None,
                                    dn_tiles=dict(tm=Ts, bg=2, G=1))
    st = lambda a: a[None]
    return (yp, ys, st(kvp), st(krp), st(sdp), st(cvp), st(kvs), st(krs), st(sds), st(cvs))
```

```python
import functools
import math

import jax
import jax.numpy as jnp
from jax import lax
from jax.experimental import pallas as pl
from jax.experimental.pallas import tpu as pltpu

D_MODEL = 1024
CHUNK = 64
EPS = 1e-6
DN_HEADS = 8
DN_DK = 128
DN_DV = 128
DN_QK = DN_HEADS * DN_DK
DN_V = DN_HEADS * DN_DV
DN_CONV_CH = 2 * DN_QK + DN_V
CONV_W = 4
MLA_HEADS = 8
QK_NOPE = 128
QK_ROPE = 64
QK_HEAD = QK_NOPE + QK_ROPE
V_HEAD = 128
KV_RANK = 512
MLA_Q = MLA_HEADS * QK_HEAD
MLA_V = MLA_HEADS * V_HEAD
ROPE_THETA = 10000.0

LANES = 128
QK_CAT = 256
ATTN_SUB = 256
ATTN_LOOKAHEAD = 2
KR_OFF = 0
BETA_OFF = QK_ROPE
ALPHA_OFF = QK_ROPE + DN_HEADS
VMEM_LIMIT = 56 * 1024 * 1024

F32 = jnp.float32
BF16 = jnp.bfloat16
HI = lax.Precision.HIGHEST


def _dot(a, b):
    return jnp.dot(a, b, preferred_element_type=F32)


def _dot_nt(a, b, precision=None):
    return lax.dot_general(a, b, (((1,), (1,)), ((), ())), preferred_element_type=F32, precision=precision)


def _dot_tn(a, b):
    return lax.dot_general(a, b, (((0,), (0,)), ((), ())), preferred_element_type=F32)


def _cparams(sem):
    return pltpu.CompilerParams(dimension_semantics=sem, vmem_limit_bytes=VMEM_LIMIT)


def _resident(shape):
    nd = len(shape)
    return pl.BlockSpec(shape, lambda *_: (0,) * nd, pipeline_mode=pl.Buffered(1))


def _ada_kernel(c_ref, w_ref, b_ref, o_ref):
    o_ref[...] = jnp.dot(c_ref[...], w_ref[...], preferred_element_type=F32, precision=HI) + b_ref[...]


def _ada(c_all, w_ada, b_ada):
    rows = c_all.shape[0]
    tn = 512
    return pl.pallas_call(
        _ada_kernel,
        grid=(3 * D_MODEL // tn,),
        in_specs=[pl.BlockSpec((rows, D_MODEL), lambda j: (0, 0)),
                  pl.BlockSpec((D_MODEL, tn), lambda j: (0, j)),
                  pl.BlockSpec((1, tn), lambda j: (0, j))],
        out_specs=pl.BlockSpec((rows, tn), lambda j: (0, j)),
        out_shape=jax.ShapeDtypeStruct((rows, 3 * D_MODEL), F32),
        compiler_params=_cparams(("arbitrary",)),
        name="ada",
    )(c_all, w_ada, b_ada)


_PROJ_GROUPS = (("qkv", DN_CONV_CH, BF16), ("z_a", DN_V, BF16), ("qn", MLA_HEADS * QK_NOPE, BF16),
                ("qr", MLA_HEADS * QK_ROPE, BF16), ("z_b", MLA_V, BF16), ("g_a", D_MODEL, BF16),
                ("g_b", D_MODEL, BF16), ("ckv", KV_RANK, F32), ("small", LANES, F32))


def _pack_w_in(w_in):
    o = 0
    qkv = w_in[:, o:o + DN_CONV_CH]; o += DN_CONV_CH
    z_a = w_in[:, o:o + DN_V]; o += DN_V
    beta = w_in[:, o:o + DN_HEADS]; o += DN_HEADS
    alpha = w_in[:, o:o + DN_HEADS]; o += DN_HEADS
    q = w_in[:, o:o + MLA_Q].reshape(D_MODEL, MLA_HEADS, QK_HEAD); o += MLA_Q
    ckv = w_in[:, o:o + KV_RANK]; o += KV_RANK
    kr = w_in[:, o:o + QK_ROPE]; o += QK_ROPE
    z_b = w_in[:, o:o + MLA_V]; o += MLA_V
    g_a = w_in[:, o:o + D_MODEL]; o += D_MODEL
    g_b = w_in[:, o:o + D_MODEL]
    qn = q[:, :, :QK_NOPE].reshape(D_MODEL, MLA_HEADS * QK_NOPE)
    qr = q[:, :, QK_NOPE:].reshape(D_MODEL, MLA_HEADS * QK_ROPE)
    pad = jnp.zeros((D_MODEL, LANES - QK_ROPE - 2 * DN_HEADS), w_in.dtype)
    small = jnp.concatenate([kr, beta, alpha, pad], axis=1)
    return tuple(w.astype(BF16) for w in (qkv, z_a, qn, qr, z_b, g_a, g_b, ckv, small))


def _in_proj_kernel(x_ref, mod_ref, gain_ref, *refs):
    bb, tm, d = x_ref.shape
    x = x_ref[...]
    ms = jnp.mean(x * x, axis=-1, keepdims=True)
    y = x * lax.rsqrt(ms + EPS) * gain_ref[...]
    shift = mod_ref[:, :, 0:d]
    scale = mod_ref[:, :, d:2 * d]
    h = (y * (1.0 + scale) + shift).astype(BF16).reshape(bb * tm, d)
    n = len(_PROJ_GROUPS)
    for (_, width, dtype), w_ref, o_ref in zip(_PROJ_GROUPS, refs[:n], refs[n:]):
        o_ref[...] = _dot(h, w_ref[...]).astype(dtype).reshape(bb, tm, width)


def _in_proj(x, mod3, gain, w_pack, bb, tm):
    B, T, _ = x.shape
    row = lambda b, t: (b, t, 0)
    return pl.pallas_call(
        _in_proj_kernel,
        grid=(B // bb, T // tm),
        in_specs=[pl.BlockSpec((bb, tm, D_MODEL), row),
                  pl.BlockSpec((bb, 1, 3 * D_MODEL), lambda b, t: (b, 0, 0)),
                  _resident((1, D_MODEL))]
        + [_resident((D_MODEL, w)) for _, w, _ in _PROJ_GROUPS],
        out_specs=[pl.BlockSpec((bb, tm, w), row) for _, w, _ in _PROJ_GROUPS],
        out_shape=[jax.ShapeDtypeStruct((B, T, w), dt) for _, w, dt in _PROJ_GROUPS],
        compiler_params=_cparams(("arbitrary", "arbitrary")),
        name="in_proj",
    )(x, mod3, gain, *w_pack)


def _softplus(x):
    return jnp.maximum(x, 0.0) + jnp.log(1.0 + jnp.exp(-jnp.abs(x)))


def _dn_prep_kernel(qkv_ref, prev_ref, cs_ref, small_ref, wconv_ref, alog_ref, dtb_ref,
                    wq_ref, u_ref, kd_ref, attn_ref, egl_ref, *, C):
    t = pl.program_id(1)
    tm = qkv_ref.shape[1]
    nc = tm // C
    pad = prev_ref.shape[1]

    x16 = qkv_ref[0]
    hist = jnp.where(t == 0, cs_ref[0], prev_ref[0].astype(F32))
    hist_hi = hist.astype(BF16)
    rem = hist - hist_hi.astype(F32)
    hist_mid = rem.astype(BF16)
    hist_lo = (rem - hist_mid.astype(F32)).astype(BF16)
    full16 = jnp.concatenate([hist_hi, hist_mid, hist_lo, x16], axis=0)
    n_sh = CONV_W - 1
    srow = lax.broadcasted_iota(jnp.int32, (n_sh * tm, 3 * pad + tm), 0)
    scol = lax.broadcasted_iota(jnp.int32, (n_sh * tm, 3 * pad + tm), 1)
    src = srow % tm + srow // tm + pad - n_sh
    shift_sel = ((scol == src + 2 * pad) | ((scol == src) & (scol < pad))
                 | ((scol == src + pad) & (scol < 2 * pad))).astype(BF16)
    shifted = _dot(shift_sel, full16)
    conv = x16.astype(F32) * wconv_ref[n_sh:CONV_W, :]
    for i in range(n_sh):
        conv = conv + shifted[i * tm:(i + 1) * tm, :] * wconv_ref[i:i + 1, :]
    act = conv * jax.nn.sigmoid(conv)

    sm = small_ref[0]
    beta_all = jax.nn.sigmoid(sm)
    g_all = -jnp.exp(alog_ref[...]) * _softplus(sm + dtb_ref[...])
    rt = lax.broadcasted_iota(jnp.int32, (tm, tm), 0)
    ct = lax.broadcasted_iota(jnp.int32, (tm, tm), 1)
    chunk_tri = ((rt // C == ct // C) & (rt >= ct)).astype(F32)
    gcum = jnp.dot(chunk_tri, g_all, preferred_element_type=F32, precision=HI)
    sel = (lax.broadcasted_iota(jnp.int32, (DN_HEADS, LANES), 1)
           == lax.broadcasted_iota(jnp.int32, (DN_HEADS, LANES), 0) + ALPHA_OFF).astype(F32)
    gcum_t = _dot_nt(sel, gcum, precision=HI)

    def per_head_lanes(x, off):
        hi = x.astype(BF16)
        r1 = x - hi.astype(F32)
        mid = r1.astype(BF16)
        lo = (r1 - mid.astype(F32)).astype(BF16)
        erow = lax.broadcasted_iota(jnp.int32, (3 * LANES, DN_QK), 0) % LANES
        ecol = lax.broadcasted_iota(jnp.int32, (3 * LANES, DN_QK), 1) // DN_DK
        return _dot(jnp.concatenate([hi, mid, lo], axis=1), (erow == ecol + off).astype(BF16))

    g_b = per_head_lanes(gcum, ALPHA_OFF)
    beta_b = per_head_lanes(beta_all, BETA_OFF)
    glast_b = jnp.concatenate(
        [jnp.broadcast_to(g_b[c * C + C - 1:(c + 1) * C, :], (C, DN_QK)) for c in range(nc)], axis=0)
    eg_b = jnp.exp(g_b)
    kdf_b = jnp.exp(glast_b - g_b)
    for c in range(nc):
        egl_ref[0, c] = jnp.exp(g_b[c * C + C - 1:(c + 1) * C, :])

    ri = lax.broadcasted_iota(jnp.int32, (C, C), 0)
    ci = lax.broadcasted_iota(jnp.int32, (C, C), 1)
    tri_incl = ri >= ci
    tri_strict = ri > ci
    eye = (ri == ci).astype(F32)
    pair_masks = []
    m = 1
    while m < C:
        pair_masks.append((ri // (2 * m) == ci // (2 * m)) & (ri // m != ci // m))
        m *= 2

    heads = range(DN_HEADS)
    qn, kn = [], []
    for h in heads:
        qh = act[:, h * DN_DK:(h + 1) * DN_DK]
        kh = act[:, DN_QK + h * DN_DK:DN_QK + (h + 1) * DN_DK]
        qn.append(qh * lax.rsqrt(jnp.sum(qh * qh, axis=-1, keepdims=True) + EPS) * (DN_DK ** -0.5))
        kn.append(kh * lax.rsqrt(jnp.sum(kh * kh, axis=-1, keepdims=True) + EPS))

    hl = lambda h: slice(h * DN_DK, (h + 1) * DN_DK)
    k16 = [kn[h].astype(BF16) for h in heads]
    q16 = [qn[h].astype(BF16) for h in heads]
    kb = [kn[h] * beta_b[:, hl(h)] for h in heads]
    kb16 = [kb[h].astype(BF16) for h in heads]
    kbe16 = [(kb[h] * eg_b[:, hl(h)]).astype(BF16) for h in heads]
    vb16 = [(act[:, 2 * DN_QK + h * DN_DV:2 * DN_QK + (h + 1) * DN_DV] * beta_b[:, hl(h)]).astype(BF16)
            for h in heads]
    qe16 = [(qn[h] * eg_b[:, hl(h)]).astype(BF16) for h in heads]
    for h in heads:
        kd_ref[0, :, hl(h)] = (kn[h] * kdf_b[:, hl(h)]).astype(BF16)

    attn_ref[...] = jnp.zeros_like(attn_ref)
    items = [(c, h) for c in range(nc) for h in heads]
    rows = lambda c: slice(c * C, (c + 1) * C)
    decay, rhs, qk = [], [], []
    for c, h in items:
        r = rows(c)
        gc = g_b[r, h * DN_DK:h * DN_DK + C]
        decay.append(jnp.exp(jnp.where(tri_incl, gc - gcum_t[h:h + 1, r], -1e30)))
        rhs.append(jnp.concatenate([kbe16[h][r], vb16[h][r]], axis=1))
        wq_ref[0, c, C:2 * C, hl(h)] = qe16[h][r]
        qk.append(_dot_nt(jnp.concatenate([kb16[h][r], q16[h][r]], axis=0), k16[h][r]))

    lmat = []
    for i, (c, h) in enumerate(items):
        lmat.append(jnp.where(tri_strict, qk[i][:C] * decay[i], 0.0))
        attn_ref[0, rows(c), h * DN_DK:h * DN_DK + C] = (qk[i][C:] * decay[i]).astype(BF16)

    pinv = [eye - jnp.where(pair_masks[0], l, 0.0) for l in lmat]
    for mask in pair_masks[1:]:
        p16 = [p.astype(BF16) for p in pinv]
        tmp = [_dot(p16[i], jnp.where(mask, lmat[i], 0.0).astype(BF16)).astype(BF16) for i in range(len(items))]
        pinv = [pinv[i] - _dot(tmp[i], p16[i]) for i in range(len(items))]

    for i, (c, h) in enumerate(items):
        wu = _dot(pinv[i].astype(BF16), rhs[i])
        wq_ref[0, c, 0:C, hl(h)] = wu[:, :DN_DK].astype(BF16)
        u_ref[0, rows(c), hl(h)] = wu[:, DN_DK:]


def _dn_scan_kernel(wq_ref, u_ref, kd_ref, attn_ref, egl_ref, za_ref, s0_ref, onorm_ref,
                    ua_ref, sfin_ref, s_ref, *, C):
    n = pl.program_id(1)
    bg, G = wq_ref.shape[0], wq_ref.shape[1]

    @pl.when(n == 0)
    def _():
        s_ref[...] = s0_ref[...]

    chains = [(b, h) for b in range(bg) for h in range(DN_HEADS)]
    for g in range(G):
        r = slice(g * C, (g + 1) * C)
        s_old = [s_ref[b, h] for b, h in chains]
        s16 = [s.astype(BF16) for s in s_old]
        ws = [_dot(wq_ref[b, g, :, h * DN_DK:(h + 1) * DN_DK], s16[i]) for i, (b, h) in enumerate(chains)]
        v16 = [(u_ref[b, r, h * DN_DV:(h + 1) * DN_DV] - ws[i][:C]).astype(BF16)
               for i, (b, h) in enumerate(chains)]
        for i, (b, h) in enumerate(chains):
            lo = h * DN_DK
            s_ref[b, h] = s_old[i] * egl_ref[b, g, :, lo:lo + DN_DK] + _dot_tn(kd_ref[b, r, lo:lo + DN_DK], v16[i])
        for i, (b, h) in enumerate(chains):
            lo = h * DN_DV
            o = ws[i][C:] + _dot(attn_ref[b, r, lo:lo + C], v16[i])
            o = o * lax.rsqrt(jnp.mean(o * o, axis=-1, keepdims=True) + EPS) * onorm_ref[...]
            z = za_ref[b, r, lo:lo + DN_DV].astype(F32)
            ua_ref[b, r, lo:lo + DN_DV] = (o * (z * jax.nn.sigmoid(z))).astype(BF16)

    @pl.when(n == pl.num_programs(1) - 1)
    def _():
        sfin_ref[...] = s_ref[...]


def _deltanet(qkv, small, z_a, conv_state, s0, w_conv, alog_v, dtb_v, onorm, tm, bg, G):
    B, T, _ = qkv.shape
    C = min(CHUNK, T)
    N = T // C
    nc = tm // C
    hist_rows = 16
    cs = jnp.pad(conv_state, ((0, 0), (hist_rows - (CONV_W - 1), 0), (0, 0)))
    tile = lambda b, t: (b, t, 0)
    prev = lambda b, t: (b, jnp.maximum(t * (tm // hist_rows) - 1, 0), 0)
    wq, u, kd, attn, egl = pl.pallas_call(
        functools.partial(_dn_prep_kernel, C=C),
        grid=(B, T // tm),
        in_specs=[pl.BlockSpec((1, tm, DN_CONV_CH), tile),
                  pl.BlockSpec((1, hist_rows, DN_CONV_CH), prev),
                  pl.BlockSpec((1, hist_rows, DN_CONV_CH), lambda b, t: (b, 0, 0)),
                  pl.BlockSpec((1, tm, LANES), tile),
                  _resident((CONV_W, DN_CONV_CH)),
                  _resident((1, LANES)),
                  _resident((1, LANES))],
        out_specs=[pl.BlockSpec((1, nc, 2 * C, DN_QK), lambda b, t: (b, t, 0, 0)),
                   pl.BlockSpec((1, tm, DN_V), tile),
                   pl.BlockSpec((1, tm, DN_QK), tile),
                   pl.BlockSpec((1, tm, DN_V), tile),
                   pl.BlockSpec((1, nc, 1, DN_QK), lambda b, t: (b, t, 0, 0))],
        out_shape=[jax.ShapeDtypeStruct((B, N, 2 * C, DN_QK), BF16),
                   jax.ShapeDtypeStruct((B, T, DN_V), F32),
                   jax.ShapeDtypeStruct((B, T, DN_QK), BF16),
                   jax.ShapeDtypeStruct((B, T, DN_V), BF16),
                   jax.ShapeDtypeStruct((B, N, 1, DN_QK), F32)],
        compiler_params=_cparams(("arbitrary", "arbitrary")),
        name="dn_prep",
    )(qkv, qkv, cs, small, w_conv, alog_v, dtb_v)

    grp = lambda b, n: (b, n, 0)
    grp4 = lambda b, n: (b, n, 0, 0)
    state = pl.BlockSpec((bg, DN_HEADS, DN_DK, DN_DV), lambda b, n: (b, 0, 0, 0))
    u_a, s_new = pl.pallas_call(
        functools.partial(_dn_scan_kernel, C=C),
        grid=(B // bg, N // G),
        in_specs=[pl.BlockSpec((bg, G, 2 * C, DN_QK), grp4),
                  pl.BlockSpec((bg, G * C, DN_V), grp),
                  pl.BlockSpec((bg, G * C, DN_QK), grp),
                  pl.BlockSpec((bg, G * C, DN_V), grp),
                  pl.BlockSpec((bg, G, 1, DN_QK), grp4),
                  pl.BlockSpec((bg, G * C, DN_V), grp),
                  state,
                  _resident((1, DN_DV))],
        out_specs=[pl.BlockSpec((bg, G * C, DN_V), grp), state],
        out_shape=[jax.ShapeDtypeStruct((B, T, DN_V), BF16),
                   jax.ShapeDtypeStruct((B, DN_HEADS, DN_DK, DN_DV), F32)],
        scratch_shapes=[pltpu.VMEM((bg, DN_HEADS, DN_DK, DN_DV), F32)],
        compiler_params=_cparams(("arbitrary", "arbitrary")),
        name="dn_scan",
    )(wq, u, kd, attn, egl, z_a, s0, onorm)
    conv_new = qkv[:, T - (CONV_W - 1):, :].astype(F32)
    return u_a, s_new, conv_new


def _head_rms(x, gain_row, width):
    outs = []
    for h in range(x.shape[1] // width):
        xh = x[:, h * width:(h + 1) * width]
        outs.append(xh * lax.rsqrt(jnp.mean(xh * xh, axis=-1, keepdims=True) + EPS) * gain_row)
    return outs


def _kv_up(ckv16, kr_pad16, wuk_ref, wuv_ref, knorm_ref, kcat_ref, v_ref, shape3):
    bb, tm = shape3
    kn = _head_rms(_dot(ckv16, wuk_ref[...]), knorm_ref[...], QK_NOPE)
    for h in range(MLA_HEADS):
        kcat_ref[:, :, h * QK_CAT:h * QK_CAT + QK_NOPE] = kn[h].astype(BF16).reshape(bb, tm, QK_NOPE)
        kcat_ref[:, :, h * QK_CAT + QK_NOPE:(h + 1) * QK_CAT] = kr_pad16.reshape(bb, tm, LANES)
    v_ref[...] = _dot(ckv16, wuv_ref[...]).astype(BF16).reshape(bb, tm, MLA_V)


def _rope_angles(rows, tm, t0, q_off, lanes):
    pos = (lax.broadcasted_iota(jnp.int32, (rows, lanes), 0) % tm + t0 + q_off).astype(F32)
    half = QK_ROPE // 2
    fidx = (lax.broadcasted_iota(jnp.int32, (rows, lanes), 1) % half).astype(F32)
    inv = jnp.exp(fidx * (-math.log(ROPE_THETA) / half))
    ang = pos * inv
    return jnp.cos(ang), jnp.sin(ang)


def _mla_prep_kernel(qn_ref, qr_ref, ckv_ref, small_ref, qng_ref, qrg_ref, kvg_ref, krg_ref, kng_ref,
                     wuk_ref, wuv_ref, qcat_ref, kcat_ref, v_ref, ckvn_ref, krn_ref, *, q_off):
    bb, tm, _ = qn_ref.shape
    rows = bb * tm
    t0 = pl.program_id(1) * tm
    scale = QK_HEAD ** -0.5 * math.log2(math.e)
    cos, sin = _rope_angles(rows, tm, t0, q_off, LANES)
    lane = lax.broadcasted_iota(jnp.int32, (rows, LANES), 1)
    low_half = lane < QK_ROPE
    first = (lane % QK_ROPE) < (QK_ROPE // 2)

    def rope(y):
        rot = jnp.where(first, -pltpu.roll(y, LANES - QK_ROPE // 2, 1), pltpu.roll(y, QK_ROPE // 2, 1))
        return y * cos + rot * sin

    def rms64(x):
        xx = x * x
        s_lo = jnp.sum(jnp.where(low_half, xx, 0.0), axis=-1, keepdims=True)
        s_hi = jnp.sum(jnp.where(low_half, 0.0, xx), axis=-1, keepdims=True)
        return lax.rsqrt(jnp.where(low_half, s_lo, s_hi) * (1.0 / QK_ROPE) + EPS)

    qn = _head_rms(qn_ref[...].astype(F32).reshape(rows, MLA_HEADS * QK_NOPE), qng_ref[...] * scale, QK_NOPE)
    qr_all = qr_ref[...].astype(F32).reshape(rows, MLA_HEADS * QK_ROPE)
    for c in range(MLA_HEADS // 2):
        x = qr_all[:, c * LANES:(c + 1) * LANES]
        y = rope(x * rms64(x) * qrg_ref[...]) * scale
        even = jnp.where(low_half, y, 0.0)
        odd = jnp.where(low_half, pltpu.roll(y, QK_ROPE, 1), 0.0)
        for h, part in ((2 * c, even), (2 * c + 1, odd)):
            qcat_ref[:, :, h * QK_CAT:h * QK_CAT + QK_NOPE] = qn[h].astype(BF16).reshape(bb, tm, QK_NOPE)
            qcat_ref[:, :, h * QK_CAT + QK_NOPE:(h + 1) * QK_CAT] = part.astype(BF16).reshape(bb, tm, LANES)

    sm = small_ref[...].reshape(rows, LANES)
    kr = rope(sm * rms64(sm) * krg_ref[...])
    krn_ref[...] = kr[:, :QK_ROPE].reshape(bb, tm, QK_ROPE)
    kr_pad16 = jnp.where(low_half, kr, 0.0).astype(BF16)

    ckv = ckv_ref[...].reshape(rows, KV_RANK)
    ckvn = ckv * lax.rsqrt(jnp.mean(ckv * ckv, axis=-1, keepdims=True) + EPS) * kvg_ref[...]
    ckvn_ref[...] = ckvn.reshape(bb, tm, KV_RANK)
    _kv_up(ckvn.astype(BF16), kr_pad16, wuk_ref, wuv_ref, kng_ref, kcat_ref, v_ref, (bb, tm))


def _mla_prep(qn_raw, qr_raw, ckv_raw, small, qn_gain, qr_gain, kv_gain, kr_gain, kn_gain, w_uk16, w_uv16,
              bb, tm, q_off):
    B, T, _ = qn_raw.shape
    row = lambda b, t: (b, t, 0)
    widths_in = (MLA_HEADS * QK_NOPE, MLA_HEADS * QK_ROPE, KV_RANK, LANES)
    outs = ((MLA_HEADS * QK_CAT, BF16), (MLA_HEADS * QK_CAT, BF16), (MLA_V, BF16), (KV_RANK, F32), (QK_ROPE, F32))
    return pl.pallas_call(
        functools.partial(_mla_prep_kernel, q_off=q_off),
        grid=(B // bb, T // tm),
        in_specs=[pl.BlockSpec((bb, tm, w), row) for w in widths_in]
        + [_resident((1, QK_NOPE)), _resident((1, LANES)), _resident((1, KV_RANK)), _resident((1, LANES)),
           _resident((1, QK_NOPE)), _resident((KV_RANK, MLA_HEADS * QK_NOPE)), _resident((KV_RANK, MLA_V))],
        out_specs=[pl.BlockSpec((bb, tm, w), row) for w, _ in outs],
        out_shape=[jax.ShapeDtypeStruct((B, T, w), dt) for w, dt in outs],
        compiler_params=_cparams(("arbitrary", "arbitrary")),
        name="mla_prep",
    )(qn_raw, qr_raw, ckv_raw, small, qn_gain, qr_gain, kv_gain, kr_gain, kn_gain, w_uk16, w_uv16)


def _chunk_mask(qpos0, kpos0, tq, tk):
    qc = (lax.broadcasted_iota(jnp.int32, (tq, tk), 0) + qpos0) // CHUNK
    kc = (lax.broadcasted_iota(jnp.int32, (tq, tk), 1) + kpos0) // CHUNK
    return kc <= qc


def _attn_prompt_kernel(q_ref, k_ref, v_ref, o_ref, m_ref, l_ref, acc_ref, *, tq, tk, sub):
    i = pl.program_id(2)
    nsub = tq // sub
    ratio = tq // tk
    m_ref[...] = jnp.full_like(m_ref, -1e30)
    l_ref[...] = jnp.zeros_like(l_ref)
    acc_ref[...] = jnp.zeros_like(acc_ref)

    def scores(r, k):
        return _dot_nt(q_ref[0, r * sub:(r + 1) * sub, :], k)

    def softmax_pv(r, sr, v, mask):
        rows = slice(r * sub, (r + 1) * sub)
        if mask is not None:
            sr = jnp.where(mask, sr, -1e30)
        m_old = m_ref[rows, :]
        m_new = jnp.maximum(m_old, jnp.max(sr, axis=-1, keepdims=True))
        alpha = jnp.exp2(m_old - m_new)
        p = jnp.exp2(sr - jnp.tile(m_new, (1, tk // LANES)))
        psum = p[:, 0:LANES]
        for c in range(1, tk // LANES):
            psum = psum + p[:, c * LANES:(c + 1) * LANES]
        l_ref[rows, :] = alpha * l_ref[rows, :] + psum
        acc_ref[rows, :] = alpha * acc_ref[rows, :] + _dot(p.astype(BF16), v)
        m_ref[rows, :] = m_new

    def run(j0, items):
        kv = {}
        for d in sorted({d for d, _, _ in items}):
            start = pl.multiple_of((j0 + d) * tk, tk)
            kv[d] = (k_ref[0, pl.ds(start, tk), :], v_ref[0, pl.ds(start, tk), :])
        s = {n: scores(items[n][1], kv[items[n][0]][0]) for n in range(min(ATTN_LOOKAHEAD, len(items)))}
        for n, (d, r, mask) in enumerate(items):
            ahead = n + ATTN_LOOKAHEAD
            if ahead < len(items):
                s[ahead] = scores(items[ahead][1], kv[items[ahead][0]][0])
            softmax_pv(r, s.pop(n), kv[d][1], mask)

    full = [(d, r, None) for d in range(ratio) for r in range(nsub)]

    def body(jj, carry):
        run(jj * ratio, full)
        return carry

    lax.fori_loop(0, i, body, 0)
    diag = []
    for d in range(ratio):
        for r in range(nsub):
            q_lo, q_hi = (r * sub) // CHUNK, (r * sub + sub - 1) // CHUNK
            k_lo, k_hi = (d * tk) // CHUNK, (d * tk + tk - 1) // CHUNK
            if k_lo > q_hi:
                continue
            diag.append((d, r, None if k_hi <= q_lo else _chunk_mask(r * sub, d * tk, sub, tk)))
    run(i * ratio, diag)
    l = jnp.sum(l_ref[...], axis=-1, keepdims=True)
    o_ref[0] = (acc_ref[...] / l).astype(BF16)


def _attn_prompt(qcat, kcat, v, tq, tk):
    B, T, _ = v.shape
    return pl.pallas_call(
        functools.partial(_attn_prompt_kernel, tq=tq, tk=tk, sub=ATTN_SUB),
        grid=(B, MLA_HEADS, T // tq),
        in_specs=[pl.BlockSpec((1, tq, QK_CAT), lambda b, h, i: (b, i, h)),
                  pl.BlockSpec((1, T, QK_CAT), lambda b, h, i: (b, 0, h)),
                  pl.BlockSpec((1, T, V_HEAD), lambda b, h, i: (b, 0, h))],
        out_specs=pl.BlockSpec((1, tq, V_HEAD), lambda b, h, i: (b, i, h)),
        out_shape=jax.ShapeDtypeStruct((B, T, MLA_V), BF16),
        scratch_shapes=[pltpu.VMEM((tq, LANES), F32), pltpu.VMEM((tq, LANES), F32),
                        pltpu.VMEM((tq, V_HEAD), F32)],
        compiler_params=_cparams(("arbitrary", "arbitrary", "arbitrary")),
        name="attn_prompt",
    )(qcat, kcat, v)


def _attn_sample_kernel(q_ref, kn_ref, vn_ref, ckv_ref, kr_ref, kng_ref, wuk_ref, wuv_ref, o_ref,
                        m_ref, l_ref, acc_ref, *, tk):
    T = q_ref.shape[1]
    P = ckv_ref.shape[1]
    R = MLA_HEADS * T
    q = q_ref[0]
    row_head = lax.broadcasted_iota(jnp.int32, (R, 1), 0) // T
    q_chunk = (lax.broadcasted_iota(jnp.int32, (R, 1), 0) % T + P) // CHUNK

    def stacked(x, width):
        lane_head = lax.broadcasted_iota(jnp.int32, (R, x.shape[1]), 1) // width
        return jnp.where(lane_head == row_head, jnp.concatenate([x] * MLA_HEADS, axis=0), jnp.zeros((), x.dtype))

    q_nope = stacked(jnp.concatenate([q[:, h * QK_CAT:h * QK_CAT + QK_NOPE] for h in range(MLA_HEADS)], axis=1),
                     QK_NOPE)
    q_rope = jnp.concatenate([q[:, h * QK_CAT + QK_NOPE:h * QK_CAT + QK_HEAD] for h in range(MLA_HEADS)],
                             axis=0)

    def update(s, k_chunk):
        s = jnp.where(k_chunk <= q_chunk, s, -1e30)
        m_old = m_ref[...]
        m_new = jnp.maximum(m_old, jnp.max(s, axis=-1, keepdims=True))
        alpha = jnp.exp2(m_old - m_new)
        p = jnp.exp2(s - m_new)
        l_ref[...] = alpha * l_ref[...] + jnp.sum(p, axis=-1, keepdims=True)
        m_ref[...] = m_new
        return alpha, p.astype(BF16)

    m_ref[...] = jnp.full_like(m_ref, -1e30)
    l_ref[...] = jnp.zeros_like(l_ref)
    acc_ref[...] = jnp.zeros_like(acc_ref)

    def body(c, carry):
        start = pl.multiple_of(c * tk, tk)
        ckv16 = ckv_ref[0, pl.ds(start, tk), :].astype(BF16)
        kn = _head_rms(_dot(ckv16, wuk_ref[...]), kng_ref[...], QK_NOPE)
        kn16 = jnp.concatenate([x.astype(BF16) for x in kn], axis=1)
        s = _dot_nt(q_nope, kn16) + _dot_nt(q_rope, kr_ref[0, pl.ds(start, tk), :].astype(BF16))
        k_chunk = (lax.broadcasted_iota(jnp.int32, (1, tk), 1) + start) // CHUNK
        alpha, p16 = update(s, k_chunk)
        acc_ref[...] = alpha * acc_ref[...] + _dot(p16, ckv16)
        return carry

    lax.fori_loop(0, P // tk, body, 0)

    s_new = _dot_nt(stacked(q, QK_CAT), kn_ref[0])
    alpha, p16 = update(s_new, (lax.broadcasted_iota(jnp.int32, (1, T), 1) + P) // CHUNK)
    pc16 = (alpha * acc_ref[...]).astype(BF16)
    inv_l = 1.0 / l_ref[...]
    for h in range(MLA_HEADS):
        rows = slice(h * T, (h + 1) * T)
        lanes = slice(h * V_HEAD, (h + 1) * V_HEAD)
        o = _dot(pc16[rows], wuv_ref[:, lanes]) + _dot(p16[rows], vn_ref[0, :, lanes])
        o_ref[0, :, lanes] = (o * inv_l[rows]).astype(BF16)


def _attn_sample(qcat, kcat_new, v_new, past_ckv, past_kr, kn_gain, w_uk16, w_uv16, tk):
    B, T, _ = v_new.shape
    P = past_ckv.shape[1]
    R = MLA_HEADS * T
    perb = lambda b: (b, 0, 0)
    return pl.pallas_call(
        functools.partial(_attn_sample_kernel, tk=tk),
        grid=(B,),
        in_specs=[pl.BlockSpec((1, T, MLA_HEADS * QK_CAT), perb),
                  pl.BlockSpec((1, T, MLA_HEADS * QK_CAT), perb),
                  pl.BlockSpec((1, T, MLA_V), perb),
                  pl.BlockSpec((1, P, KV_RANK), perb),
                  pl.BlockSpec((1, P, QK_ROPE), perb),
                  _resident((1, QK_NOPE)),
                  _resident((KV_RANK, MLA_HEADS * QK_NOPE)),
                  _resident((KV_RANK, MLA_V))],
        out_specs=pl.BlockSpec((1, T, MLA_V), perb),
        out_shape=jax.ShapeDtypeStruct((B, T, MLA_V), BF16),
        scratch_shapes=[pltpu.VMEM((R, 1), F32), pltpu.VMEM((R, 1), F32), pltpu.VMEM((R, KV_RANK), F32)],
        compiler_params=_cparams(("arbitrary",)),
        name="attn_sample",
    )(qcat, kcat_new, v_new, past_ckv, past_kr, kn_gain, w_uk16, w_uv16)


def _out_kernel(x_ref, mod_ref, ua_ref, ob_ref, zb_ref, ga_ref, gb_ref, wdn_ref, wmla_ref, wout_ref, y_ref):
    bb, tm, d = x_ref.shape
    rows = bb * tm
    zb = zb_ref[...].astype(F32)
    ub = (ob_ref[...].astype(F32) * (zb * jax.nn.sigmoid(zb))).astype(BF16).reshape(rows, d)
    ya = _dot(ua_ref[...].reshape(rows, d), wdn_ref[...])
    yb = _dot(ub, wmla_ref[...])
    ga = jax.nn.sigmoid(ga_ref[...].astype(F32)).reshape(rows, d)
    gb = jax.nn.sigmoid(gb_ref[...].astype(F32)).reshape(rows, d)
    merged = (ga * ya + gb * yb).astype(BF16)
    out = _dot(merged, wout_ref[...]).reshape(bb, tm, d)
    gate = mod_ref[:, :, 2 * d:3 * d]
    y_ref[...] = x_ref[...] + gate * out


def _out_proj(x, mod3, u_a, o_b, z_b, g_a, g_b, w_dn16, w_mla16, w_out16, bb, tm):
    B, T, _ = x.shape
    row = lambda b, t: (b, t, 0)
    act = pl.BlockSpec((bb, tm, D_MODEL), row)
    return pl.pallas_call(
        _out_kernel,
        grid=(B // bb, T // tm),
        in_specs=[act, pl.BlockSpec((bb, 1, 3 * D_MODEL), lambda b, t: (b, 0, 0)), act, act, act, act, act,
                  _resident((D_MODEL, D_MODEL)), _resident((D_MODEL, D_MODEL)), _resident((D_MODEL, D_MODEL))],
        out_specs=act,
        out_shape=jax.ShapeDtypeStruct((B, T, D_MODEL), F32),
        compiler_params=_cparams(("arbitrary", "arbitrary")),
        name="out_proj",
    )(x, mod3, u_a, o_b, z_b, g_a, g_b, w_dn16, w_mla16, w_out16)


def _lane_vec(v, off):
    return jnp.zeros((1, LANES), F32).at[0, off:off + v.shape[0]].set(v)


def _tiles(B, T, cached):
    if cached:
        whole = (B, T)
        return dict(proj=whole, prep=whole, out=whole, dn=dict(tm=T, bg=2, G=1), attn_tk=512)
    return dict(proj=(1, 256), prep=(1, 256), out=(1, 512), dn=dict(tm=256, bg=B, G=4), attn=(1024, 512))


def _layer(x, mod, conv_state, s0, past, prm, q_off):
    B, T, _ = x.shape
    tiles = _tiles(B, T, past is not None)
    mod3 = mod.reshape(B, 1, 3 * D_MODEL)
    qkv, z_a, qn_raw, qr_raw, z_b, g_a, g_b, ckv_raw, small = _in_proj(
        x, mod3, prm["norm_gain"], prm["w_pack"], *tiles["proj"])
    u_a, s_new, conv_new = _deltanet(qkv, small, z_a, conv_state, s0, prm["w_conv"], prm["alog_v"],
                                     prm["dtb_v"], prm["dn_out_norm"], **tiles["dn"])
    qcat, kcat, v, ckv_new, kr_new = _mla_prep(
        qn_raw, qr_raw, ckv_raw, small, prm["q_nope_norm"], prm["qr_gain"], prm["kv_norm"], prm["kr_gain"],
        prm["k_nope_norm"], prm["w_uk"], prm["w_uv"], *tiles["prep"], q_off)
    if past is None:
        o_b = _attn_prompt(qcat, kcat, v, *tiles["attn"])
    else:
        past_ckv, past_kr = past
        o_b = _attn_sample(qcat, kcat, v, past_ckv, past_kr, prm["k_nope_norm"], prm["w_uk"], prm["w_uv"],
                           tiles["attn_tk"])
    y = _out_proj(x, mod3, u_a, o_b, z_b, g_a, g_b, prm["w_o_dn"], prm["w_o_mla"], prm["w_out"], *tiles["out"])
    return y, conv_new, s_new, ckv_new, kr_new


def kernel(x_prompt, x_sample, c_prompt, c_sample, cache_ckv, cache_krope, state_delta, state_conv, norm_gain, w_ada, b_ada, w_in, w_conv, a_log, dt_bias, dn_out_norm, q_nope_norm, q_rope_norm, k_nope_norm, k_rope_norm, kv_norm, w_uk, w_uv, w_o_dn, w_o_mla, w_out):
    depth = w_in.shape[0]
    assert depth == 1, "single-layer configuration"
    l = 0
    B, T, _ = x_prompt.shape
    Bs, Ts, _ = x_sample.shape
    past_len = cache_ckv.shape[2]

    row = lambda v: v.reshape(1, -1).astype(F32)
    prm = dict(
        norm_gain=row(norm_gain[l]),
        w_pack=_pack_w_in(w_in[l]),
        w_conv=w_conv[l],
        alog_v=_lane_vec(a_log[l], ALPHA_OFF),
        dtb_v=_lane_vec(dt_bias[l], ALPHA_OFF),
        dn_out_norm=row(dn_out_norm[l]),
        q_nope_norm=row(q_nope_norm[l]),
        qr_gain=jnp.tile(row(q_rope_norm[l]), (1, LANES // QK_ROPE)),
        kv_norm=row(kv_norm[l]),
        kr_gain=_lane_vec(k_rope_norm[l], KR_OFF),
        k_nope_norm=row(k_nope_norm[l]),
        w_uk=w_uk[l].astype(BF16),
        w_uv=w_uv[l].astype(BF16),
        w_o_dn=w_o_dn[l].astype(BF16),
        w_o_mla=w_o_mla[l].astype(BF16),
        w_out=w_out[l].astype(BF16),
    )

    rows = B + Bs
    rows_pad = -(-rows // 8) * 8
    c_all = jnp.concatenate([c_prompt, c_sample, jnp.zeros((rows_pad - rows, D_MODEL), F32)], axis=0)
    mod = _ada(c_all, w_ada[l], b_ada[l].reshape(1, -1))

    zeros_conv = jnp.zeros((B, CONV_W - 1, DN_CONV_CH), F32)
    zeros_state = jnp.zeros((B, DN_HEADS, DN_DK, DN_DV), F32)
    yp, cvp, sdp, kvp, krp = _layer(x_prompt, mod[:B], zeros_conv, zeros_state, None, prm, q_off=0)
    ys, cvs, sds, kvs, krs = _layer(x_sample, mod[B:rows], state_conv[l], state_delta[l],
                                    (cache_ckv[l], cache_krope[l]), prm, q_off=past_len)
    st = lambda a: a[None]
    return (yp, ys, st(kvp), st(krp), st(sdp), st(cvp), st(kvs), st(krs), st(sds), st(cvs))
```

```python
import functools
import math

import jax
import jax.numpy as jnp
from jax import lax
from jax.experimental import pallas as pl
from jax.experimental.pallas import tpu as pltpu

D_MODEL = 1024
CHUNK = 64
EPS = 1e-6
DN_HEADS = 8
DN_DK = 128
DN_DV = 128
DN_QK = DN_HEADS * DN_DK
DN_V = DN_HEADS * DN_DV
DN_CONV_CH = 2 * DN_QK + DN_V
CONV_W = 4
MLA_HEADS = 8
QK_NOPE = 128
QK_ROPE = 64
QK_HEAD = QK_NOPE + QK_ROPE
V_HEAD = 128
KV_RANK = 512
MLA_Q = MLA_HEADS * QK_HEAD
MLA_V = MLA_HEADS * V_HEAD
ROPE_THETA = 10000.0

LANES = 128
QK_CAT = 256
ATTN_SUB = 256
ATTN_LOOKAHEAD = 2
ATTN_TILES_PER_TRIP = 4
KR_OFF = 0
BETA_OFF = QK_ROPE
ALPHA_OFF = QK_ROPE + DN_HEADS
VMEM_LIMIT = 56 * 1024 * 1024

F32 = jnp.float32
BF16 = jnp.bfloat16
HI = lax.Precision.HIGHEST


def _dot(a, b):
    return jnp.dot(a, b, preferred_element_type=F32)


def _dot_nt(a, b, precision=None):
    return lax.dot_general(a, b, (((1,), (1,)), ((), ())), preferred_element_type=F32, precision=precision)


def _dot_tn(a, b):
    return lax.dot_general(a, b, (((0,), (0,)), ((), ())), preferred_element_type=F32)


def _cparams(sem):
    return pltpu.CompilerParams(dimension_semantics=sem, vmem_limit_bytes=VMEM_LIMIT)


def _resident(shape):
    nd = len(shape)
    return pl.BlockSpec(shape, lambda *_: (0,) * nd, pipeline_mode=pl.Buffered(1))


def _ada_kernel(c_ref, w_ref, b_ref, o_ref):
    o_ref[...] = jnp.dot(c_ref[...], w_ref[...], preferred_element_type=F32, precision=HI) + b_ref[...]


def _ada(c_all, w_ada, b_ada):
    rows = c_all.shape[0]
    tn = 512
    return pl.pallas_call(
        _ada_kernel,
        grid=(3 * D_MODEL // tn,),
        in_specs=[pl.BlockSpec((rows, D_MODEL), lambda j: (0, 0)),
                  pl.BlockSpec((D_MODEL, tn), lambda j: (0, j)),
                  pl.BlockSpec((1, tn), lambda j: (0, j))],
        out_specs=pl.BlockSpec((rows, tn), lambda j: (0, j)),
        out_shape=jax.ShapeDtypeStruct((rows, 3 * D_MODEL), F32),
        compiler_params=_cparams(("arbitrary",)),
        name="ada",
    )(c_all, w_ada, b_ada)


_PROJ_GROUPS = (("qkv", DN_CONV_CH, BF16), ("z_a", DN_V, BF16), ("qn", MLA_HEADS * QK_NOPE, BF16),
                ("qr", MLA_HEADS * QK_ROPE, BF16), ("z_b", MLA_V, BF16), ("g_a", D_MODEL, BF16),
                ("g_b", D_MODEL, BF16), ("ckv", KV_RANK, F32), ("small", LANES, F32))


def _pack_w_in(w_in):
    o = 0
    qkv = w_in[:, o:o + DN_CONV_CH]; o += DN_CONV_CH
    z_a = w_in[:, o:o + DN_V]; o += DN_V
    beta = w_in[:, o:o + DN_HEADS]; o += DN_HEADS
    alpha = w_in[:, o:o + DN_HEADS]; o += DN_HEADS
    q = w_in[:, o:o + MLA_Q].reshape(D_MODEL, MLA_HEADS, QK_HEAD); o += MLA_Q
    ckv = w_in[:, o:o + KV_RANK]; o += KV_RANK
    kr = w_in[:, o:o + QK_ROPE]; o += QK_ROPE
    z_b = w_in[:, o:o + MLA_V]; o += MLA_V
    g_a = w_in[:, o:o + D_MODEL]; o += D_MODEL
    g_b = w_in[:, o:o + D_MODEL]
    qn = q[:, :, :QK_NOPE].reshape(D_MODEL, MLA_HEADS * QK_NOPE)
    qr = q[:, :, QK_NOPE:].reshape(D_MODEL, MLA_HEADS * QK_ROPE)
    pad = jnp.zeros((D_MODEL, LANES - QK_ROPE - 2 * DN_HEADS), w_in.dtype)
    small = jnp.concatenate([kr, beta, alpha, pad], axis=1)
    return tuple(w.astype(BF16) for w in (qkv, z_a, qn, qr, z_b, g_a, g_b, ckv, small))


def _in_proj_kernel(x_ref, mod_ref, gain_ref, *refs):
    bb, tm, d = x_ref.shape
    x = x_ref[...]
    ms = jnp.mean(x * x, axis=-1, keepdims=True)
    y = x * lax.rsqrt(ms + EPS) * gain_ref[...]
    shift = mod_ref[:, :, 0:d]
    scale = mod_ref[:, :, d:2 * d]
    h = (y * (1.0 + scale) + shift).astype(BF16).reshape(bb * tm, d)
    n = len(_PROJ_GROUPS)
    for (_, width, dtype), w_ref, o_ref in zip(_PROJ_GROUPS, refs[:n], refs[n:]):
        o_ref[...] = _dot(h, w_ref[...]).astype(dtype).reshape(bb, tm, width)


def _in_proj(x, mod3, gain, w_pack, bb, tm):
    B, T, _ = x.shape
    row = lambda b, t: (b, t, 0)
    return pl.pallas_call(
        _in_proj_kernel,
        grid=(B // bb, T // tm),
        in_specs=[pl.BlockSpec((bb, tm, D_MODEL), row),
                  pl.BlockSpec((bb, 1, 3 * D_MODEL), lambda b, t: (b, 0, 0)),
                  _resident((1, D_MODEL))]
        + [_resident((D_MODEL, w)) for _, w, _ in _PROJ_GROUPS],
        out_specs=[pl.BlockSpec((bb, tm, w), row) for _, w, _ in _PROJ_GROUPS],
        out_shape=[jax.ShapeDtypeStruct((B, T, w), dt) for _, w, dt in _PROJ_GROUPS],
        compiler_params=_cparams(("arbitrary", "arbitrary")),
        name="in_proj",
    )(x, mod3, gain, *w_pack)


def _softplus(x):
    return jnp.maximum(x, 0.0) + jnp.log(1.0 + jnp.exp(-jnp.abs(x)))


def _dn_prep_kernel(qkv_ref, prev_ref, cs_ref, small_ref, wconv_ref, alog_ref, dtb_ref,
                    wq_ref, u_ref, kd_ref, attn_ref, egl_ref, *, C):
    t = pl.program_id(1)
    tm = qkv_ref.shape[1]
    nc = tm // C
    pad = prev_ref.shape[1]

    x16 = qkv_ref[0]
    hist = jnp.where(t == 0, cs_ref[0], prev_ref[0].astype(F32))
    hist_hi = hist.astype(BF16)
    rem = hist - hist_hi.astype(F32)
    hist_mid = rem.astype(BF16)
    hist_lo = (rem - hist_mid.astype(F32)).astype(BF16)
    full16 = jnp.concatenate([hist_hi, hist_mid, hist_lo, x16], axis=0)
    n_sh = CONV_W - 1
    srow = lax.broadcasted_iota(jnp.int32, (n_sh * tm, 3 * pad + tm), 0)
    scol = lax.broadcasted_iota(jnp.int32, (n_sh * tm, 3 * pad + tm), 1)
    src = srow % tm + srow // tm + pad - n_sh
    shift_sel = ((scol == src + 2 * pad) | ((scol == src) & (scol < pad))
                 | ((scol == src + pad) & (scol < 2 * pad))).astype(BF16)
    shifted = _dot(shift_sel, full16)
    conv = x16.astype(F32) * wconv_ref[n_sh:CONV_W, :]
    for i in range(n_sh):
        conv = conv + shifted[i * tm:(i + 1) * tm, :] * wconv_ref[i:i + 1, :]
    act = conv * jax.nn.sigmoid(conv)

    sm = small_ref[0]
    beta_all = jax.nn.sigmoid(sm)
    g_all = -jnp.exp(alog_ref[...]) * _softplus(sm + dtb_ref[...])
    rt = lax.broadcasted_iota(jnp.int32, (tm, tm), 0)
    ct = lax.broadcasted_iota(jnp.int32, (tm, tm), 1)
    chunk_tri = ((rt // C == ct // C) & (rt >= ct)).astype(F32)
    gcum = jnp.dot(chunk_tri, g_all, preferred_element_type=F32, precision=HI)
    sel = (lax.broadcasted_iota(jnp.int32, (DN_HEADS, LANES), 1)
           == lax.broadcasted_iota(jnp.int32, (DN_HEADS, LANES), 0) + ALPHA_OFF).astype(F32)
    gcum_t = _dot_nt(sel, gcum, precision=HI)

    def per_head_lanes(x, off):
        hi = x.astype(BF16)
        r1 = x - hi.astype(F32)
        mid = r1.astype(BF16)
        lo = (r1 - mid.astype(F32)).astype(BF16)
        erow = lax.broadcasted_iota(jnp.int32, (3 * LANES, DN_QK), 0) % LANES
        ecol = lax.broadcasted_iota(jnp.int32, (3 * LANES, DN_QK), 1) // DN_DK
        return _dot(jnp.concatenate([hi, mid, lo], axis=1), (erow == ecol + off).astype(BF16))

    g_b = per_head_lanes(gcum, ALPHA_OFF)
    beta_b = per_head_lanes(beta_all, BETA_OFF)
    glast_b = jnp.concatenate(
        [jnp.broadcast_to(g_b[c * C + C - 1:(c + 1) * C, :], (C, DN_QK)) for c in range(nc)], axis=0)
    eg_b = jnp.exp(g_b)
    kdf_b = jnp.exp(glast_b - g_b)
    for c in range(nc):
        egl_ref[0, c] = jnp.exp(g_b[c * C + C - 1:(c + 1) * C, :])

    ri = lax.broadcasted_iota(jnp.int32, (C, C), 0)
    ci = lax.broadcasted_iota(jnp.int32, (C, C), 1)
    tri_incl = ri >= ci
    tri_strict = ri > ci
    eye = (ri == ci).astype(F32)
    pair_masks = []
    m = 1
    while m < C:
        pair_masks.append((ri // (2 * m) == ci // (2 * m)) & (ri // m != ci // m))
        m *= 2

    heads = range(DN_HEADS)
    qn, kn = [], []
    for h in heads:
        qh = act[:, h * DN_DK:(h + 1) * DN_DK]
        kh = act[:, DN_QK + h * DN_DK:DN_QK + (h + 1) * DN_DK]
        qn.append(qh * lax.rsqrt(jnp.sum(qh * qh, axis=-1, keepdims=True) + EPS) * (DN_DK ** -0.5))
        kn.append(kh * lax.rsqrt(jnp.sum(kh * kh, axis=-1, keepdims=True) + EPS))

    hl = lambda h: slice(h * DN_DK, (h + 1) * DN_DK)
    k16 = [kn[h].astype(BF16) for h in heads]
    q16 = [qn[h].astype(BF16) for h in heads]
    kb = [kn[h] * beta_b[:, hl(h)] for h in heads]
    kb16 = [kb[h].astype(BF16) for h in heads]
    kbe16 = [(kb[h] * eg_b[:, hl(h)]).astype(BF16) for h in heads]
    vb16 = [(act[:, 2 * DN_QK + h * DN_DV:2 * DN_QK + (h + 1) * DN_DV] * beta_b[:, hl(h)]).astype(BF16)
            for h in heads]
    qe16 = [(qn[h] * eg_b[:, hl(h)]).astype(BF16) for h in heads]
    for h in heads:
        kd_ref[0, :, hl(h)] = (kn[h] * kdf_b[:, hl(h)]).astype(BF16)

    attn_ref[...] = jnp.zeros_like(attn_ref)
    items = [(c, h) for c in range(nc) for h in heads]
    rows = lambda c: slice(c * C, (c + 1) * C)
    decay, rhs, qk = [], [], []
    for c, h in items:
        r = rows(c)
        gc = g_b[r, h * DN_DK:h * DN_DK + C]
        decay.append(jnp.exp(jnp.where(tri_incl, gc - gcum_t[h:h + 1, r], -1e30)))
        rhs.append(jnp.concatenate([kbe16[h][r], vb16[h][r]], axis=1))
        wq_ref[0, c, C:2 * C, hl(h)] = qe16[h][r]
        qk.append(_dot_nt(jnp.concatenate([kb16[h][r], q16[h][r]], axis=0), k16[h][r]))

    lmat = []
    for i, (c, h) in enumerate(items):
        lmat.append(jnp.where(tri_strict, qk[i][:C] * decay[i], 0.0))
        attn_ref[0, rows(c), h * DN_DK:h * DN_DK + C] = (qk[i][C:] * decay[i]).astype(BF16)

    pinv = [eye - jnp.where(pair_masks[0], l, 0.0) for l in lmat]
    for mask in pair_masks[1:]:
        p16 = [p.astype(BF16) for p in pinv]
        tmp = [_dot(p16[i], jnp.where(mask, lmat[i], 0.0).astype(BF16)).astype(BF16) for i in range(len(items))]
        pinv = [pinv[i] - _dot(tmp[i], p16[i]) for i in range(len(items))]

    for i, (c, h) in enumerate(items):
        wu = _dot(pinv[i].astype(BF16), rhs[i])
        wq_ref[0, c, 0:C, hl(h)] = wu[:, :DN_DK].astype(BF16)
        u_ref[0, rows(c), hl(h)] = wu[:, DN_DK:]


def _dn_scan_kernel(wq_ref, u_ref, kd_ref, attn_ref, egl_ref, za_ref, s0_ref, onorm_ref,
                    ua_ref, sfin_ref, s_ref, *, C):
    n = pl.program_id(1)
    bg, G = wq_ref.shape[0], wq_ref.shape[1]

    @pl.when(n == 0)
    def _():
        s_ref[...] = s0_ref[...]

    chains = [(b, h) for b in range(bg) for h in range(DN_HEADS)]
    for g in range(G):
        r = slice(g * C, (g + 1) * C)
        s_old = [s_ref[b, h] for b, h in chains]
        s16 = [s.astype(BF16) for s in s_old]
        ws = [_dot(wq_ref[b, g, :, h * DN_DK:(h + 1) * DN_DK], s16[i]) for i, (b, h) in enumerate(chains)]
        v16 = [(u_ref[b, r, h * DN_DV:(h + 1) * DN_DV] - ws[i][:C]).astype(BF16)
               for i, (b, h) in enumerate(chains)]
        for i, (b, h) in enumerate(chains):
            lo = h * DN_DK
            s_ref[b, h] = s_old[i] * egl_ref[b, g, :, lo:lo + DN_DK] + _dot_tn(kd_ref[b, r, lo:lo + DN_DK], v16[i])
        for i, (b, h) in enumerate(chains):
            lo = h * DN_DV
            o = ws[i][C:] + _dot(attn_ref[b, r, lo:lo + C], v16[i])
            o = o * lax.rsqrt(jnp.mean(o * o, axis=-1, keepdims=True) + EPS) * onorm_ref[...]
            z = za_ref[b, r, lo:lo + DN_DV].astype(F32)
            ua_ref[b, r, lo:lo + DN_DV] = (o * (z * jax.nn.sigmoid(z))).astype(BF16)

    @pl.when(n == pl.num_programs(1) - 1)
    def _():
        sfin_ref[...] = s_ref[...]


def _deltanet(qkv, small, z_a, conv_state, s0, w_conv, alog_v, dtb_v, onorm, tm, bg, G):
    B, T, _ = qkv.shape
    C = min(CHUNK, T)
    N = T // C
    nc = tm // C
    hist_rows = 16
    cs = jnp.pad(conv_state, ((0, 0), (hist_rows - (CONV_W - 1), 0), (0, 0)))
    tile = lambda b, t: (b, t, 0)
    prev = lambda b, t: (b, jnp.maximum(t * (tm // hist_rows) - 1, 0), 0)
    wq, u, kd, attn, egl = pl.pallas_call(
        functools.partial(_dn_prep_kernel, C=C),
        grid=(B, T // tm),
        in_specs=[pl.BlockSpec((1, tm, DN_CONV_CH), tile),
                  pl.BlockSpec((1, hist_rows, DN_CONV_CH), prev),
                  pl.BlockSpec((1, hist_rows, DN_CONV_CH), lambda b, t: (b, 0, 0)),
                  pl.BlockSpec((1, tm, LANES), tile),
                  _resident((CONV_W, DN_CONV_CH)),
                  _resident((1, LANES)),
                  _resident((1, LANES))],
        out_specs=[pl.BlockSpec((1, nc, 2 * C, DN_QK), lambda b, t: (b, t, 0, 0)),
                   pl.BlockSpec((1, tm, DN_V), tile),
                   pl.BlockSpec((1, tm, DN_QK), tile),
                   pl.BlockSpec((1, tm, DN_V), tile),
                   pl.BlockSpec((1, nc, 1, DN_QK), lambda b, t: (b, t, 0, 0))],
        out_shape=[jax.ShapeDtypeStruct((B, N, 2 * C, DN_QK), BF16),
                   jax.ShapeDtypeStruct((B, T, DN_V), F32),
                   jax.ShapeDtypeStruct((B, T, DN_QK), BF16),
                   jax.ShapeDtypeStruct((B, T, DN_V), BF16),
                   jax.ShapeDtypeStruct((B, N, 1, DN_QK), F32)],
        compiler_params=_cparams(("arbitrary", "arbitrary")),
        name="dn_prep",
    )(qkv, qkv, cs, small, w_conv, alog_v, dtb_v)

    grp = lambda b, n: (b, n, 0)
    grp4 = lambda b, n: (b, n, 0, 0)
    state = pl.BlockSpec((bg, DN_HEADS, DN_DK, DN_DV), lambda b, n: (b, 0, 0, 0))
    u_a, s_new = pl.pallas_call(
        functools.partial(_dn_scan_kernel, C=C),
        grid=(B // bg, N // G),
        in_specs=[pl.BlockSpec((bg, G, 2 * C, DN_QK), grp4),
                  pl.BlockSpec((bg, G * C, DN_V), grp),
                  pl.BlockSpec((bg, G * C, DN_QK), grp),
                  pl.BlockSpec((bg, G * C, DN_V), grp),
                  pl.BlockSpec((bg, G, 1, DN_QK), grp4),
                  pl.BlockSpec((bg, G * C, DN_V), grp),
                  state,
                  _resident((1, DN_DV))],
        out_specs=[pl.BlockSpec((bg, G * C, DN_V), grp), state],
        out_shape=[jax.ShapeDtypeStruct((B, T, DN_V), BF16),
                   jax.ShapeDtypeStruct((B, DN_HEADS, DN_DK, DN_DV), F32)],
        scratch_shapes=[pltpu.VMEM((bg, DN_HEADS, DN_DK, DN_DV), F32)],
        compiler_params=_cparams(("arbitrary", "arbitrary")),
        name="dn_scan",
    )(wq, u, kd, attn, egl, z_a, s0, onorm)
    conv_new = qkv[:, T - (CONV_W - 1):, :].astype(F32)
    return u_a, s_new, conv_new


def _head_rms(x, gain_row, width):
    outs = []
    for h in range(x.shape[1] // width):
        xh = x[:, h * width:(h + 1) * width]
        outs.append(xh * lax.rsqrt(jnp.mean(xh * xh, axis=-1, keepdims=True) + EPS) * gain_row)
    return outs


def _kv_up(ckv16, kr_pad16, wuk_ref, wuv_ref, knorm_ref, kcat_ref, v_ref, shape3):
    bb, tm = shape3
    kn = _head_rms(_dot(ckv16, wuk_ref[...]), knorm_ref[...], QK_NOPE)
    for h in range(MLA_HEADS):
        kcat_ref[:, :, h * QK_CAT:h * QK_CAT + QK_NOPE] = kn[h].astype(BF16).reshape(bb, tm, QK_NOPE)
        kcat_ref[:, :, h * QK_CAT + QK_NOPE:(h + 1) * QK_CAT] = kr_pad16.reshape(bb, tm, LANES)
    v_ref[...] = _dot(ckv16, wuv_ref[...]).astype(BF16).reshape(bb, tm, MLA_V)


def _rope_inv_freq(shape):
    half = QK_ROPE // 2
    fidx = (lax.broadcasted_iota(jnp.int32, shape, 1) % half).astype(F32)
    return jnp.exp(fidx * (-math.log(ROPE_THETA) / half))


def _mla_prep_kernel(qn_ref, qr_ref, ckv_ref, small_ref, qng_ref, qrg_ref, kvg_ref, krg_ref, kng_ref,
                     wuk_ref, wuv_ref, qcat_ref, kcat_ref, v_ref, ckvn_ref, krn_ref, rope_ref, *, q_off):
    bb, tm, _ = qn_ref.shape
    rows = bb * tm
    t0 = pl.program_id(1) * tm
    scale = QK_HEAD ** -0.5 * math.log2(math.e)

    @pl.when((pl.program_id(0) == 0) & (pl.program_id(1) == 0))
    def _():
        off = (lax.broadcasted_iota(jnp.int32, (rows, LANES), 0) % tm).astype(F32) * _rope_inv_freq((rows, LANES))
        rope_ref[0] = jnp.cos(off)
        rope_ref[1] = jnp.sin(off)

    base = (t0 + q_off).astype(F32) * _rope_inv_freq((1, LANES))
    cos_a, sin_a = jnp.cos(base), jnp.sin(base)
    cos = cos_a * rope_ref[0] - sin_a * rope_ref[1]
    sin = sin_a * rope_ref[0] + cos_a * rope_ref[1]
    lane = lax.broadcasted_iota(jnp.int32, (rows, LANES), 1)
    low_half = lane < QK_ROPE
    first = (lane % QK_ROPE) < (QK_ROPE // 2)

    def rope(y):
        rot = jnp.where(first, -pltpu.roll(y, LANES - QK_ROPE // 2, 1), pltpu.roll(y, QK_ROPE // 2, 1))
        return y * cos + rot * sin

    def rms64(x):
        xx = x * x
        s_lo = jnp.sum(jnp.where(low_half, xx, 0.0), axis=-1, keepdims=True)
        s_hi = jnp.sum(jnp.where(low_half, 0.0, xx), axis=-1, keepdims=True)
        return lax.rsqrt(jnp.where(low_half, s_lo, s_hi) * (1.0 / QK_ROPE) + EPS)

    qn = _head_rms(qn_ref[...].astype(F32).reshape(rows, MLA_HEADS * QK_NOPE), qng_ref[...] * scale, QK_NOPE)
    qr_all = qr_ref[...].astype(F32).reshape(rows, MLA_HEADS * QK_ROPE)
    for c in range(MLA_HEADS // 2):
        x = qr_all[:, c * LANES:(c + 1) * LANES]
        y = rope(x * rms64(x) * qrg_ref[...]) * scale
        even = jnp.where(low_half, y, 0.0)
        odd = jnp.where(low_half, pltpu.roll(y, QK_ROPE, 1), 0.0)
        for h, part in ((2 * c, even), (2 * c + 1, odd)):
            qcat_ref[:, :, h * QK_CAT:h * QK_CAT + QK_NOPE] = qn[h].astype(BF16).reshape(bb, tm, QK_NOPE)
            qcat_ref[:, :, h * QK_CAT + QK_NOPE:(h + 1) * QK_CAT] = part.astype(BF16).reshape(bb, tm, LANES)

    sm = small_ref[...].reshape(rows, LANES)
    kr = rope(sm * rms64(sm) * krg_ref[...])
    krn_ref[...] = kr[:, :QK_ROPE].reshape(bb, tm, QK_ROPE)
    kr_pad16 = jnp.where(low_half, kr, 0.0).astype(BF16)

    ckv = ckv_ref[...].reshape(rows, KV_RANK)
    ckvn = ckv * lax.rsqrt(jnp.mean(ckv * ckv, axis=-1, keepdims=True) + EPS) * kvg_ref[...]
    ckvn_ref[...] = ckvn.reshape(bb, tm, KV_RANK)
    _kv_up(ckvn.astype(BF16), kr_pad16, wuk_ref, wuv_ref, kng_ref, kcat_ref, v_ref, (bb, tm))


def _mla_prep(qn_raw, qr_raw, ckv_raw, small, qn_gain, qr_gain, kv_gain, kr_gain, kn_gain, w_uk16, w_uv16,
              bb, tm, q_off):
    B, T, _ = qn_raw.shape
    row = lambda b, t: (b, t, 0)
    widths_in = (MLA_HEADS * QK_NOPE, MLA_HEADS * QK_ROPE, KV_RANK, LANES)
    outs = ((MLA_HEADS * QK_CAT, BF16), (MLA_HEADS * QK_CAT, BF16), (MLA_V, BF16), (KV_RANK, F32), (QK_ROPE, F32))
    return pl.pallas_call(
        functools.partial(_mla_prep_kernel, q_off=q_off),
        grid=(B // bb, T // tm),
        in_specs=[pl.BlockSpec((bb, tm, w), row) for w in widths_in]
        + [_resident((1, QK_NOPE)), _resident((1, LANES)), _resident((1, KV_RANK)), _resident((1, LANES)),
           _resident((1, QK_NOPE)), _resident((KV_RANK, MLA_HEADS * QK_NOPE)), _resident((KV_RANK, MLA_V))],
        out_specs=[pl.BlockSpec((bb, tm, w), row) for w, _ in outs],
        out_shape=[jax.ShapeDtypeStruct((B, T, w), dt) for w, dt in outs],
        scratch_shapes=[pltpu.VMEM((2, bb * tm, LANES), F32)],
        compiler_params=_cparams(("arbitrary", "arbitrary")),
        name="mla_prep",
    )(qn_raw, qr_raw, ckv_raw, small, qn_gain, qr_gain, kv_gain, kr_gain, kn_gain, w_uk16, w_uv16)


def _chunk_mask(qpos0, kpos0, tq, tk):
    qc = (lax.broadcasted_iota(jnp.int32, (tq, tk), 0) + qpos0) // CHUNK
    kc = (lax.broadcasted_iota(jnp.int32, (tq, tk), 1) + kpos0) // CHUNK
    return kc <= qc


def _attn_prompt_kernel(q_ref, k_ref, v_ref, o_ref, m_ref, l_ref, acc_ref, *, tq, tk, sub):
    i = pl.program_id(2)
    nsub = tq // sub
    ratio = tq // tk
    m_ref[...] = jnp.full_like(m_ref, -1e30)
    l_ref[...] = jnp.zeros_like(l_ref)
    acc_ref[...] = jnp.zeros_like(acc_ref)

    def scores(r, k):
        return _dot_nt(q_ref[0, r * sub:(r + 1) * sub, :], k)

    def softmax_pv(r, sr, v, mask):
        rows = slice(r * sub, (r + 1) * sub)
        if mask is not None:
            sr = jnp.where(mask, sr, -1e30)
        m_old = m_ref[rows, :]
        m_new = jnp.maximum(m_old, jnp.max(sr, axis=-1, keepdims=True))
        alpha = jnp.exp2(m_old - m_new)
        p = jnp.exp2(sr - jnp.tile(m_new, (1, tk // LANES)))
        psum = p[:, 0:LANES]
        for c in range(1, tk // LANES):
            psum = psum + p[:, c * LANES:(c + 1) * LANES]
        l_ref[rows, :] = alpha * l_ref[rows, :] + psum
        acc_ref[rows, :] = alpha * acc_ref[rows, :] + _dot(p.astype(BF16), v)
        m_ref[rows, :] = m_new

    def run(j0, items):
        kv = {}
        for d in sorted({d for d, _, _ in items}):
            start = pl.multiple_of((j0 + d) * tk, tk)
            kv[d] = (k_ref[0, pl.ds(start, tk), :], v_ref[0, pl.ds(start, tk), :])
        s = {n: scores(items[n][1], kv[items[n][0]][0]) for n in range(min(ATTN_LOOKAHEAD, len(items)))}
        for n, (d, r, mask) in enumerate(items):
            ahead = n + ATTN_LOOKAHEAD
            if ahead < len(items):
                s[ahead] = scores(items[ahead][1], kv[items[ahead][0]][0])
            softmax_pv(r, s.pop(n), kv[d][1], mask)

    per_trip = math.gcd(ratio, ATTN_TILES_PER_TRIP)
    full = [(d, r, None) for d in range(per_trip) for r in range(nsub)]

    def body(jj, carry):
        run(jj * per_trip, full)
        return carry

    lax.fori_loop(0, i * (ratio // per_trip), body, 0)
    diag = []
    for d in range(ratio):
        for r in range(nsub):
            q_lo, q_hi = (r * sub) // CHUNK, (r * sub + sub - 1) // CHUNK
            k_lo, k_hi = (d * tk) // CHUNK, (d * tk + tk - 1) // CHUNK
            if k_lo > q_hi:
                continue
            diag.append((d, r, None if k_hi <= q_lo else _chunk_mask(r * sub, d * tk, sub, tk)))
    run(i * ratio, diag)
    l = jnp.sum(l_ref[...], axis=-1, keepdims=True)
    o_ref[0] = (acc_ref[...] / l).astype(BF16)


def _attn_prompt(qcat, kcat, v, tq, tk):
    B, T, _ = v.shape
    return pl.pallas_call(
        functools.partial(_attn_prompt_kernel, tq=tq, tk=tk, sub=ATTN_SUB),
        grid=(B, MLA_HEADS, T // tq),
        in_specs=[pl.BlockSpec((1, tq, QK_CAT), lambda b, h, i: (b, i, h)),
                  pl.BlockSpec((1, T, QK_CAT), lambda b, h, i: (b, 0, h)),
                  pl.BlockSpec((1, T, V_HEAD), lambda b, h, i: (b, 0, h))],
        out_specs=pl.BlockSpec((1, tq, V_HEAD), lambda b, h, i: (b, i, h)),
        out_shape=jax.ShapeDtypeStruct((B, T, MLA_V), BF16),
        scratch_shapes=[pltpu.VMEM((tq, LANES), F32), pltpu.VMEM((tq, LANES), F32),
                        pltpu.VMEM((tq, V_HEAD), F32)],
        compiler_params=_cparams(("arbitrary", "arbitrary", "arbitrary")),
        name="attn_prompt",
    )(qcat, kcat, v)


def _attn_sample_kernel(q_ref, kn_ref, vn_ref, ckv_ref, kr_ref, kng_ref, wuk_ref, wuv_ref, o_ref,
                        m_ref, l_ref, acc_ref, *, tk):
    T = q_ref.shape[1]
    P = ckv_ref.shape[1]
    R = MLA_HEADS * T
    q = q_ref[0]
    row_head = lax.broadcasted_iota(jnp.int32, (R, 1), 0) // T
    q_chunk = (lax.broadcasted_iota(jnp.int32, (R, 1), 0) % T + P) // CHUNK

    def stacked(x, width):
        lane_head = lax.broadcasted_iota(jnp.int32, (R, x.shape[1]), 1) // width
        return jnp.where(lane_head == row_head, jnp.concatenate([x] * MLA_HEADS, axis=0), jnp.zeros((), x.dtype))

    q_nope = stacked(jnp.concatenate([q[:, h * QK_CAT:h * QK_CAT + QK_NOPE] for h in range(MLA_HEADS)], axis=1),
                     QK_NOPE)
    q_rope = jnp.concatenate([q[:, h * QK_CAT + QK_NOPE:h * QK_CAT + QK_HEAD] for h in range(MLA_HEADS)],
                             axis=0)

    def update(s, k_chunk):
        s = jnp.where(k_chunk <= q_chunk, s, -1e30)
        m_old = m_ref[...]
        m_new = jnp.maximum(m_old, jnp.max(s, axis=-1, keepdims=True))
        alpha = jnp.exp2(m_old - m_new)
        p = jnp.exp2(s - m_new)
        l_ref[...] = alpha * l_ref[...] + jnp.sum(p, axis=-1, keepdims=True)
        m_ref[...] = m_new
        return alpha, p.astype(BF16)

    m_ref[...] = jnp.full_like(m_ref, -1e30)
    l_ref[...] = jnp.zeros_like(l_ref)
    acc_ref[...] = jnp.zeros_like(acc_ref)

    def body(c, carry):
        start = pl.multiple_of(c * tk, tk)
        ckv16 = ckv_ref[0, pl.ds(start, tk), :].astype(BF16)
        kn = _head_rms(_dot(ckv16, wuk_ref[...]), kng_ref[...], QK_NOPE)
        kn16 = jnp.concatenate([x.astype(BF16) for x in kn], axis=1)
        s = _dot_nt(q_nope, kn16) + _dot_nt(q_rope, kr_ref[0, pl.ds(start, tk), :].astype(BF16))
        k_chunk = (lax.broadcasted_iota(jnp.int32, (1, tk), 1) + start) // CHUNK
        alpha, p16 = update(s, k_chunk)
        acc_ref[...] = alpha * acc_ref[...] + _dot(p16, ckv16)
        return carry

    lax.fori_loop(0, P // tk, body, 0)

    s_new = _dot_nt(stacked(q, QK_CAT), kn_ref[0])
    alpha, p16 = update(s_new, (lax.broadcasted_iota(jnp.int32, (1, T), 1) + P) // CHUNK)
    pc16 = (alpha * acc_ref[...]).astype(BF16)
    inv_l = 1.0 / l_ref[...]
    for h in range(MLA_HEADS):
        rows = slice(h * T, (h + 1) * T)
        lanes = slice(h * V_HEAD, (h + 1) * V_HEAD)
        o = _dot(pc16[rows], wuv_ref[:, lanes]) + _dot(p16[rows], vn_ref[0, :, lanes])
        o_ref[0, :, lanes] = (o * inv_l[rows]).astype(BF16)


def _attn_sample(qcat, kcat_new, v_new, past_ckv, past_kr, kn_gain, w_uk16, w_uv16, tk):
    B, T, _ = v_new.shape
    P = past_ckv.shape[1]
    R = MLA_HEADS * T
    perb = lambda b: (b, 0, 0)
    return pl.pallas_call(
        functools.partial(_attn_sample_kernel, tk=tk),
        grid=(B,),
        in_specs=[pl.BlockSpec((1, T, MLA_HEADS * QK_CAT), perb),
                  pl.BlockSpec((1, T, MLA_HEADS * QK_CAT), perb),
                  pl.BlockSpec((1, T, MLA_V), perb),
                  pl.BlockSpec((1, P, KV_RANK), perb),
                  pl.BlockSpec((1, P, QK_ROPE), perb),
                  _resident((1, QK_NOPE)),
                  _resident((KV_RANK, MLA_HEADS * QK_NOPE)),
                  _resident((KV_RANK, MLA_V))],
        out_specs=pl.BlockSpec((1, T, MLA_V), perb),
        out_shape=jax.ShapeDtypeStruct((B, T, MLA_V), BF16),
        scratch_shapes=[pltpu.VMEM((R, 1), F32), pltpu.VMEM((R, 1), F32), pltpu.VMEM((R, KV_RANK), F32)],
        compiler_params=_cparams(("arbitrary",)),
        name="attn_sample",
    )(qcat, kcat_new, v_new, past_ckv, past_kr, kn_gain, w_uk16, w_uv16)


def _out_kernel(x_ref, mod_ref, ua_ref, ob_ref, zb_ref, ga_ref, gb_ref, wdn_ref, wmla_ref, wout_ref, y_ref):
    bb, tm, d = x_ref.shape
    rows = bb * tm
    zb = zb_ref[...].astype(F32)
    ub = (ob_ref[...].astype(F32) * (zb * jax.nn.sigmoid(zb))).astype(BF16).reshape(rows, d)
    ya = _dot(ua_ref[...].reshape(rows, d), wdn_ref[...])
    yb = _dot(ub, wmla_ref[...])
    ga = jax.nn.sigmoid(ga_ref[...].astype(F32)).reshape(rows, d)
    gb = jax.nn.sigmoid(gb_ref[...].astype(F32)).reshape(rows, d)
    merged = (ga * ya + gb * yb).astype(BF16)
    out = _dot(merged, wout_ref[...]).reshape(bb, tm, d)
    gate = mod_ref[:, :, 2 * d:3 * d]
    y_ref[...] = x_ref[...] + gate * out


def _out_proj(x, mod3, u_a, o_b, z_b, g_a, g_b, w_dn16, w_mla16, w_out16, bb, tm):
    B, T, _ = x.shape
    row = lambda b, t: (b, t, 0)
    act = pl.BlockSpec((bb, tm, D_MODEL), row)
    return pl.pallas_call(
        _out_kernel,
        grid=(B // bb, T // tm),
        in_specs=[act, pl.BlockSpec((bb, 1, 3 * D_MODEL), lambda b, t: (b, 0, 0)), act, act, act, act, act,
                  _resident((D_MODEL, D_MODEL)), _resident((D_MODEL, D_MODEL)), _resident((D_MODEL, D_MODEL))],
        out_specs=act,
        out_shape=jax.ShapeDtypeStruct((B, T, D_MODEL), F32),
        compiler_params=_cparams(("arbitrary", "arbitrary")),
        name="out_proj",
    )(x, mod3, u_a, o_b, z_b, g_a, g_b, w_dn16, w_mla16, w_out16)


def _lane_vec(v, off):
    return jnp.zeros((1, LANES), F32).at[0, off:off + v.shape[0]].set(v)


def _tiles(B, T, cached):
    if cached:
        whole = (B, T)
        return dict(proj=whole, prep=whole, out=whole, dn=dict(tm=T, bg=2, G=1), attn_tk=512)
    return dict(proj=(1, 256), prep=(1, 256), out=(1, 512), dn=dict(tm=256, bg=B, G=4), attn=(2048, 512))


def _layer(x, mod, conv_state, s0, past, prm, q_off):
    B, T, _ = x.shape
    tiles = _tiles(B, T, past is not None)
    mod3 = mod.reshape(B, 1, 3 * D_MODEL)
    qkv, z_a, qn_raw, qr_raw, z_b, g_a, g_b, ckv_raw, small = _in_proj(
        x, mod3, prm["norm_gain"], prm["w_pack"], *tiles["proj"])
    u_a, s_new, conv_new = _deltanet(qkv, small, z_a, conv_state, s0, prm["w_conv"], prm["alog_v"],
                                     prm["dtb_v"], prm["dn_out_norm"], **tiles["dn"])
    qcat, kcat, v, ckv_new, kr_new = _mla_prep(
        qn_raw, qr_raw, ckv_raw, small, prm["q_nope_norm"], prm["qr_gain"], prm["kv_norm"], prm["kr_gain"],
        prm["k_nope_norm"], prm["w_uk"], prm["w_uv"], *tiles["prep"], q_off)
    if past is None:
        o_b = _attn_prompt(qcat, kcat, v, *tiles["attn"])
    else:
        past_ckv, past_kr = past
        o_b = _attn_sample(qcat, kcat, v, past_ckv, past_kr, prm["k_nope_norm"], prm["w_uk"], prm["w_uv"],
                           tiles["attn_tk"])
    y = _out_proj(x, mod3, u_a, o_b, z_b, g_a, g_b, prm["w_o_dn"], prm["w_o_mla"], prm["w_out"], *tiles["out"])
    return y, conv_new, s_new, ckv_new, kr_new


def kernel(x_prompt, x_sample, c_prompt, c_sample, cache_ckv, cache_krope, state_delta, state_conv, norm_gain, w_ada, b_ada, w_in, w_conv, a_log, dt_bias, dn_out_norm, q_nope_norm, q_rope_norm, k_nope_norm, k_rope_norm, kv_norm, w_uk, w_uv, w_o_dn, w_o_mla, w_out):
    depth = w_in.shape[0]
    assert depth == 1, "single-layer configuration"
    l = 0
    B, T, _ = x_prompt.shape
    Bs, Ts, _ = x_sample.shape
    past_len = cache_ckv.shape[2]

    row = lambda v: v.reshape(1, -1).astype(F32)
    prm = dict(
        norm_gain=row(norm_gain[l]),
        w_pack=_pack_w_in(w_in[l]),
        w_conv=w_conv[l],
        alog_v=_lane_vec(a_log[l], ALPHA_OFF),
        dtb_v=_lane_vec(dt_bias[l], ALPHA_OFF),
        dn_out_norm=row(dn_out_norm[l]),
        q_nope_norm=row(q_nope_norm[l]),
        qr_gain=jnp.tile(row(q_rope_norm[l]), (1, LANES // QK_ROPE)),
        kv_norm=row(kv_norm[l]),
        kr_gain=_lane_vec(k_rope_norm[l], KR_OFF),
        k_nope_norm=row(k_nope_norm[l]),
        w_uk=w_uk[l].astype(BF16),
        w_uv=w_uv[l].astype(BF16),
        w_o_dn=w_o_dn[l].astype(BF16),
        w_o_mla=w_o_mla[l].astype(BF16),
        w_out=w_out[l].astype(BF16),
    )

    rows = B + Bs
    rows_pad = -(-rows // 8) * 8
    c_all = jnp.concatenate([c_prompt, c_sample, jnp.zeros((rows_pad - rows, D_MODEL), F32)], axis=0)
    mod = _ada(c_all, w_ada[l], b_ada[l].reshape(1, -1))

    zeros_conv = jnp.zeros((B, CONV_W - 1, DN_CONV_CH), F32)
    zeros_state = jnp.zeros((B, DN_HEADS, DN_DK, DN_DV), F32)
    yp, cvp, sdp, kvp, krp = _layer(x_prompt, mod[:B], zeros_conv, zeros_state, None, prm, q_off=0)
    ys, cvs, sds, kvs, krs = _layer(x_sample, mod[B:rows], state_conv[l], state_delta[l],
                                    (cache_ckv[l], cache_krope[l]), prm, q_off=past_len)
    st = lambda a: a[None]
    return (yp, ys, st(kvp), st(krp), st(sdp), st(cvp), st(kvs), st(krs), st(sds), st(cvs))
```

```python
import functools
import math

import jax
import jax.numpy as jnp
from jax import lax
from jax.experimental import pallas as pl
from jax.experimental.pallas import tpu as pltpu

D_MODEL = 1024
CHUNK = 64
EPS = 1e-6
DN_HEADS = 8
DN_DK = 128
DN_DV = 128
DN_QK = DN_HEADS * DN_DK
DN_V = DN_HEADS * DN_DV
DN_CONV_CH = 2 * DN_QK + DN_V
CONV_W = 4
MLA_HEADS = 8
QK_NOPE = 128
QK_ROPE = 64
QK_HEAD = QK_NOPE + QK_ROPE
V_HEAD = 128
KV_RANK = 512
MLA_Q = MLA_HEADS * QK_HEAD
MLA_V = MLA_HEADS * V_HEAD
ROPE_THETA = 10000.0

LANES = 128
QK_CAT = 256
ATTN_SUB = 256
ATTN_LOOKAHEAD = 2
ATTN_TILES_PER_TRIP = 4
KR_OFF = 0
BETA_OFF = QK_ROPE
ALPHA_OFF = QK_ROPE + DN_HEADS
VMEM_LIMIT = 56 * 1024 * 1024

F32 = jnp.float32
BF16 = jnp.bfloat16
HI = lax.Precision.HIGHEST


def _dot(a, b):
    return jnp.dot(a, b, preferred_element_type=F32)


def _dot_nt(a, b, precision=None):
    return lax.dot_general(a, b, (((1,), (1,)), ((), ())), preferred_element_type=F32, precision=precision)


def _dot_tn(a, b):
    return lax.dot_general(a, b, (((0,), (0,)), ((), ())), preferred_element_type=F32)


def _cparams(sem):
    return pltpu.CompilerParams(dimension_semantics=sem, vmem_limit_bytes=VMEM_LIMIT)


def _resident(shape):
    nd = len(shape)
    return pl.BlockSpec(shape, lambda *_: (0,) * nd, pipeline_mode=pl.Buffered(1))


def _ada_kernel(c_ref, w_ref, b_ref, o_ref):
    o_ref[...] = jnp.dot(c_ref[...], w_ref[...], preferred_element_type=F32, precision=HI) + b_ref[...]


def _ada(c_all, w_ada, b_ada):
    rows = c_all.shape[0]
    tn = 512
    return pl.pallas_call(
        _ada_kernel,
        grid=(3 * D_MODEL // tn,),
        in_specs=[pl.BlockSpec((rows, D_MODEL), lambda j: (0, 0)),
                  pl.BlockSpec((D_MODEL, tn), lambda j: (0, j)),
                  pl.BlockSpec((1, tn), lambda j: (0, j))],
        out_specs=pl.BlockSpec((rows, tn), lambda j: (0, j)),
        out_shape=jax.ShapeDtypeStruct((rows, 3 * D_MODEL), F32),
        compiler_params=_cparams(("arbitrary",)),
        name="ada",
    )(c_all, w_ada, b_ada)


_PROJ_WIDTHS = (LANES, KV_RANK, MLA_HEADS * QK_ROPE, MLA_HEADS * QK_NOPE, DN_CONV_CH, DN_V, MLA_V, D_MODEL, D_MODEL)


def _pack_w_in(w_in):
    o = 0
    qkv = w_in[:, o:o + DN_CONV_CH]; o += DN_CONV_CH
    z_a = w_in[:, o:o + DN_V]; o += DN_V
    beta = w_in[:, o:o + DN_HEADS]; o += DN_HEADS
    alpha = w_in[:, o:o + DN_HEADS]; o += DN_HEADS
    q = w_in[:, o:o + MLA_Q].reshape(D_MODEL, MLA_HEADS, QK_HEAD); o += MLA_Q
    ckv = w_in[:, o:o + KV_RANK]; o += KV_RANK
    kr = w_in[:, o:o + QK_ROPE]; o += QK_ROPE
    z_b = w_in[:, o:o + MLA_V]; o += MLA_V
    g_a = w_in[:, o:o + D_MODEL]; o += D_MODEL
    g_b = w_in[:, o:o + D_MODEL]
    qn = q[:, :, :QK_NOPE].reshape(D_MODEL, MLA_HEADS * QK_NOPE)
    qr = q[:, :, QK_NOPE:].reshape(D_MODEL, MLA_HEADS * QK_ROPE)
    pad = jnp.zeros((D_MODEL, LANES - QK_ROPE - 2 * DN_HEADS), w_in.dtype)
    small = jnp.concatenate([kr, beta, alpha, pad], axis=1)
    return tuple(w.astype(BF16) for w in (small, ckv, qr, qn, qkv, z_a, z_b, g_a, g_b))


def _in_proj_kernel(x_ref, mod_ref, gain_ref,
                    w_small, w_ckv, w_qr, w_qn, w_qkv, w_za, w_zb, w_ga, w_gb,
                    qng_ref, qrg_ref, kvg_ref, krg_ref, kng_ref, wuk_ref, wuv_ref,
                    small_ref, qkv_ref, za_ref, zb_ref, ga_ref, gb_ref,
                    qcat_ref, kcat_ref, v_ref, ckvn_ref, krn_ref, rope_ref, *, q_off):
    bb, tm, d = x_ref.shape
    rows = bb * tm

    @pl.when((pl.program_id(0) == 0) & (pl.program_id(1) == 0))
    def _():
        off = (lax.broadcasted_iota(jnp.int32, (rows, LANES), 0) % tm).astype(F32) * _rope_inv_freq((rows, LANES))
        rope_ref[0] = jnp.cos(off)
        rope_ref[1] = jnp.sin(off)

    x = x_ref[...]
    ms = jnp.mean(x * x, axis=-1, keepdims=True)
    y = x * lax.rsqrt(ms + EPS) * gain_ref[...]
    shift = mod_ref[:, :, 0:d]
    scale = mod_ref[:, :, d:2 * d]
    h = (y * (1.0 + scale) + shift).astype(BF16).reshape(rows, d)

    def project(w_ref, o_ref):
        o_ref[...] = _dot(h, w_ref[...]).astype(o_ref.dtype).reshape(o_ref.shape)

    sm = _dot(h, w_small[...])
    small_ref[...] = sm.reshape(bb, tm, LANES)
    ckv = _dot(h, w_ckv[...])
    qr_all = _dot(h, w_qr[...])
    qn_all = _dot(h, w_qn[...])
    project(w_qkv, qkv_ref)

    t0 = pl.program_id(1) * tm
    qscale = QK_HEAD ** -0.5 * math.log2(math.e)
    base = (t0 + q_off).astype(F32) * _rope_inv_freq((1, LANES))
    cos_a, sin_a = jnp.cos(base), jnp.sin(base)
    cos = cos_a * rope_ref[0] - sin_a * rope_ref[1]
    sin = sin_a * rope_ref[0] + cos_a * rope_ref[1]
    lane = lax.broadcasted_iota(jnp.int32, (rows, LANES), 1)
    low_half = lane < QK_ROPE
    first = (lane % QK_ROPE) < (QK_ROPE // 2)

    def rope(z):
        rot = jnp.where(first, -pltpu.roll(z, LANES - QK_ROPE // 2, 1), pltpu.roll(z, QK_ROPE // 2, 1))
        return z * cos + rot * sin

    def rms64(z):
        zz = z * z
        s_lo = jnp.sum(jnp.where(low_half, zz, 0.0), axis=-1, keepdims=True)
        s_hi = jnp.sum(jnp.where(low_half, 0.0, zz), axis=-1, keepdims=True)
        return lax.rsqrt(jnp.where(low_half, s_lo, s_hi) * (1.0 / QK_ROPE) + EPS)

    ckvn = ckv * lax.rsqrt(jnp.mean(ckv * ckv, axis=-1, keepdims=True) + EPS) * kvg_ref[...]
    ckvn_ref[...] = ckvn.reshape(bb, tm, KV_RANK)
    ckvn16 = ckvn.astype(BF16)
    kr = rope(sm * rms64(sm) * krg_ref[...])
    krn_ref[...] = kr[:, :QK_ROPE].reshape(bb, tm, QK_ROPE)
    kr_pad16 = jnp.where(low_half, kr, 0.0).astype(BF16).reshape(bb, tm, LANES)

    project(w_za, za_ref)
    k_raw = _dot(ckvn16, wuk_ref[...])
    v_ref[...] = _dot(ckvn16, wuv_ref[...]).astype(BF16).reshape(bb, tm, MLA_V)
    project(w_zb, zb_ref)

    qn = _head_rms(qn_all, qng_ref[...] * qscale, QK_NOPE)
    for c in range(MLA_HEADS // 2):
        z = qr_all[:, c * LANES:(c + 1) * LANES]
        z = rope(z * rms64(z) * qrg_ref[...]) * qscale
        even = jnp.where(low_half, z, 0.0)
        odd = jnp.where(low_half, pltpu.roll(z, QK_ROPE, 1), 0.0)
        for hh, part in ((2 * c, even), (2 * c + 1, odd)):
            qcat_ref[:, :, hh * QK_CAT:hh * QK_CAT + QK_NOPE] = qn[hh].astype(BF16).reshape(bb, tm, QK_NOPE)
            qcat_ref[:, :, hh * QK_CAT + QK_NOPE:(hh + 1) * QK_CAT] = part.astype(BF16).reshape(bb, tm, LANES)
    project(w_ga, ga_ref)

    kn = _head_rms(k_raw, kng_ref[...], QK_NOPE)
    for hh in range(MLA_HEADS):
        kcat_ref[:, :, hh * QK_CAT:hh * QK_CAT + QK_NOPE] = kn[hh].astype(BF16).reshape(bb, tm, QK_NOPE)
        kcat_ref[:, :, hh * QK_CAT + QK_NOPE:(hh + 1) * QK_CAT] = kr_pad16
    project(w_gb, gb_ref)


def _in_proj(x, mod3, gain, w_pack, mla, bb, tm, q_off):
    B, T, _ = x.shape
    row = lambda b, t: (b, t, 0)
    outs = ((LANES, F32), (DN_CONV_CH, BF16), (DN_V, BF16), (MLA_V, BF16), (D_MODEL, BF16), (D_MODEL, BF16),
            (MLA_HEADS * QK_CAT, BF16), (MLA_HEADS * QK_CAT, BF16), (MLA_V, BF16), (KV_RANK, F32), (QK_ROPE, F32))
    return pl.pallas_call(
        functools.partial(_in_proj_kernel, q_off=q_off),
        grid=(B // bb, T // tm),
        in_specs=[pl.BlockSpec((bb, tm, D_MODEL), row),
                  pl.BlockSpec((bb, 1, 3 * D_MODEL), lambda b, t: (b, 0, 0)),
                  _resident((1, D_MODEL))]
        + [_resident((D_MODEL, w)) for w in _PROJ_WIDTHS]
        + [_resident((1, QK_NOPE)), _resident((1, LANES)), _resident((1, KV_RANK)), _resident((1, LANES)),
           _resident((1, QK_NOPE)), _resident((KV_RANK, MLA_HEADS * QK_NOPE)), _resident((KV_RANK, MLA_V))],
        out_specs=[pl.BlockSpec((bb, tm, w), row) for w, _ in outs],
        out_shape=[jax.ShapeDtypeStruct((B, T, w), dt) for w, dt in outs],
        scratch_shapes=[pltpu.VMEM((2, bb * tm, LANES), F32)],
        compiler_params=_cparams(("arbitrary", "arbitrary")),
        name="in_proj",
    )(x, mod3, gain, *w_pack, *mla)


def _softplus(x):
    return jnp.maximum(x, 0.0) + jnp.log(1.0 + jnp.exp(-jnp.abs(x)))


def _dn_prep_kernel(qkv_ref, prev_ref, cs_ref, small_ref, wconv_ref, alog_ref, dtb_ref,
                    wq_ref, u_ref, kd_ref, attn_ref, egl_ref, *, C):
    t = pl.program_id(1)
    tm = qkv_ref.shape[1]
    nc = tm // C
    pad = prev_ref.shape[1]

    x16 = qkv_ref[0]
    hist = jnp.where(t == 0, cs_ref[0], prev_ref[0].astype(F32))
    hist_hi = hist.astype(BF16)
    rem = hist - hist_hi.astype(F32)
    hist_mid = rem.astype(BF16)
    hist_lo = (rem - hist_mid.astype(F32)).astype(BF16)
    full16 = jnp.concatenate([hist_hi, hist_mid, hist_lo, x16], axis=0)
    n_sh = CONV_W - 1
    srow = lax.broadcasted_iota(jnp.int32, (n_sh * tm, 3 * pad + tm), 0)
    scol = lax.broadcasted_iota(jnp.int32, (n_sh * tm, 3 * pad + tm), 1)
    src = srow % tm + srow // tm + pad - n_sh
    shift_sel = ((scol == src + 2 * pad) | ((scol == src) & (scol < pad))
                 | ((scol == src + pad) & (scol < 2 * pad))).astype(BF16)
    shifted = _dot(shift_sel, full16)
    conv = x16.astype(F32) * wconv_ref[n_sh:CONV_W, :]
    for i in range(n_sh):
        conv = conv + shifted[i * tm:(i + 1) * tm, :] * wconv_ref[i:i + 1, :]
    act = conv * jax.nn.sigmoid(conv)

    sm = small_ref[0]
    beta_all = jax.nn.sigmoid(sm)
    g_all = -jnp.exp(alog_ref[...]) * _softplus(sm + dtb_ref[...])
    rt = lax.broadcasted_iota(jnp.int32, (tm, tm), 0)
    ct = lax.broadcasted_iota(jnp.int32, (tm, tm), 1)
    chunk_tri = ((rt // C == ct // C) & (rt >= ct)).astype(F32)
    gcum = jnp.dot(chunk_tri, g_all, preferred_element_type=F32, precision=HI)
    sel = (lax.broadcasted_iota(jnp.int32, (DN_HEADS, LANES), 1)
           == lax.broadcasted_iota(jnp.int32, (DN_HEADS, LANES), 0) + ALPHA_OFF).astype(F32)
    gcum_t = _dot_nt(sel, gcum, precision=HI)

    def per_head_lanes(x, off):
        hi = x.astype(BF16)
        r1 = x - hi.astype(F32)
        mid = r1.astype(BF16)
        lo = (r1 - mid.astype(F32)).astype(BF16)
        erow = lax.broadcasted_iota(jnp.int32, (3 * LANES, DN_QK), 0) % LANES
        ecol = lax.broadcasted_iota(jnp.int32, (3 * LANES, DN_QK), 1) // DN_DK
        return _dot(jnp.concatenate([hi, mid, lo], axis=1), (erow == ecol + off).astype(BF16))

    g_b = per_head_lanes(gcum, ALPHA_OFF)
    beta_b = per_head_lanes(beta_all, BETA_OFF)
    glast_b = jnp.concatenate(
        [jnp.broadcast_to(g_b[c * C + C - 1:(c + 1) * C, :], (C, DN_QK)) for c in range(nc)], axis=0)
    eg_b = jnp.exp(g_b)
    kdf_b = jnp.exp(glast_b - g_b)
    for c in range(nc):
        egl_ref[0, c] = jnp.exp(g_b[c * C + C - 1:(c + 1) * C, :])

    ri = lax.broadcasted_iota(jnp.int32, (C, C), 0)
    ci = lax.broadcasted_iota(jnp.int32, (C, C), 1)
    tri_incl = ri >= ci
    tri_strict = ri > ci
    eye = (ri == ci).astype(F32)
    pair_masks = []
    m = 1
    while m < C:
        pair_masks.append((ri // (2 * m) == ci // (2 * m)) & (ri // m != ci // m))
        m *= 2

    heads = range(DN_HEADS)
    qn, kn = [], []
    for h in heads:
        qh = act[:, h * DN_DK:(h + 1) * DN_DK]
        kh = act[:, DN_QK + h * DN_DK:DN_QK + (h + 1) * DN_DK]
        qn.append(qh * lax.rsqrt(jnp.sum(qh * qh, axis=-1, keepdims=True) + EPS) * (DN_DK ** -0.5))
        kn.append(kh * lax.rsqrt(jnp.sum(kh * kh, axis=-1, keepdims=True) + EPS))

    hl = lambda h: slice(h * DN_DK, (h + 1) * DN_DK)
    k16 = [kn[h].astype(BF16) for h in heads]
    q16 = [qn[h].astype(BF16) for h in heads]
    kb = [kn[h] * beta_b[:, hl(h)] for h in heads]
    kb16 = [kb[h].astype(BF16) for h in heads]
    kbe16 = [(kb[h] * eg_b[:, hl(h)]).astype(BF16) for h in heads]
    vb16 = [(act[:, 2 * DN_QK + h * DN_DV:2 * DN_QK + (h + 1) * DN_DV] * beta_b[:, hl(h)]).astype(BF16)
            for h in heads]
    qe16 = [(qn[h] * eg_b[:, hl(h)]).astype(BF16) for h in heads]
    for h in heads:
        kd_ref[0, :, hl(h)] = (kn[h] * kdf_b[:, hl(h)]).astype(BF16)

    attn_ref[...] = jnp.zeros_like(attn_ref)
    items = [(c, h) for c in range(nc) for h in heads]
    rows = lambda c: slice(c * C, (c + 1) * C)
    decay, rhs, qk = [], [], []
    for c, h in items:
        r = rows(c)
        gc = g_b[r, h * DN_DK:h * DN_DK + C]
        decay.append(jnp.exp(jnp.where(tri_incl, gc - gcum_t[h:h + 1, r], -1e30)))
        rhs.append(jnp.concatenate([kbe16[h][r], vb16[h][r]], axis=1))
        wq_ref[0, c, C:2 * C, hl(h)] = qe16[h][r]
        qk.append(_dot_nt(jnp.concatenate([kb16[h][r], q16[h][r]], axis=0), k16[h][r]))

    lmat = []
    for i, (c, h) in enumerate(items):
        lmat.append(jnp.where(tri_strict, qk[i][:C] * decay[i], 0.0))
        attn_ref[0, rows(c), h * DN_DK:h * DN_DK + C] = (qk[i][C:] * decay[i]).astype(BF16)

    pinv = [eye - jnp.where(pair_masks[0], l, 0.0) for l in lmat]
    for mask in pair_masks[1:]:
        p16 = [p.astype(BF16) for p in pinv]
        tmp = [_dot(p16[i], jnp.where(mask, lmat[i], 0.0).astype(BF16)).astype(BF16) for i in range(len(items))]
        pinv = [pinv[i] - _dot(tmp[i], p16[i]) for i in range(len(items))]

    for i, (c, h) in enumerate(items):
        wu = _dot(pinv[i].astype(BF16), rhs[i])
        wq_ref[0, c, 0:C, hl(h)] = wu[:, :DN_DK].astype(BF16)
        u_ref[0, rows(c), hl(h)] = wu[:, DN_DK:]


def _dn_scan_kernel(wq_ref, u_ref, kd_ref, attn_ref, egl_ref, za_ref, s0_ref, onorm_ref,
                    ua_ref, sfin_ref, s_ref, *, C):
    n = pl.program_id(1)
    bg, G = wq_ref.shape[0], wq_ref.shape[1]

    @pl.when(n == 0)
    def _():
        s_ref[...] = s0_ref[...]

    chains = [(b, h) for b in range(bg) for h in range(DN_HEADS)]
    for g in range(G):
        r = slice(g * C, (g + 1) * C)
        s_old = [s_ref[b, h] for b, h in chains]
        s16 = [s.astype(BF16) for s in s_old]
        ws = [_dot(wq_ref[b, g, :, h * DN_DK:(h + 1) * DN_DK], s16[i]) for i, (b, h) in enumerate(chains)]
        v16 = [(u_ref[b, r, h * DN_DV:(h + 1) * DN_DV] - ws[i][:C]).astype(BF16)
               for i, (b, h) in enumerate(chains)]
        for i, (b, h) in enumerate(chains):
            lo = h * DN_DK
            s_ref[b, h] = s_old[i] * egl_ref[b, g, :, lo:lo + DN_DK] + _dot_tn(kd_ref[b, r, lo:lo + DN_DK], v16[i])
        for i, (b, h) in enumerate(chains):
            lo = h * DN_DV
            o = ws[i][C:] + _dot(attn_ref[b, r, lo:lo + C], v16[i])
            o = o * lax.rsqrt(jnp.mean(o * o, axis=-1, keepdims=True) + EPS) * onorm_ref[...]
            z = za_ref[b, r, lo:lo + DN_DV].astype(F32)
            ua_ref[b, r, lo:lo + DN_DV] = (o * (z * jax.nn.sigmoid(z))).astype(BF16)

    @pl.when(n == pl.num_programs(1) - 1)
    def _():
        sfin_ref[...] = s_ref[...]


def _deltanet(qkv, small, z_a, conv_state, s0, w_conv, alog_v, dtb_v, onorm, tm, bg, G):
    B, T, _ = qkv.shape
    C = min(CHUNK, T)
    N = T // C
    nc = tm // C
    hist_rows = 16
    cs = jnp.pad(conv_state, ((0, 0), (hist_rows - (CONV_W - 1), 0), (0, 0)))
    tile = lambda b, t: (b, t, 0)
    prev = lambda b, t: (b, jnp.maximum(t * (tm // hist_rows) - 1, 0), 0)
    wq, u, kd, attn, egl = pl.pallas_call(
        functools.partial(_dn_prep_kernel, C=C),
        grid=(B, T // tm),
        in_specs=[pl.BlockSpec((1, tm, DN_CONV_CH), tile),
                  pl.BlockSpec((1, hist_rows, DN_CONV_CH), prev),
                  pl.BlockSpec((1, hist_rows, DN_CONV_CH), lambda b, t: (b, 0, 0)),
                  pl.BlockSpec((1, tm, LANES), tile),
                  _resident((CONV_W, DN_CONV_CH)),
                  _resident((1, LANES)),
                  _resident((1, LANES))],
        out_specs=[pl.BlockSpec((1, nc, 2 * C, DN_QK), lambda b, t: (b, t, 0, 0)),
                   pl.BlockSpec((1, tm, DN_V), tile),
                   pl.BlockSpec((1, tm, DN_QK), tile),
                   pl.BlockSpec((1, tm, DN_V), tile),
                   pl.BlockSpec((1, nc, 1, DN_QK), lambda b, t: (b, t, 0, 0))],
        out_shape=[jax.ShapeDtypeStruct((B, N, 2 * C, DN_QK), BF16),
                   jax.ShapeDtypeStruct((B, T, DN_V), F32),
                   jax.ShapeDtypeStruct((B, T, DN_QK), BF16),
                   jax.ShapeDtypeStruct((B, T, DN_V), BF16),
                   jax.ShapeDtypeStruct((B, N, 1, DN_QK), F32)],
        compiler_params=_cparams(("arbitrary", "arbitrary")),
        name="dn_prep",
    )(qkv, qkv, cs, small, w_conv, alog_v, dtb_v)

    grp = lambda b, n: (b, n, 0)
    grp4 = lambda b, n: (b, n, 0, 0)
    state = pl.BlockSpec((bg, DN_HEADS, DN_DK, DN_DV), lambda b, n: (b, 0, 0, 0))
    u_a, s_new = pl.pallas_call(
        functools.partial(_dn_scan_kernel, C=C),
        grid=(B // bg, N // G),
        in_specs=[pl.BlockSpec((bg, G, 2 * C, DN_QK), grp4),
                  pl.BlockSpec((bg, G * C, DN_V), grp),
                  pl.BlockSpec((bg, G * C, DN_QK), grp),
                  pl.BlockSpec((bg, G * C, DN_V), grp),
                  pl.BlockSpec((bg, G, 1, DN_QK), grp4),
                  pl.BlockSpec((bg, G * C, DN_V), grp),
                  state,
                  _resident((1, DN_DV))],
        out_specs=[pl.BlockSpec((bg, G * C, DN_V), grp), state],
        out_shape=[jax.ShapeDtypeStruct((B, T, DN_V), BF16),
                   jax.ShapeDtypeStruct((B, DN_HEADS, DN_DK, DN_DV), F32)],
        scratch_shapes=[pltpu.VMEM((bg, DN_HEADS, DN_DK, DN_DV), F32)],
        compiler_params=_cparams(("arbitrary", "arbitrary")),
        name="dn_scan",
    )(wq, u, kd, attn, egl, z_a, s0, onorm)
    conv_new = qkv[:, T - (CONV_W - 1):, :].astype(F32)
    return u_a, s_new, conv_new


def _head_rms(x, gain_row, width):
    outs = []
    for h in range(x.shape[1] // width):
        xh = x[:, h * width:(h + 1) * width]
        outs.append(xh * lax.rsqrt(jnp.mean(xh * xh, axis=-1, keepdims=True) + EPS) * gain_row)
    return outs


def _rope_inv_freq(shape):
    half = QK_ROPE // 2
    fidx = (lax.broadcasted_iota(jnp.int32, shape, 1) % half).astype(F32)
    return jnp.exp(fidx * (-math.log(ROPE_THETA) / half))


def _chunk_mask(qpos0, kpos0, tq, tk):
    qc = (lax.broadcasted_iota(jnp.int32, (tq, tk), 0) + qpos0) // CHUNK
    kc = (lax.broadcasted_iota(jnp.int32, (tq, tk), 1) + kpos0) // CHUNK
    return kc <= qc


def _attn_prompt_kernel(q_ref, k_ref, v_ref, o_ref, m_ref, l_ref, acc_ref, *, tq, tk, sub):
    i = pl.program_id(2)
    nsub = tq // sub
    ratio = tq // tk
    m_ref[...] = jnp.full_like(m_ref, -1e30)
    l_ref[...] = jnp.zeros_like(l_ref)
    acc_ref[...] = jnp.zeros_like(acc_ref)

    def scores(r, k):
        return _dot_nt(q_ref[0, r * sub:(r + 1) * sub, :], k)

    def softmax_pv(r, sr, v, mask):
        rows = slice(r * sub, (r + 1) * sub)
        if mask is not None:
            sr = jnp.where(mask, sr, -1e30)
        m_old = m_ref[rows, :]
        m_new = jnp.maximum(m_old, jnp.max(sr, axis=-1, keepdims=True))
        alpha = jnp.exp2(m_old - m_new)
        p = jnp.exp2(sr - jnp.tile(m_new, (1, tk // LANES)))
        psum = p[:, 0:LANES]
        for c in range(1, tk // LANES):
            psum = psum + p[:, c * LANES:(c + 1) * LANES]
        l_ref[rows, :] = alpha * l_ref[rows, :] + psum
        acc_ref[rows, :] = alpha * acc_ref[rows, :] + _dot(p.astype(BF16), v)
        m_ref[rows, :] = m_new

    def run(j0, items):
        kv = {}
        for d in sorted({d for d, _, _ in items}):
            start = pl.multiple_of((j0 + d) * tk, tk)
            kv[d] = (k_ref[0, pl.ds(start, tk), :], v_ref[0, pl.ds(start, tk), :])
        s = {n: scores(items[n][1], kv[items[n][0]][0]) for n in range(min(ATTN_LOOKAHEAD, len(items)))}
        for n, (d, r, mask) in enumerate(items):
            ahead = n + ATTN_LOOKAHEAD
            if ahead < len(items):
                s[ahead] = scores(items[ahead][1], kv[items[ahead][0]][0])
            softmax_pv(r, s.pop(n), kv[d][1], mask)

    per_trip = math.gcd(ratio, ATTN_TILES_PER_TRIP)
    full = [(d, r, None) for d in range(per_trip) for r in range(nsub)]

    def body(jj, carry):
        run(jj * per_trip, full)
        return carry

    lax.fori_loop(0, i * (ratio // per_trip), body, 0)
    diag = []
    for d in range(ratio):
        for r in range(nsub):
            q_lo, q_hi = (r * sub) // CHUNK, (r * sub + sub - 1) // CHUNK
            k_lo, k_hi = (d * tk) // CHUNK, (d * tk + tk - 1) // CHUNK
            if k_lo > q_hi:
                continue
            diag.append((d, r, None if k_hi <= q_lo else _chunk_mask(r * sub, d * tk, sub, tk)))
    run(i * ratio, diag)
    l = jnp.sum(l_ref[...], axis=-1, keepdims=True)
    o_ref[0] = (acc_ref[...] / l).astype(BF16)


def _attn_prompt(qcat, kcat, v, tq, tk):
    B, T, _ = v.shape
    return pl.pallas_call(
        functools.partial(_attn_prompt_kernel, tq=tq, tk=tk, sub=ATTN_SUB),
        grid=(B, MLA_HEADS, T // tq),
        in_specs=[pl.BlockSpec((1, tq, QK_CAT), lambda b, h, i: (b, i, h)),
                  pl.BlockSpec((1, T, QK_CAT), lambda b, h, i: (b, 0, h)),
                  pl.BlockSpec((1, T, V_HEAD), lambda b, h, i: (b, 0, h))],
        out_specs=pl.BlockSpec((1, tq, V_HEAD), lambda b, h, i: (b, i, h)),
        out_shape=jax.ShapeDtypeStruct((B, T, MLA_V), BF16),
        scratch_shapes=[pltpu.VMEM((tq, LANES), F32), pltpu.VMEM((tq, LANES), F32),
                        pltpu.VMEM((tq, V_HEAD), F32)],
        compiler_params=_cparams(("arbitrary", "arbitrary", "arbitrary")),
        name="attn_prompt",
    )(qcat, kcat, v)


def _attn_sample_kernel(q_ref, kn_ref, vn_ref, ckv_ref, kr_ref, kng_ref, wuk_ref, wuv_ref, o_ref,
                        m_ref, l_ref, acc_ref, *, tk):
    T = q_ref.shape[1]
    P = ckv_ref.shape[1]
    R = MLA_HEADS * T
    q = q_ref[0]
    row_head = lax.broadcasted_iota(jnp.int32, (R, 1), 0) // T
    q_chunk = (lax.broadcasted_iota(jnp.int32, (R, 1), 0) % T + P) // CHUNK

    def stacked(x, width):
        lane_head = lax.broadcasted_iota(jnp.int32, (R, x.shape[1]), 1) // width
        return jnp.where(lane_head == row_head, jnp.concatenate([x] * MLA_HEADS, axis=0), jnp.zeros((), x.dtype))

    q_nope = stacked(jnp.concatenate([q[:, h * QK_CAT:h * QK_CAT + QK_NOPE] for h in range(MLA_HEADS)], axis=1),
                     QK_NOPE)
    q_rope = jnp.concatenate([q[:, h * QK_CAT + QK_NOPE:h * QK_CAT + QK_HEAD] for h in range(MLA_HEADS)],
                             axis=0)

    def update(s, k_chunk):
        s = jnp.where(k_chunk <= q_chunk, s, -1e30)
        m_old = m_ref[...]
        m_new = jnp.maximum(m_old, jnp.max(s, axis=-1, keepdims=True))
        alpha = jnp.exp2(m_old - m_new)
        p = jnp.exp2(s - m_new)
        l_ref[...] = alpha * l_ref[...] + jnp.sum(p, axis=-1, keepdims=True)
        m_ref[...] = m_new
        return alpha, p.astype(BF16)

    m_ref[...] = jnp.full_like(m_ref, -1e30)
    l_ref[...] = jnp.zeros_like(l_ref)
    acc_ref[...] = jnp.zeros_like(acc_ref)

    def body(c, carry):
        start = pl.multiple_of(c * tk, tk)
        ckv16 = ckv_ref[0, pl.ds(start, tk), :].astype(BF16)
        kn = _head_rms(_dot(ckv16, wuk_ref[...]), kng_ref[...], QK_NOPE)
        kn16 = jnp.concatenate([x.astype(BF16) for x in kn], axis=1)
        s = _dot_nt(q_nope, kn16) + _dot_nt(q_rope, kr_ref[0, pl.ds(start, tk), :].astype(BF16))
        k_chunk = (lax.broadcasted_iota(jnp.int32, (1, tk), 1) + start) // CHUNK
        alpha, p16 = update(s, k_chunk)
        acc_ref[...] = alpha * acc_ref[...] + _dot(p16, ckv16)
        return carry

    lax.fori_loop(0, P // tk, body, 0)

    s_new = _dot_nt(stacked(q, QK_CAT), kn_ref[0])
    alpha, p16 = update(s_new, (lax.broadcasted_iota(jnp.int32, (1, T), 1) + P) // CHUNK)
    pc16 = (alpha * acc_ref[...]).astype(BF16)
    inv_l = 1.0 / l_ref[...]
    for h in range(MLA_HEADS):
        rows = slice(h * T, (h + 1) * T)
        lanes = slice(h * V_HEAD, (h + 1) * V_HEAD)
        o = _dot(pc16[rows], wuv_ref[:, lanes]) + _dot(p16[rows], vn_ref[0, :, lanes])
        o_ref[0, :, lanes] = (o * inv_l[rows]).astype(BF16)


def _attn_sample(qcat, kcat_new, v_new, past_ckv, past_kr, kn_gain, w_uk16, w_uv16, tk):
    B, T, _ = v_new.shape
    P = past_ckv.shape[1]
    R = MLA_HEADS * T
    perb = lambda b: (b, 0, 0)
    return pl.pallas_call(
        functools.partial(_attn_sample_kernel, tk=tk),
        grid=(B,),
        in_specs=[pl.BlockSpec((1, T, MLA_HEADS * QK_CAT), perb),
                  pl.BlockSpec((1, T, MLA_HEADS * QK_CAT), perb),
                  pl.BlockSpec((1, T, MLA_V), perb),
                  pl.BlockSpec((1, P, KV_RANK), perb),
                  pl.BlockSpec((1, P, QK_ROPE), perb),
                  _resident((1, QK_NOPE)),
                  _resident((KV_RANK, MLA_HEADS * QK_NOPE)),
                  _resident((KV_RANK, MLA_V))],
        out_specs=pl.BlockSpec((1, T, MLA_V), perb),
        out_shape=jax.ShapeDtypeStruct((B, T, MLA_V), BF16),
        scratch_shapes=[pltpu.VMEM((R, 1), F32), pltpu.VMEM((R, 1), F32), pltpu.VMEM((R, KV_RANK), F32)],
        compiler_params=_cparams(("arbitrary",)),
        name="attn_sample",
    )(qcat, kcat_new, v_new, past_ckv, past_kr, kn_gain, w_uk16, w_uv16)


def _out_kernel(x_ref, mod_ref, ua_ref, ob_ref, zb_ref, ga_ref, gb_ref, wdn_ref, wmla_ref, wout_ref, y_ref):
    bb, tm, d = x_ref.shape
    rows = bb * tm
    zb = zb_ref[...].astype(F32)
    ub = (ob_ref[...].astype(F32) * (zb * jax.nn.sigmoid(zb))).astype(BF16).reshape(rows, d)
    ya = _dot(ua_ref[...].reshape(rows, d), wdn_ref[...])
    yb = _dot(ub, wmla_ref[...])
    ga = jax.nn.sigmoid(ga_ref[...].astype(F32)).reshape(rows, d)
    gb = jax.nn.sigmoid(gb_ref[...].astype(F32)).reshape(rows, d)
    merged = (ga * ya + gb * yb).astype(BF16)
    out = _dot(merged, wout_ref[...]).reshape(bb, tm, d)
    gate = mod_ref[:, :, 2 * d:3 * d]
    y_ref[...] = x_ref[...] + gate * out


def _out_proj(x, mod3, u_a, o_b, z_b, g_a, g_b, w_dn16, w_mla16, w_out16, bb, tm):
    B, T, _ = x.shape
    row = lambda b, t: (b, t, 0)
    act = pl.BlockSpec((bb, tm, D_MODEL), row)
    return pl.pallas_call(
        _out_kernel,
        grid=(B // bb, T // tm),
        in_specs=[act, pl.BlockSpec((bb, 1, 3 * D_MODEL), lambda b, t: (b, 0, 0)), act, act, act, act, act,
                  _resident((D_MODEL, D_MODEL)), _resident((D_MODEL, D_MODEL)), _resident((D_MODEL, D_MODEL))],
        out_specs=act,
        out_shape=jax.ShapeDtypeStruct((B, T, D_MODEL), F32),
        compiler_params=_cparams(("arbitrary", "arbitrary")),
        name="out_proj",
    )(x, mod3, u_a, o_b, z_b, g_a, g_b, w_dn16, w_mla16, w_out16)


def _lane_vec(v, off):
    return jnp.zeros((1, LANES), F32).at[0, off:off + v.shape[0]].set(v)


def _tiles(B, T, cached):
    if cached:
        whole = (B, T)
        return dict(proj=whole, out=whole, dn=dict(tm=T, bg=2, G=1), attn_tk=512)
    return dict(proj=(1, 256), out=(1, 512), dn=dict(tm=256, bg=B, G=4), attn=(2048, 512))


def _layer(x, mod, conv_state, s0, past, prm, q_off):
    B, T, _ = x.shape
    tiles = _tiles(B, T, past is not None)
    mod3 = mod.reshape(B, 1, 3 * D_MODEL)
    mla = (prm["q_nope_norm"], prm["qr_gain"], prm["kv_norm"], prm["kr_gain"], prm["k_nope_norm"],
           prm["w_uk"], prm["w_uv"])
    small, qkv, z_a, z_b, g_a, g_b, qcat, kcat, v, ckv_new, kr_new = _in_proj(
        x, mod3, prm["norm_gain"], prm["w_pack"], mla, *tiles["proj"], q_off)
    u_a, s_new, conv_new = _deltanet(qkv, small, z_a, conv_state, s0, prm["w_conv"], prm["alog_v"],
                                     prm["dtb_v"], prm["dn_out_norm"], **tiles["dn"])
    if past is None:
        o_b = _attn_prompt(qcat, kcat, v, *tiles["attn"])
    else:
        past_ckv, past_kr = past
        o_b = _attn_sample(qcat, kcat, v, past_ckv, past_kr, prm["k_nope_norm"], prm["w_uk"], prm["w_uv"],
                           tiles["attn_tk"])
    y = _out_proj(x, mod3, u_a, o_b, z_b, g_a, g_b, prm["w_o_dn"], prm["w_o_mla"], prm["w_out"], *tiles["out"])
    return y, conv_new, s_new, ckv_new, kr_new


def kernel(x_prompt, x_sample, c_prompt, c_sample, cache_ckv, cache_krope, state_delta, state_conv, norm_gain, w_ada, b_ada, w_in, w_conv, a_log, dt_bias, dn_out_norm, q_nope_norm, q_rope_norm, k_nope_norm, k_rope_norm, kv_norm, w_uk, w_uv, w_o_dn, w_o_mla, w_out):
    depth = w_in.shape[0]
    assert depth == 1, "single-layer configuration"
    l = 0
    B, T, _ = x_prompt.shape
    Bs, Ts, _ = x_sample.shape
    past_len = cache_ckv.shape[2]

    row = lambda v: v.reshape(1, -1).astype(F32)
    prm = dict(
        norm_gain=row(norm_gain[l]),
        w_pack=_pack_w_in(w_in[l]),
        w_conv=w_conv[l],
        alog_v=_lane_vec(a_log[l], ALPHA_OFF),
        dtb_v=_lane_vec(dt_bias[l], ALPHA_OFF),
        dn_out_norm=row(dn_out_norm[l]),
        q_nope_norm=row(q_nope_norm[l]),
        qr_gain=jnp.tile(row(q_rope_norm[l]), (1, LANES // QK_ROPE)),
        kv_norm=row(kv_norm[l]),
        kr_gain=_lane_vec(k_rope_norm[l], KR_OFF),
        k_nope_norm=row(k_nope_norm[l]),
        w_uk=w_uk[l].astype(BF16),
        w_uv=w_uv[l].astype(BF16),
        w_o_dn=w_o_dn[l].astype(BF16),
        w_o_mla=w_o_mla[l].astype(BF16),
        w_out=w_out[l].astype(BF16),
    )

    rows = B + Bs
    rows_pad = -(-rows // 8) * 8
    c_all = jnp.concatenate([c_prompt, c_sample, jnp.zeros((rows_pad - rows, D_MODEL), F32)], axis=0)
    mod = _ada(c_all, w_ada[l], b_ada[l].reshape(1, -1))

    zeros_conv = jnp.zeros((B, CONV_W - 1, DN_CONV_CH), F32)
    zeros_state = jnp.zeros((B, DN_HEADS, DN_DK, DN_DV), F32)
    yp, cvp, sdp, kvp, krp = _layer(x_prompt, mod[:B], zeros_conv, zeros_state, None, prm, q_off=0)
    ys, cvs, sds, kvs, krs = _layer(x_sample, mod[B:rows], state_conv[l], state_delta[l],
                                    (cache_ckv[l], cache_krope[l]), prm, q_off=past_len)
    st = lambda a: a[None]
    return (yp, ys, st(kvp), st(krp), st(sdp), st(cvp), st(kvs), st(krs), st(sds), st(cvs))
```

```python
import functools
import math

import jax
import jax.numpy as jnp
from jax import lax
from jax.experimental import pallas as pl
from jax.experimental.pallas import tpu as pltpu

D_MODEL = 1024
CHUNK = 64
EPS = 1e-6
DN_HEADS = 8
DN_DK = 128
DN_DV = 128
DN_QK = DN_HEADS * DN_DK
DN_V = DN_HEADS * DN_DV
DN_CONV_CH = 2 * DN_QK + DN_V
CONV_W = 4
MLA_HEADS = 8
QK_NOPE = 128
QK_ROPE = 64
QK_HEAD = QK_NOPE + QK_ROPE
V_HEAD = 128
KV_RANK = 512
MLA_Q = MLA_HEADS * QK_HEAD
MLA_V = MLA_HEADS * V_HEAD
ROPE_THETA = 10000.0

LANES = 128
QK_CAT = 256
CONV_ROWS = 128
ATTN_SUB = 256
ATTN_LOOKAHEAD = 2
ATTN_TILES_PER_TRIP = 4
KR_OFF = 0
BETA_OFF = QK_ROPE
ALPHA_OFF = QK_ROPE + DN_HEADS
VMEM_LIMIT = 56 * 1024 * 1024

F32 = jnp.float32
BF16 = jnp.bfloat16
HI = lax.Precision.HIGHEST


def _dot(a, b):
    return jnp.dot(a, b, preferred_element_type=F32)


def _dot_nt(a, b, precision=None):
    return lax.dot_general(a, b, (((1,), (1,)), ((), ())), preferred_element_type=F32, precision=precision)


def _dot_tn(a, b):
    return lax.dot_general(a, b, (((0,), (0,)), ((), ())), preferred_element_type=F32)


def _cparams(sem):
    return pltpu.CompilerParams(dimension_semantics=sem, vmem_limit_bytes=VMEM_LIMIT)


def _resident(shape):
    nd = len(shape)
    return pl.BlockSpec(shape, lambda *_: (0,) * nd, pipeline_mode=pl.Buffered(1))


def _ada_kernel(c_ref, w_ref, b_ref, o_ref):
    o_ref[...] = jnp.dot(c_ref[...], w_ref[...], preferred_element_type=F32, precision=HI) + b_ref[...]


def _ada(c_all, w_ada, b_ada):
    rows = c_all.shape[0]
    tn = 512
    return pl.pallas_call(
        _ada_kernel,
        grid=(3 * D_MODEL // tn,),
        in_specs=[pl.BlockSpec((rows, D_MODEL), lambda j: (0, 0)),
                  pl.BlockSpec((D_MODEL, tn), lambda j: (0, j)),
                  pl.BlockSpec((1, tn), lambda j: (0, j))],
        out_specs=pl.BlockSpec((rows, tn), lambda j: (0, j)),
        out_shape=jax.ShapeDtypeStruct((rows, 3 * D_MODEL), F32),
        compiler_params=_cparams(("arbitrary",)),
        name="ada",
    )(c_all, w_ada, b_ada)


_PROJ_WIDTHS = (LANES, KV_RANK, MLA_HEADS * QK_ROPE, MLA_HEADS * QK_NOPE, DN_CONV_CH, DN_V, MLA_V, D_MODEL, D_MODEL)


def _pack_w_in(w_in):
    o = 0
    qkv = w_in[:, o:o + DN_CONV_CH]; o += DN_CONV_CH
    z_a = w_in[:, o:o + DN_V]; o += DN_V
    beta = w_in[:, o:o + DN_HEADS]; o += DN_HEADS
    alpha = w_in[:, o:o + DN_HEADS]; o += DN_HEADS
    q = w_in[:, o:o + MLA_Q].reshape(D_MODEL, MLA_HEADS, QK_HEAD); o += MLA_Q
    ckv = w_in[:, o:o + KV_RANK]; o += KV_RANK
    kr = w_in[:, o:o + QK_ROPE]; o += QK_ROPE
    z_b = w_in[:, o:o + MLA_V]; o += MLA_V
    g_a = w_in[:, o:o + D_MODEL]; o += D_MODEL
    g_b = w_in[:, o:o + D_MODEL]
    qn = q[:, :, :QK_NOPE].reshape(D_MODEL, MLA_HEADS * QK_NOPE)
    qr = q[:, :, QK_NOPE:].reshape(D_MODEL, MLA_HEADS * QK_ROPE)
    pad = jnp.zeros((D_MODEL, LANES - QK_ROPE - 2 * DN_HEADS), w_in.dtype)
    small = jnp.concatenate([kr, beta, alpha, pad], axis=1)
    return tuple(w.astype(BF16) for w in (small, ckv, qr, qn, qkv, z_a, z_b, g_a, g_b))


def _in_proj_kernel(x_ref, mod_ref, gain_ref,
                    w_small, w_ckv, w_qr, w_qn, w_qkv, w_za, w_zb, w_ga, w_gb,
                    qng_ref, qrg_ref, kvg_ref, krg_ref, kng_ref, wuk_ref, wuv_ref,
                    small_ref, qkv_ref, za_ref, zb_ref, ga_ref, gb_ref,
                    qcat_ref, kcat_ref, v_ref, ckvn_ref, krn_ref, rope_ref, *, q_off):
    bb, tm, d = x_ref.shape
    rows = bb * tm

    @pl.when((pl.program_id(0) == 0) & (pl.program_id(1) == 0))
    def _():
        off = (lax.broadcasted_iota(jnp.int32, (rows, LANES), 0) % tm).astype(F32) * _rope_inv_freq((rows, LANES))
        rope_ref[0] = jnp.cos(off)
        rope_ref[1] = jnp.sin(off)

    x = x_ref[...]
    ms = jnp.mean(x * x, axis=-1, keepdims=True)
    y = x * lax.rsqrt(ms + EPS) * gain_ref[...]
    shift = mod_ref[:, :, 0:d]
    scale = mod_ref[:, :, d:2 * d]
    h = (y * (1.0 + scale) + shift).astype(BF16).reshape(rows, d)

    def project(w_ref, o_ref):
        o_ref[...] = _dot(h, w_ref[...]).astype(o_ref.dtype).reshape(o_ref.shape)

    sm = _dot(h, w_small[...])
    small_ref[...] = sm.reshape(bb, tm, LANES)
    ckv = _dot(h, w_ckv[...])
    qr_all = _dot(h, w_qr[...])
    qn_all = _dot(h, w_qn[...])
    project(w_qkv, qkv_ref)

    t0 = pl.program_id(1) * tm
    qscale = QK_HEAD ** -0.5 * math.log2(math.e)
    base = (t0 + q_off).astype(F32) * _rope_inv_freq((1, LANES))
    cos_a, sin_a = jnp.cos(base), jnp.sin(base)
    cos = cos_a * rope_ref[0] - sin_a * rope_ref[1]
    sin = sin_a * rope_ref[0] + cos_a * rope_ref[1]
    lane = lax.broadcasted_iota(jnp.int32, (rows, LANES), 1)
    low_half = lane < QK_ROPE
    first = (lane % QK_ROPE) < (QK_ROPE // 2)

    def rope(z):
        rot = jnp.where(first, -pltpu.roll(z, LANES - QK_ROPE // 2, 1), pltpu.roll(z, QK_ROPE // 2, 1))
        return z * cos + rot * sin

    def rms64(z):
        zz = z * z
        s_lo = jnp.sum(jnp.where(low_half, zz, 0.0), axis=-1, keepdims=True)
        s_hi = jnp.sum(jnp.where(low_half, 0.0, zz), axis=-1, keepdims=True)
        return lax.rsqrt(jnp.where(low_half, s_lo, s_hi) * (1.0 / QK_ROPE) + EPS)

    ckvn = ckv * lax.rsqrt(jnp.mean(ckv * ckv, axis=-1, keepdims=True) + EPS) * kvg_ref[...]
    ckvn_ref[...] = ckvn.reshape(bb, tm, KV_RANK)
    ckvn16 = ckvn.astype(BF16)
    kr = rope(sm * rms64(sm) * krg_ref[...])
    krn_ref[...] = kr[:, :QK_ROPE].reshape(bb, tm, QK_ROPE)
    kr_pad16 = jnp.where(low_half, kr, 0.0).astype(BF16).reshape(bb, tm, LANES)

    project(w_za, za_ref)
    k_raw = _dot(ckvn16, wuk_ref[...])
    v_ref[...] = _dot(ckvn16, wuv_ref[...]).astype(BF16).reshape(bb, tm, MLA_V)
    project(w_zb, zb_ref)

    qn = _head_rms(qn_all, qng_ref[...] * qscale, QK_NOPE)
    for c in range(MLA_HEADS // 2):
        z = qr_all[:, c * LANES:(c + 1) * LANES]
        z = rope(z * rms64(z) * qrg_ref[...]) * qscale
        even = jnp.where(low_half, z, 0.0)
        odd = jnp.where(low_half, pltpu.roll(z, QK_ROPE, 1), 0.0)
        for hh, part in ((2 * c, even), (2 * c + 1, odd)):
            qcat_ref[:, :, hh * QK_CAT:hh * QK_CAT + QK_NOPE] = qn[hh].astype(BF16).reshape(bb, tm, QK_NOPE)
            qcat_ref[:, :, hh * QK_CAT + QK_NOPE:(hh + 1) * QK_CAT] = part.astype(BF16).reshape(bb, tm, LANES)
    project(w_ga, ga_ref)

    kn = _head_rms(k_raw, kng_ref[...], QK_NOPE)
    for hh in range(MLA_HEADS):
        kcat_ref[:, :, hh * QK_CAT:hh * QK_CAT + QK_NOPE] = kn[hh].astype(BF16).reshape(bb, tm, QK_NOPE)
        kcat_ref[:, :, hh * QK_CAT + QK_NOPE:(hh + 1) * QK_CAT] = kr_pad16
    project(w_gb, gb_ref)


def _in_proj(x, mod3, gain, w_pack, mla, bb, tm, q_off):
    B, T, _ = x.shape
    row = lambda b, t: (b, t, 0)
    outs = ((LANES, F32), (DN_CONV_CH, BF16), (DN_V, BF16), (MLA_V, BF16), (D_MODEL, BF16), (D_MODEL, BF16),
            (MLA_HEADS * QK_CAT, BF16), (MLA_HEADS * QK_CAT, BF16), (MLA_V, BF16), (KV_RANK, F32), (QK_ROPE, F32))
    return pl.pallas_call(
        functools.partial(_in_proj_kernel, q_off=q_off),
        grid=(B // bb, T // tm),
        in_specs=[pl.BlockSpec((bb, tm, D_MODEL), row),
                  pl.BlockSpec((bb, 1, 3 * D_MODEL), lambda b, t: (b, 0, 0)),
                  _resident((1, D_MODEL))]
        + [_resident((D_MODEL, w)) for w in _PROJ_WIDTHS]
        + [_resident((1, QK_NOPE)), _resident((1, LANES)), _resident((1, KV_RANK)), _resident((1, LANES)),
           _resident((1, QK_NOPE)), _resident((KV_RANK, MLA_HEADS * QK_NOPE)), _resident((KV_RANK, MLA_V))],
        out_specs=[pl.BlockSpec((bb, tm, w), row) for w, _ in outs],
        out_shape=[jax.ShapeDtypeStruct((B, T, w), dt) for w, dt in outs],
        scratch_shapes=[pltpu.VMEM((2, bb * tm, LANES), F32)],
        compiler_params=_cparams(("arbitrary", "arbitrary")),
        name="in_proj",
    )(x, mod3, gain, *w_pack, *mla)


def _softplus(x):
    return jnp.maximum(x, 0.0) + jnp.log(1.0 + jnp.exp(-jnp.abs(x)))


def _dn_prep_kernel(qkv_ref, prev_ref, cs_ref, small_ref, wconv_ref, alog_ref, dtb_ref,
                    w_ref, u_ref, qe_ref, kd_ref, attn_ref, egl_ref, l_scr, rhs_scr, *, C, tiles_per_seq):
    step = pl.program_id(0)
    tile_idx = jnp.minimum(step, pl.num_programs(0) - 2) % tiles_per_seq
    tm = qkv_ref.shape[1]
    nc = tm // C
    pad = prev_ref.shape[1]

    wr = step % 2
    rd = 1 - wr

    @pl.when(step == 0)
    def _():
        l_scr[...] = jnp.zeros_like(l_scr)
        rhs_scr[...] = jnp.zeros_like(rhs_scr)

    x16 = qkv_ref[0]
    hist = jnp.where(tile_idx == 0, cs_ref[0], prev_ref[0].astype(F32))
    hist_hi = hist.astype(BF16)
    rem = hist - hist_hi.astype(F32)
    hist_mid = rem.astype(BF16)
    hist_lo = (rem - hist_mid.astype(F32)).astype(BF16)
    n_sh = CONV_W - 1
    rs = min(tm, CONV_ROWS)

    def shifted_taps(pieces, xs):
        npc = len(pieces)
        full16 = jnp.concatenate(list(pieces) + [xs], axis=0)
        srow = lax.broadcasted_iota(jnp.int32, (n_sh * rs, npc * pad + rs), 0)
        scol = lax.broadcasted_iota(jnp.int32, (n_sh * rs, npc * pad + rs), 1)
        src = srow % rs + srow // rs + pad - n_sh
        sel = scol == src + (npc - 1) * pad
        for p in range(npc - 1):
            sel = sel | ((scol == src + p * pad) & (scol < (p + 1) * pad))
        return _dot(sel.astype(BF16), full16)

    conv_parts = []
    for j in range(tm // rs):
        xs = x16[j * rs:(j + 1) * rs]
        pieces = (hist_hi, hist_mid, hist_lo) if j == 0 else (x16[j * rs - pad:j * rs],)
        shifted = shifted_taps(pieces, xs)
        part = xs.astype(F32) * wconv_ref[n_sh:CONV_W, :]
        for i in range(n_sh):
            part = part + shifted[i * rs:(i + 1) * rs, :] * wconv_ref[i:i + 1, :]
        conv_parts.append(part)
    conv = conv_parts[0] if len(conv_parts) == 1 else jnp.concatenate(conv_parts, axis=0)
    act = conv * jax.nn.sigmoid(conv)

    sm = small_ref[0]
    beta_all = jax.nn.sigmoid(sm)
    g_all = -jnp.exp(alog_ref[...]) * _softplus(sm + dtb_ref[...])
    rt = lax.broadcasted_iota(jnp.int32, (tm, tm), 0)
    ct = lax.broadcasted_iota(jnp.int32, (tm, tm), 1)
    chunk_tri = ((rt // C == ct // C) & (rt >= ct)).astype(F32)
    gcum = jnp.dot(chunk_tri, g_all, preferred_element_type=F32, precision=HI)
    sel = (lax.broadcasted_iota(jnp.int32, (DN_HEADS, LANES), 1)
           == lax.broadcasted_iota(jnp.int32, (DN_HEADS, LANES), 0) + ALPHA_OFF).astype(F32)
    gcum_t = _dot_nt(sel, gcum, precision=HI)

    def per_head_lanes(x, off):
        hi = x.astype(BF16)
        r1 = x - hi.astype(F32)
        mid = r1.astype(BF16)
        lo = (r1 - mid.astype(F32)).astype(BF16)
        erow = lax.broadcasted_iota(jnp.int32, (3 * LANES, DN_QK), 0) % LANES
        ecol = lax.broadcasted_iota(jnp.int32, (3 * LANES, DN_QK), 1) // DN_DK
        return _dot(jnp.concatenate([hi, mid, lo], axis=1), (erow == ecol + off).astype(BF16))

    g_b = per_head_lanes(gcum, ALPHA_OFF)
    beta_b = per_head_lanes(beta_all, BETA_OFF)
    glast_b = jnp.concatenate(
        [jnp.broadcast_to(g_b[c * C + C - 1:(c + 1) * C, :], (C, DN_QK)) for c in range(nc)], axis=0)
    eg_b = jnp.exp(g_b)
    kdf_b = jnp.exp(glast_b - g_b)
    for c in range(nc):
        egl_ref[0, c] = jnp.exp(g_b[c * C + C - 1:(c + 1) * C, :])

    ri = lax.broadcasted_iota(jnp.int32, (C, C), 0)
    ci = lax.broadcasted_iota(jnp.int32, (C, C), 1)
    tri_incl = ri >= ci
    tri_strict = ri > ci
    eye = (ri == ci).astype(F32)
    pair_masks = []
    m = 1
    while m < C:
        pair_masks.append((ri // (2 * m) == ci // (2 * m)) & (ri // m != ci // m))
        m *= 2

    heads = range(DN_HEADS)
    hl = lambda h: slice(h * DN_DK, (h + 1) * DN_DK)
    items = [(c, h) for c in range(nc) for h in heads]
    rows = lambda c: slice(c * C, (c + 1) * C)

    l_prev = [l_scr[rd, i] for i in range(len(items))]
    pinv = [eye - jnp.where(pair_masks[0], l, 0.0) for l in l_prev]
    for mask in pair_masks[1:]:
        p16 = [p.astype(BF16) for p in pinv]
        tmp = [_dot(p16[i], jnp.where(mask, l_prev[i], 0.0).astype(BF16)).astype(BF16) for i in range(len(items))]
        pinv = [pinv[i] - _dot(tmp[i], p16[i]) for i in range(len(items))]
    for i, (c, h) in enumerate(items):
        wu = _dot(pinv[i].astype(BF16), rhs_scr[rd, i])
        w_ref[0, c, :, hl(h)] = wu[:, :DN_DK].astype(BF16)
        u_ref[0, rows(c), hl(h)] = wu[:, DN_DK:]

    qn, kn = [], []
    for h in heads:
        qh = act[:, h * DN_DK:(h + 1) * DN_DK]
        kh = act[:, DN_QK + h * DN_DK:DN_QK + (h + 1) * DN_DK]
        qn.append(qh * lax.rsqrt(jnp.sum(qh * qh, axis=-1, keepdims=True) + EPS) * (DN_DK ** -0.5))
        kn.append(kh * lax.rsqrt(jnp.sum(kh * kh, axis=-1, keepdims=True) + EPS))

    k16 = [kn[h].astype(BF16) for h in heads]
    q16 = [qn[h].astype(BF16) for h in heads]
    kb = [kn[h] * beta_b[:, hl(h)] for h in heads]
    kb16 = [kb[h].astype(BF16) for h in heads]
    kbe16 = [(kb[h] * eg_b[:, hl(h)]).astype(BF16) for h in heads]
    vb16 = [(act[:, 2 * DN_QK + h * DN_DV:2 * DN_QK + (h + 1) * DN_DV] * beta_b[:, hl(h)]).astype(BF16)
            for h in heads]
    qe16 = [(qn[h] * eg_b[:, hl(h)]).astype(BF16) for h in heads]
    for h in heads:
        kd_ref[0, :, hl(h)] = (kn[h] * kdf_b[:, hl(h)]).astype(BF16)

    attn_ref[...] = jnp.zeros_like(attn_ref)
    decay, qk = [], []
    for i, (c, h) in enumerate(items):
        r = rows(c)
        gc = g_b[r, h * DN_DK:h * DN_DK + C]
        decay.append(jnp.exp(jnp.where(tri_incl, gc - gcum_t[h:h + 1, r], -1e30)))
        rhs_scr[wr, i] = jnp.concatenate([kbe16[h][r], vb16[h][r]], axis=1)
        qe_ref[0, c, :, hl(h)] = qe16[h][r]
        qk.append(_dot_nt(jnp.concatenate([kb16[h][r], q16[h][r]], axis=0), k16[h][r]))

    for i, (c, h) in enumerate(items):
        l_scr[wr, i] = jnp.where(tri_strict, qk[i][:C] * decay[i], 0.0)
        attn_ref[0, rows(c), h * DN_DK:h * DN_DK + C] = (qk[i][C:] * decay[i]).astype(BF16)


def _dn_scan_kernel(w_ref, qe_ref, u_ref, kd_ref, attn_ref, egl_ref, za_ref, s0_ref, onorm_ref,
                    ua_ref, sfin_ref, s_ref, *, C):
    n = pl.program_id(1)
    bg, G = w_ref.shape[0], w_ref.shape[1]

    @pl.when(n == 0)
    def _():
        s_ref[...] = s0_ref[...]

    chains = [(b, h) for b in range(bg) for h in range(DN_HEADS)]
    for g in range(G):
        r = slice(g * C, (g + 1) * C)
        s_old = [s_ref[b, h] for b, h in chains]
        s16 = [s.astype(BF16) for s in s_old]
        ws = [_dot(jnp.concatenate([w_ref[b, g, :, h * DN_DK:(h + 1) * DN_DK],
                                    qe_ref[b, g, :, h * DN_DK:(h + 1) * DN_DK]], axis=0), s16[i])
              for i, (b, h) in enumerate(chains)]
        v16 = [(u_ref[b, r, h * DN_DV:(h + 1) * DN_DV] - ws[i][:C]).astype(BF16)
               for i, (b, h) in enumerate(chains)]
        for i, (b, h) in enumerate(chains):
            lo = h * DN_DK
            s_ref[b, h] = s_old[i] * egl_ref[b, g, :, lo:lo + DN_DK] + _dot_tn(kd_ref[b, r, lo:lo + DN_DK], v16[i])
        for i, (b, h) in enumerate(chains):
            lo = h * DN_DV
            o = ws[i][C:] + _dot(attn_ref[b, r, lo:lo + C], v16[i])
            o = o * lax.rsqrt(jnp.mean(o * o, axis=-1, keepdims=True) + EPS) * onorm_ref[...]
            z = za_ref[b, r, lo:lo + DN_DV].astype(F32)
            ua_ref[b, r, lo:lo + DN_DV] = (o * (z * jax.nn.sigmoid(z))).astype(BF16)

    @pl.when(n == pl.num_programs(1) - 1)
    def _():
        sfin_ref[...] = s_ref[...]


def _deltanet(qkv, small, z_a, conv_state, s0, w_conv, alog_v, dtb_v, onorm, tm, bg, G):
    B, T, _ = qkv.shape
    C = min(CHUNK, T)
    N = T // C
    nc = tm // C
    hist_rows = 16
    cs = jnp.pad(conv_state, ((0, 0), (hist_rows - (CONV_W - 1), 0), (0, 0)))
    nt = T // tm
    n_tiles = B * nt

    def cur(s):
        s = jnp.minimum(s, n_tiles - 1)
        return s // nt, s % nt

    def done(s):
        s = jnp.maximum(s - 1, 0)
        return s // nt, s % nt

    tile = lambda s: (*cur(s), 0)
    tile4 = lambda s: (*cur(s), 0, 0)
    prev = lambda s: (cur(s)[0], jnp.maximum(cur(s)[1] * (tm // hist_rows) - 1, 0), 0)
    w, u, qe, kd, attn, egl = pl.pallas_call(
        functools.partial(_dn_prep_kernel, C=C, tiles_per_seq=nt),
        grid=(n_tiles + 1,),
        in_specs=[pl.BlockSpec((1, tm, DN_CONV_CH), tile),
                  pl.BlockSpec((1, hist_rows, DN_CONV_CH), prev),
                  pl.BlockSpec((1, hist_rows, DN_CONV_CH), lambda s: (cur(s)[0], 0, 0)),
                  pl.BlockSpec((1, tm, LANES), tile),
                  _resident((CONV_W, DN_CONV_CH)),
                  _resident((1, LANES)),
                  _resident((1, LANES))],
        out_specs=[pl.BlockSpec((1, nc, C, DN_QK), lambda s: (*done(s), 0, 0)),
                   pl.BlockSpec((1, tm, DN_V), lambda s: (*done(s), 0)),
                   pl.BlockSpec((1, nc, C, DN_QK), tile4),
                   pl.BlockSpec((1, tm, DN_QK), tile),
                   pl.BlockSpec((1, tm, DN_V), tile),
                   pl.BlockSpec((1, nc, 1, DN_QK), tile4)],
        out_shape=[jax.ShapeDtypeStruct((B, N, C, DN_QK), BF16),
                   jax.ShapeDtypeStruct((B, T, DN_V), F32),
                   jax.ShapeDtypeStruct((B, N, C, DN_QK), BF16),
                   jax.ShapeDtypeStruct((B, T, DN_QK), BF16),
                   jax.ShapeDtypeStruct((B, T, DN_V), BF16),
                   jax.ShapeDtypeStruct((B, N, 1, DN_QK), F32)],
        scratch_shapes=[pltpu.VMEM((2, nc * DN_HEADS, C, C), F32),
                        pltpu.VMEM((2, nc * DN_HEADS, C, DN_DK + DN_DV), BF16)],
        compiler_params=_cparams(("arbitrary",)),
        name="dn_prep",
    )(qkv, qkv, cs, small, w_conv, alog_v, dtb_v)

    grp = lambda b, n: (b, n, 0)
    grp4 = lambda b, n: (b, n, 0, 0)
    state = pl.BlockSpec((bg, DN_HEADS, DN_DK, DN_DV), lambda b, n: (b, 0, 0, 0))
    u_a, s_new = pl.pallas_call(
        functools.partial(_dn_scan_kernel, C=C),
        grid=(B // bg, N // G),
        in_specs=[pl.BlockSpec((bg, G, C, DN_QK), grp4),
                  pl.BlockSpec((bg, G, C, DN_QK), grp4),
                  pl.BlockSpec((bg, G * C, DN_V), grp),
                  pl.BlockSpec((bg, G * C, DN_QK), grp),
                  pl.BlockSpec((bg, G * C, DN_V), grp),
                  pl.BlockSpec((bg, G, 1, DN_QK), grp4),
                  pl.BlockSpec((bg, G * C, DN_V), grp),
                  state,
                  _resident((1, DN_DV))],
        out_specs=[pl.BlockSpec((bg, G * C, DN_V), grp), state],
        out_shape=[jax.ShapeDtypeStruct((B, T, DN_V), BF16),
                   jax.ShapeDtypeStruct((B, DN_HEADS, DN_DK, DN_DV), F32)],
        scratch_shapes=[pltpu.VMEM((bg, DN_HEADS, DN_DK, DN_DV), F32)],
        compiler_params=_cparams(("arbitrary", "arbitrary")),
        name="dn_scan",
    )(w, qe, u, kd, attn, egl, z_a, s0, onorm)
    conv_new = qkv[:, T - (CONV_W - 1):, :].astype(F32)
    return u_a, s_new, conv_new


def _head_rms(x, gain_row, width):
    outs = []
    for h in range(x.shape[1] // width):
        xh = x[:, h * width:(h + 1) * width]
        outs.append(xh * lax.rsqrt(jnp.mean(xh * xh, axis=-1, keepdims=True) + EPS) * gain_row)
    return outs


def _rope_inv_freq(shape):
    half = QK_ROPE // 2
    fidx = (lax.broadcasted_iota(jnp.int32, shape, 1) % half).astype(F32)
    return jnp.exp(fidx * (-math.log(ROPE_THETA) / half))


def _chunk_mask(qpos0, kpos0, tq, tk):
    qc = (lax.broadcasted_iota(jnp.int32, (tq, tk), 0) + qpos0) // CHUNK
    kc = (lax.broadcasted_iota(jnp.int32, (tq, tk), 1) + kpos0) // CHUNK
    return kc <= qc


def _attn_prompt_kernel(q_ref, k_ref, v_ref, o_ref, m_ref, l_ref, acc_ref, *, tq, tk, sub):
    i = pl.program_id(2)
    nsub = tq // sub
    ratio = tq // tk
    m_ref[...] = jnp.full_like(m_ref, -1e30)
    l_ref[...] = jnp.zeros_like(l_ref)
    acc_ref[...] = jnp.zeros_like(acc_ref)

    def scores(r, k):
        return _dot_nt(q_ref[0, r * sub:(r + 1) * sub, :], k)

    def softmax_pv(r, sr, v, mask):
        rows = slice(r * sub, (r + 1) * sub)
        if mask is not None:
            sr = jnp.where(mask, sr, -1e30)
        m_old = m_ref[rows, :]
        m_new = jnp.maximum(m_old, jnp.max(sr, axis=-1, keepdims=True))
        alpha = jnp.exp2(m_old - m_new)
        p = jnp.exp2(sr - jnp.tile(m_new, (1, tk // LANES)))
        psum = p[:, 0:LANES]
        for c in range(1, tk // LANES):
            psum = psum + p[:, c * LANES:(c + 1) * LANES]
        l_ref[rows, :] = alpha * l_ref[rows, :] + psum
        acc_ref[rows, :] = alpha * acc_ref[rows, :] + _dot(p.astype(BF16), v)
        m_ref[rows, :] = m_new

    def run(j0, items):
        kv = {}
        for d in sorted({d for d, _, _ in items}):
            start = pl.multiple_of((j0 + d) * tk, tk)
            kv[d] = (k_ref[0, pl.ds(start, tk), :], v_ref[0, pl.ds(start, tk), :])
        s = {n: scores(items[n][1], kv[items[n][0]][0]) for n in range(min(ATTN_LOOKAHEAD, len(items)))}
        for n, (d, r, mask) in enumerate(items):
            ahead = n + ATTN_LOOKAHEAD
            if ahead < len(items):
                s[ahead] = scores(items[ahead][1], kv[items[ahead][0]][0])
            softmax_pv(r, s.pop(n), kv[d][1], mask)

    per_trip = math.gcd(ratio, ATTN_TILES_PER_TRIP)
    full = [(d, r, None) for d in range(per_trip) for r in range(nsub)]

    def body(jj, carry):
        run(jj * per_trip, full)
        return carry

    lax.fori_loop(0, i * (ratio // per_trip), body, 0)
    diag = []
    for d in range(ratio):
        for r in range(nsub):
            q_lo, q_hi = (r * sub) // CHUNK, (r * sub + sub - 1) // CHUNK
            k_lo, k_hi = (d * tk) // CHUNK, (d * tk + tk - 1) // CHUNK
            if k_lo > q_hi:
                continue
            diag.append((d, r, None if k_hi <= q_lo else _chunk_mask(r * sub, d * tk, sub, tk)))
    run(i * ratio, diag)
    l = jnp.sum(l_ref[...], axis=-1, keepdims=True)
    o_ref[0] = (acc_ref[...] / l).astype(BF16)


def _attn_prompt(qcat, kcat, v, tq, tk):
    B, T, _ = v.shape
    return pl.pallas_call(
        functools.partial(_attn_prompt_kernel, tq=tq, tk=tk, sub=ATTN_SUB),
        grid=(B, MLA_HEADS, T // tq),
        in_specs=[pl.BlockSpec((1, tq, QK_CAT), lambda b, h, i: (b, i, h)),
                  pl.BlockSpec((1, T, QK_CAT), lambda b, h, i: (b, 0, h)),
                  pl.BlockSpec((1, T, V_HEAD), lambda b, h, i: (b, 0, h))],
        out_specs=pl.BlockSpec((1, tq, V_HEAD), lambda b, h, i: (b, i, h)),
        out_shape=jax.ShapeDtypeStruct((B, T, MLA_V), BF16),
        scratch_shapes=[pltpu.VMEM((tq, LANES), F32), pltpu.VMEM((tq, LANES), F32),
                        pltpu.VMEM((tq, V_HEAD), F32)],
        compiler_params=_cparams(("arbitrary", "arbitrary", "arbitrary")),
        name="attn_prompt",
    )(qcat, kcat, v)


def _attn_sample_kernel(q_ref, kn_ref, vn_ref, ckv_ref, kr_ref, kng_ref, wuk_ref, wuv_ref, o_ref,
                        m_ref, l_ref, acc_ref, *, tk):
    T = q_ref.shape[1]
    P = ckv_ref.shape[1]
    R = MLA_HEADS * T
    q = q_ref[0]
    row_head = lax.broadcasted_iota(jnp.int32, (R, 1), 0) // T
    q_chunk = (lax.broadcasted_iota(jnp.int32, (R, 1), 0) % T + P) // CHUNK

    def stacked(x, width):
        lane_head = lax.broadcasted_iota(jnp.int32, (R, x.shape[1]), 1) // width
        return jnp.where(lane_head == row_head, jnp.concatenate([x] * MLA_HEADS, axis=0), jnp.zeros((), x.dtype))

    q_nope = stacked(jnp.concatenate([q[:, h * QK_CAT:h * QK_CAT + QK_NOPE] for h in range(MLA_HEADS)], axis=1),
                     QK_NOPE)
    q_rope = jnp.concatenate([q[:, h * QK_CAT + QK_NOPE:h * QK_CAT + QK_HEAD] for h in range(MLA_HEADS)],
                             axis=0)

    def update(s, k_chunk):
        s = jnp.where(k_chunk <= q_chunk, s, -1e30)
        m_old = m_ref[...]
        m_new = jnp.maximum(m_old, jnp.max(s, axis=-1, keepdims=True))
        alpha = jnp.exp2(m_old - m_new)
        p = jnp.exp2(s - m_new)
        l_ref[...] = alpha * l_ref[...] + jnp.sum(p, axis=-1, keepdims=True)
        m_ref[...] = m_new
        return alpha, p.astype(BF16)

    m_ref[...] = jnp.full_like(m_ref, -1e30)
    l_ref[...] = jnp.zeros_like(l_ref)
    acc_ref[...] = jnp.zeros_like(acc_ref)

    def body(c, carry):
        start = pl.multiple_of(c * tk, tk)
        ckv16 = ckv_ref[0, pl.ds(start, tk), :].astype(BF16)
        kn = _head_rms(_dot(ckv16, wuk_ref[...]), kng_ref[...], QK_NOPE)
        kn16 = jnp.concatenate([x.astype(BF16) for x in kn], axis=1)
        s = _dot_nt(q_nope, kn16) + _dot_nt(q_rope, kr_ref[0, pl.ds(start, tk), :].astype(BF16))
        k_chunk = (lax.broadcasted_iota(jnp.int32, (1, tk), 1) + start) // CHUNK
        alpha, p16 = update(s, k_chunk)
        acc_ref[...] = alpha * acc_ref[...] + _dot(p16, ckv16)
        return carry

    lax.fori_loop(0, P // tk, body, 0)

    s_new = _dot_nt(stacked(q, QK_CAT), kn_ref[0])
    alpha, p16 = update(s_new, (lax.broadcasted_iota(jnp.int32, (1, T), 1) + P) // CHUNK)
    pc16 = (alpha * acc_ref[...]).astype(BF16)
    inv_l = 1.0 / l_ref[...]
    for h in range(MLA_HEADS):
        rows = slice(h * T, (h + 1) * T)
        lanes = slice(h * V_HEAD, (h + 1) * V_HEAD)
        o = _dot(pc16[rows], wuv_ref[:, lanes]) + _dot(p16[rows], vn_ref[0, :, lanes])
        o_ref[0, :, lanes] = (o * inv_l[rows]).astype(BF16)


def _attn_sample(qcat, kcat_new, v_new, past_ckv, past_kr, kn_gain, w_uk16, w_uv16, tk):
    B, T, _ = v_new.shape
    P = past_ckv.shape[1]
    R = MLA_HEADS * T
    perb = lambda b: (b, 0, 0)
    return pl.pallas_call(
        functools.partial(_attn_sample_kernel, tk=tk),
        grid=(B,),
        in_specs=[pl.BlockSpec((1, T, MLA_HEADS * QK_CAT), perb),
                  pl.BlockSpec((1, T, MLA_HEADS * QK_CAT), perb),
                  pl.BlockSpec((1, T, MLA_V), perb),
                  pl.BlockSpec((1, P, KV_RANK), perb),
                  pl.BlockSpec((1, P, QK_ROPE), perb),
                  _resident((1, QK_NOPE)),
                  _resident((KV_RANK, MLA_HEADS * QK_NOPE)),
                  _resident((KV_RANK, MLA_V))],
        out_specs=pl.BlockSpec((1, T, MLA_V), perb),
        out_shape=jax.ShapeDtypeStruct((B, T, MLA_V), BF16),
        scratch_shapes=[pltpu.VMEM((R, 1), F32), pltpu.VMEM((R, 1), F32), pltpu.VMEM((R, KV_RANK), F32)],
        compiler_params=_cparams(("arbitrary",)),
        name="attn_sample",
    )(qcat, kcat_new, v_new, past_ckv, past_kr, kn_gain, w_uk16, w_uv16)


def _out_kernel(x_ref, mod_ref, ua_ref, ob_ref, zb_ref, ga_ref, gb_ref, wdn_ref, wmla_ref, wout_ref, y_ref):
    bb, tm, d = x_ref.shape
    rows = bb * tm
    zb = zb_ref[...].astype(F32)
    ub = (ob_ref[...].astype(F32) * (zb * jax.nn.sigmoid(zb))).astype(BF16).reshape(rows, d)
    ya = _dot(ua_ref[...].reshape(rows, d), wdn_ref[...])
    yb = _dot(ub, wmla_ref[...])
    ga = jax.nn.sigmoid(ga_ref[...].astype(F32)).reshape(rows, d)
    gb = jax.nn.sigmoid(gb_ref[...].astype(F32)).reshape(rows, d)
    merged = (ga * ya + gb * yb).astype(BF16)
    out = _dot(merged, wout_ref[...]).reshape(bb, tm, d)
    gate = mod_ref[:, :, 2 * d:3 * d]
    y_ref[...] = x_ref[...] + gate * out


def _out_proj(x, mod3, u_a, o_b, z_b, g_a, g_b, w_dn16, w_mla16, w_out16, bb, tm):
    B, T, _ = x.shape
    row = lambda b, t: (b, t, 0)
    act = pl.BlockSpec((bb, tm, D_MODEL), row)
    return pl.pallas_call(
        _out_kernel,
        grid=(B // bb, T // tm),
        in_specs=[act, pl.BlockSpec((bb, 1, 3 * D_MODEL), lambda b, t: (b, 0, 0)), act, act, act, act, act,
                  _resident((D_MODEL, D_MODEL)), _resident((D_MODEL, D_MODEL)), _resident((D_MODEL, D_MODEL))],
        out_specs=act,
        out_shape=jax.ShapeDtypeStruct((B, T, D_MODEL), F32),
        compiler_params=_cparams(("arbitrary", "arbitrary")),
        name="out_proj",
    )(x, mod3, u_a, o_b, z_b, g_a, g_b, w_dn16, w_mla16, w_out16)


def _lane_vec(v, off):
    return jnp.zeros((1, LANES), F32).at[0, off:off + v.shape[0]].set(v)


def _tiles(B, T, cached):
    if cached:
        whole = (B, T)
        return dict(proj=whole, out=whole, dn=dict(tm=T, bg=2, G=1), attn_tk=512)
    return dict(proj=(1, 256), out=(1, 512), dn=dict(tm=256, bg=B, G=4), attn=(2048, 512))


def _layer(x, mod, conv_state, s0, past, prm, q_off):
    B, T, _ = x.shape
    tiles = _tiles(B, T, past is not None)
    mod3 = mod.reshape(B, 1, 3 * D_MODEL)
    mla = (prm["q_nope_norm"], prm["qr_gain"], prm["kv_norm"], prm["kr_gain"], prm["k_nope_norm"],
           prm["w_uk"], prm["w_uv"])
    small, qkv, z_a, z_b, g_a, g_b, qcat, kcat, v, ckv_new, kr_new = _in_proj(
        x, mod3, prm["norm_gain"], prm["w_pack"], mla, *tiles["proj"], q_off)
    u_a, s_new, conv_new = _deltanet(qkv, small, z_a, conv_state, s0, prm["w_conv"], prm["alog_v"],
                                     prm["dtb_v"], prm["dn_out_norm"], **tiles["dn"])
    if past is None:
        o_b = _attn_prompt(qcat, kcat, v, *tiles["attn"])
    else:
        past_ckv, past_kr = past
        o_b = _attn_sample(qcat, kcat, v, past_ckv, past_kr, prm["k_nope_norm"], prm["w_uk"], prm["w_uv"],
                           tiles["attn_tk"])
    y = _out_proj(x, mod3, u_a, o_b, z_b, g_a, g_b, prm["w_o_dn"], prm["w_o_mla"], prm["w_out"], *tiles["out"])
    return y, conv_new, s_new, ckv_new, kr_new


def kernel(x_prompt, x_sample, c_prompt, c_sample, cache_ckv, cache_krope, state_delta, state_conv, norm_gain, w_ada, b_ada, w_in, w_conv, a_log, dt_bias, dn_out_norm, q_nope_norm, q_rope_norm, k_nope_norm, k_rope_norm, kv_norm, w_uk, w_uv, w_o_dn, w_o_mla, w_out):
    depth = w_in.shape[0]
    assert depth == 1, "single-layer configuration"
    l = 0
    B, T, _ = x_prompt.shape
    Bs, Ts, _ = x_sample.shape
    past_len = cache_ckv.shape[2]

    row = lambda v: v.reshape(1, -1).astype(F32)
    prm = dict(
        norm_gain=row(norm_gain[l]),
        w_pack=_pack_w_in(w_in[l]),
        w_conv=w_conv[l],
        alog_v=_lane_vec(a_log[l], ALPHA_OFF),
        dtb_v=_lane_vec(dt_bias[l], ALPHA_OFF),
        dn_out_norm=row(dn_out_norm[l]),
        q_nope_norm=row(q_nope_norm[l]),
        qr_gain=jnp.tile(row(q_rope_norm[l]), (1, LANES // QK_ROPE)),
        kv_norm=row(kv_norm[l]),
        kr_gain=_lane_vec(k_rope_norm[l], KR_OFF),
        k_nope_norm=row(k_nope_norm[l]),
        w_uk=w_uk[l].astype(BF16),
        w_uv=w_uv[l].astype(BF16),
        w_o_dn=w_o_dn[l].astype(BF16),
        w_o_mla=w_o_mla[l].astype(BF16),
        w_out=w_out[l].astype(BF16),
    )

    rows = B + Bs
    rows_pad = -(-rows // 8) * 8
    c_all = jnp.concatenate([c_prompt, c_sample, jnp.zeros((rows_pad - rows, D_MODEL), F32)], axis=0)
    mod = _ada(c_all, w_ada[l], b_ada[l].reshape(1, -1))

    zeros_conv = jnp.zeros((B, CONV_W - 1, DN_CONV_CH), F32)
    zeros_state = jnp.zeros((B, DN_HEADS, DN_DK, DN_DV), F32)
    yp, cvp, sdp, kvp, krp = _layer(x_prompt, mod[:B], zeros_conv, zeros_state, None, prm, q_off=0)
    ys, cvs, sds, kvs, krs = _layer(x_sample, mod[B:rows], state_conv[l], state_delta[l],
                                    (cache_ckv[l], cache_krope[l]), prm, q_off=past_len)
    st = lambda a: a[None]
    return (yp, ys, st(kvp), st(krp), st(sdp), st(cvp), st(kvs), st(krs), st(sds), st(cvs))
```

```python
import functools
import math

import jax
import jax.numpy as jnp
from jax import lax
from jax.experimental import pallas as pl
from jax.experimental.pallas import tpu as pltpu

D_MODEL = 1024
CHUNK = 64
EPS = 1e-6
DN_HEADS = 8
DN_DK = 128
DN_DV = 128
DN_QK = DN_HEADS * DN_DK
DN_V = DN_HEADS * DN_DV
DN_CONV_CH = 2 * DN_QK + DN_V
CONV_W = 4
MLA_HEADS = 8
QK_NOPE = 128
QK_ROPE = 64
QK_HEAD = QK_NOPE + QK_ROPE
V_HEAD = 128
KV_RANK = 512
MLA_Q = MLA_HEADS * QK_HEAD
MLA_V = MLA_HEADS * V_HEAD
ROPE_THETA = 10000.0

LANES = 128
QK_CAT = 256
CONV_ROWS = 128
ATTN_SUB = 256
ATTN_LOOKAHEAD = 2
ATTN_TILES_PER_TRIP = 4
KR_OFF = 0
BETA_OFF = QK_ROPE
ALPHA_OFF = QK_ROPE + DN_HEADS
VMEM_LIMIT = 56 * 1024 * 1024

F32 = jnp.float32
BF16 = jnp.bfloat16
HI = lax.Precision.HIGHEST


def _dot(a, b):
    return jnp.dot(a, b, preferred_element_type=F32)


def _dot_nt(a, b, precision=None):
    return lax.dot_general(a, b, (((1,), (1,)), ((), ())), preferred_element_type=F32, precision=precision)


def _dot_tn(a, b):
    return lax.dot_general(a, b, (((0,), (0,)), ((), ())), preferred_element_type=F32)


def _cparams(sem):
    return pltpu.CompilerParams(dimension_semantics=sem, vmem_limit_bytes=VMEM_LIMIT)


def _resident(shape):
    nd = len(shape)
    return pl.BlockSpec(shape, lambda *_: (0,) * nd, pipeline_mode=pl.Buffered(1))


def _ada_kernel(c_ref, w_ref, b_ref, o_ref):
    o_ref[...] = jnp.dot(c_ref[...], w_ref[...], preferred_element_type=F32, precision=HI) + b_ref[...]


def _ada(c_all, w_ada, b_ada):
    rows = c_all.shape[0]
    tn = 512
    return pl.pallas_call(
        _ada_kernel,
        grid=(3 * D_MODEL // tn,),
        in_specs=[pl.BlockSpec((rows, D_MODEL), lambda j: (0, 0)),
                  pl.BlockSpec((D_MODEL, tn), lambda j: (0, j)),
                  pl.BlockSpec((1, tn), lambda j: (0, j))],
        out_specs=pl.BlockSpec((rows, tn), lambda j: (0, j)),
        out_shape=jax.ShapeDtypeStruct((rows, 3 * D_MODEL), F32),
        compiler_params=_cparams(("arbitrary",)),
        name="ada",
    )(c_all, w_ada, b_ada)


_PROJ_WIDTHS = (LANES, KV_RANK, MLA_HEADS * QK_ROPE, MLA_HEADS * QK_NOPE, DN_CONV_CH, DN_V, MLA_V, D_MODEL, D_MODEL)


_OFF_ZA = DN_CONV_CH
_OFF_BETA = _OFF_ZA + DN_V
_OFF_Q = _OFF_BETA + 2 * DN_HEADS
_OFF_CKV = _OFF_Q + MLA_Q
_OFF_KR = _OFF_CKV + KV_RANK
_OFF_ZB = _OFF_KR + QK_ROPE
_OFF_GA = _OFF_ZB + MLA_V
_OFF_GB = _OFF_GA + D_MODEL


def _pack_kernel(w_ref, ckv_ref, qr_ref, qn_ref, qkv_ref, za_ref, zb_ref, ga_ref, gb_ref):
    def cols(start, width):
        return w_ref[:, start:start + width].astype(BF16)

    qkv_ref[...] = cols(0, DN_CONV_CH)
    za_ref[...] = cols(_OFF_ZA, DN_V)
    for h in range(MLA_HEADS):
        qn_ref[:, h * QK_NOPE:(h + 1) * QK_NOPE] = cols(_OFF_Q + h * QK_HEAD, QK_NOPE)
        qr_ref[:, h * QK_ROPE:(h + 1) * QK_ROPE] = cols(_OFF_Q + h * QK_HEAD + QK_NOPE, QK_ROPE)
    ckv_ref[...] = cols(_OFF_CKV, KV_RANK)
    zb_ref[...] = cols(_OFF_ZB, MLA_V)
    ga_ref[...] = cols(_OFF_GA, D_MODEL)
    gb_ref[...] = cols(_OFF_GB, D_MODEL)


def _pack_w_in(w_in):
    rows = 256
    groups = pl.pallas_call(
        _pack_kernel,
        grid=(D_MODEL // rows,),
        in_specs=[pl.BlockSpec((rows, w_in.shape[1]), lambda i: (i, 0))],
        out_specs=[pl.BlockSpec((rows, w), lambda i: (i, 0)) for w in _PROJ_WIDTHS[1:]],
        out_shape=[jax.ShapeDtypeStruct((D_MODEL, w), BF16) for w in _PROJ_WIDTHS[1:]],
        compiler_params=_cparams(("arbitrary",)),
        name="w_pack",
    )(w_in)
    kr = w_in[:, _OFF_KR:_OFF_KR + QK_ROPE]
    gates = w_in[:, _OFF_BETA:_OFF_BETA + 2 * DN_HEADS]
    pad = jnp.zeros((D_MODEL, LANES - QK_ROPE - 2 * DN_HEADS), w_in.dtype)
    small = jnp.concatenate([kr, gates, pad], axis=1).astype(BF16)
    return (small, *groups)


def _in_proj_kernel(x_ref, mod_ref, gain_ref,
                    w_small, w_ckv, w_qr, w_qn, w_qkv, w_za, w_zb, w_ga, w_gb,
                    qng_ref, qrg_ref, kvg_ref, krg_ref, kng_ref, wuk_ref, wuv_ref,
                    small_ref, qkv_ref, za_ref, zb_ref, ga_ref, gb_ref,
                    qcat_ref, kcat_ref, v_ref, ckvn_ref, krn_ref, rope_ref, *, q_off):
    bb, tm, d = x_ref.shape
    rows = bb * tm

    @pl.when((pl.program_id(0) == 0) & (pl.program_id(1) == 0))
    def _():
        off = (lax.broadcasted_iota(jnp.int32, (rows, LANES), 0) % tm).astype(F32) * _rope_inv_freq((rows, LANES))
        rope_ref[0] = jnp.cos(off)
        rope_ref[1] = jnp.sin(off)

    x = x_ref[...]
    ms = jnp.mean(x * x, axis=-1, keepdims=True)
    y = x * lax.rsqrt(ms + EPS) * gain_ref[...]
    shift = mod_ref[:, :, 0:d]
    scale = mod_ref[:, :, d:2 * d]
    h = (y * (1.0 + scale) + shift).astype(BF16).reshape(rows, d)

    def project(w_ref, o_ref):
        o_ref[...] = _dot(h, w_ref[...]).astype(o_ref.dtype).reshape(o_ref.shape)

    sm = _dot(h, w_small[...])
    small_ref[...] = sm.reshape(bb, tm, LANES)
    ckv = _dot(h, w_ckv[...])
    qr_all = _dot(h, w_qr[...])
    qn_all = _dot(h, w_qn[...])
    project(w_qkv, qkv_ref)

    t0 = pl.program_id(1) * tm
    qscale = QK_HEAD ** -0.5 * math.log2(math.e)
    base = (t0 + q_off).astype(F32) * _rope_inv_freq((1, LANES))
    cos_a, sin_a = jnp.cos(base), jnp.sin(base)
    cos = cos_a * rope_ref[0] - sin_a * rope_ref[1]
    sin = sin_a * rope_ref[0] + cos_a * rope_ref[1]
    lane = lax.broadcasted_iota(jnp.int32, (rows, LANES), 1)
    low_half = lane < QK_ROPE
    first = (lane % QK_ROPE) < (QK_ROPE // 2)

    def rope(z):
        rot = jnp.where(first, -pltpu.roll(z, LANES - QK_ROPE // 2, 1), pltpu.roll(z, QK_ROPE // 2, 1))
        return z * cos + rot * sin

    def rms64(z):
        zz = z * z
        s_lo = jnp.sum(jnp.where(low_half, zz, 0.0), axis=-1, keepdims=True)
        s_hi = jnp.sum(jnp.where(low_half, 0.0, zz), axis=-1, keepdims=True)
        return lax.rsqrt(jnp.where(low_half, s_lo, s_hi) * (1.0 / QK_ROPE) + EPS)

    ckvn = ckv * lax.rsqrt(jnp.mean(ckv * ckv, axis=-1, keepdims=True) + EPS) * kvg_ref[...]
    ckvn_ref[...] = ckvn.reshape(bb, tm, KV_RANK)
    ckvn16 = ckvn.astype(BF16)
    kr = rope(sm * rms64(sm) * krg_ref[...])
    krn_ref[...] = kr[:, :QK_ROPE].reshape(bb, tm, QK_ROPE)
    kr_pad16 = jnp.where(low_half, kr, 0.0).astype(BF16).reshape(bb, tm, LANES)

    project(w_za, za_ref)
    k_raw = _dot(ckvn16, wuk_ref[...])
    v_ref[...] = _dot(ckvn16, wuv_ref[...]).astype(BF16).reshape(bb, tm, MLA_V)
    project(w_zb, zb_ref)

    qn = _head_rms(qn_all, qng_ref[...] * qscale, QK_NOPE)
    for c in range(MLA_HEADS // 2):
        z = qr_all[:, c * LANES:(c + 1) * LANES]
        z = rope(z * rms64(z) * qrg_ref[...]) * qscale
        even = jnp.where(low_half, z, 0.0)
        odd = jnp.where(low_half, pltpu.roll(z, QK_ROPE, 1), 0.0)
        for hh, part in ((2 * c, even), (2 * c + 1, odd)):
            qcat_ref[:, :, hh * QK_CAT:hh * QK_CAT + QK_NOPE] = qn[hh].astype(BF16).reshape(bb, tm, QK_NOPE)
            qcat_ref[:, :, hh * QK_CAT + QK_NOPE:(hh + 1) * QK_CAT] = part.astype(BF16).reshape(bb, tm, LANES)
    project(w_ga, ga_ref)

    kn = _head_rms(k_raw, kng_ref[...], QK_NOPE)
    for hh in range(MLA_HEADS):
        kcat_ref[:, :, hh * QK_CAT:hh * QK_CAT + QK_NOPE] = kn[hh].astype(BF16).reshape(bb, tm, QK_NOPE)
        kcat_ref[:, :, hh * QK_CAT + QK_NOPE:(hh + 1) * QK_CAT] = kr_pad16
    project(w_gb, gb_ref)


def _in_proj(x, mod3, gain, w_pack, mla, bb, tm, q_off):
    B, T, _ = x.shape
    row = lambda b, t: (b, t, 0)
    outs = ((LANES, F32), (DN_CONV_CH, BF16), (DN_V, BF16), (MLA_V, BF16), (D_MODEL, BF16), (D_MODEL, BF16),
            (MLA_HEADS * QK_CAT, BF16), (MLA_HEADS * QK_CAT, BF16), (MLA_V, BF16), (KV_RANK, F32), (QK_ROPE, F32))
    return pl.pallas_call(
        functools.partial(_in_proj_kernel, q_off=q_off),
        grid=(B // bb, T // tm),
        in_specs=[pl.BlockSpec((bb, tm, D_MODEL), row),
                  pl.BlockSpec((bb, 1, 3 * D_MODEL), lambda b, t: (b, 0, 0)),
                  _resident((1, D_MODEL))]
        + [_resident((D_MODEL, w)) for w in _PROJ_WIDTHS]
        + [_resident((1, QK_NOPE)), _resident((1, LANES)), _resident((1, KV_RANK)), _resident((1, LANES)),
           _resident((1, QK_NOPE)), _resident((KV_RANK, MLA_HEADS * QK_NOPE)), _resident((KV_RANK, MLA_V))],
        out_specs=[pl.BlockSpec((bb, tm, w), row) for w, _ in outs],
        out_shape=[jax.ShapeDtypeStruct((B, T, w), dt) for w, dt in outs],
        scratch_shapes=[pltpu.VMEM((2, bb * tm, LANES), F32)],
        compiler_params=_cparams(("arbitrary", "arbitrary")),
        name="in_proj",
    )(x, mod3, gain, *w_pack, *mla)


def _softplus(x):
    return jnp.maximum(x, 0.0) + jnp.log(1.0 + jnp.exp(-jnp.abs(x)))


def _dn_prep_kernel(qkv_ref, prev_ref, cs_ref, small_ref, wconv_ref, alog_ref, dtb_ref,
                    w_ref, u_ref, qe_ref, kd_ref, attn_ref, egl_ref, l_scr, rhs_scr, *, C, tiles_per_seq):
    step = pl.program_id(0)
    tile_idx = jnp.minimum(step, pl.num_programs(0) - 2) % tiles_per_seq
    tm = qkv_ref.shape[1]
    nc = tm // C
    pad = prev_ref.shape[1]

    wr = step % 2
    rd = 1 - wr

    @pl.when(step == 0)
    def _():
        l_scr[...] = jnp.zeros_like(l_scr)
        rhs_scr[...] = jnp.zeros_like(rhs_scr)

    x16 = qkv_ref[0]
    hist = jnp.where(tile_idx == 0, cs_ref[0], prev_ref[0].astype(F32))
    hist_hi = hist.astype(BF16)
    rem = hist - hist_hi.astype(F32)
    hist_mid = rem.astype(BF16)
    hist_lo = (rem - hist_mid.astype(F32)).astype(BF16)
    n_sh = CONV_W - 1
    rs = min(tm, CONV_ROWS)

    def shifted_taps(pieces, xs):
        npc = len(pieces)
        full16 = jnp.concatenate(list(pieces) + [xs], axis=0)
        srow = lax.broadcasted_iota(jnp.int32, (n_sh * rs, npc * pad + rs), 0)
        scol = lax.broadcasted_iota(jnp.int32, (n_sh * rs, npc * pad + rs), 1)
        src = srow % rs + srow // rs + pad - n_sh
        sel = scol == src + (npc - 1) * pad
        for p in range(npc - 1):
            sel = sel | ((scol == src + p * pad) & (scol < (p + 1) * pad))
        return _dot(sel.astype(BF16), full16)

    conv_parts = []
    for j in range(tm // rs):
        xs = x16[j * rs:(j + 1) * rs]
        pieces = (hist_hi, hist_mid, hist_lo) if j == 0 else (x16[j * rs - pad:j * rs],)
        shifted = shifted_taps(pieces, xs)
        part = xs.astype(F32) * wconv_ref[n_sh:CONV_W, :]
        for i in range(n_sh):
            part = part + shifted[i * rs:(i + 1) * rs, :] * wconv_ref[i:i + 1, :]
        conv_parts.append(part)
    conv = conv_parts[0] if len(conv_parts) == 1 else jnp.concatenate(conv_parts, axis=0)
    act = conv * jax.nn.sigmoid(conv)

    sm = small_ref[0]
    beta_all = jax.nn.sigmoid(sm)
    g_all = -jnp.exp(alog_ref[...]) * _softplus(sm + dtb_ref[...])
    rt = lax.broadcasted_iota(jnp.int32, (tm, tm), 0)
    ct = lax.broadcasted_iota(jnp.int32, (tm, tm), 1)
    chunk_tri = ((rt // C == ct // C) & (rt >= ct)).astype(F32)
    gcum = jnp.dot(chunk_tri, g_all, preferred_element_type=F32, precision=HI)
    sel = (lax.broadcasted_iota(jnp.int32, (DN_HEADS, LANES), 1)
           == lax.broadcasted_iota(jnp.int32, (DN_HEADS, LANES), 0) + ALPHA_OFF).astype(F32)
    gcum_t = _dot_nt(sel, gcum, precision=HI)

    def per_head_lanes(x, off):
        hi = x.astype(BF16)
        r1 = x - hi.astype(F32)
        mid = r1.astype(BF16)
        lo = (r1 - mid.astype(F32)).astype(BF16)
        erow = lax.broadcasted_iota(jnp.int32, (3 * LANES, DN_QK), 0) % LANES
        ecol = lax.broadcasted_iota(jnp.int32, (3 * LANES, DN_QK), 1) // DN_DK
        return _dot(jnp.concatenate([hi, mid, lo], axis=1), (erow == ecol + off).astype(BF16))

    g_b = per_head_lanes(gcum, ALPHA_OFF)
    beta_b = per_head_lanes(beta_all, BETA_OFF)
    glast_b = jnp.concatenate(
        [jnp.broadcast_to(g_b[c * C + C - 1:(c + 1) * C, :], (C, DN_QK)) for c in range(nc)], axis=0)
    eg_b = jnp.exp(g_b)
    kdf_b = jnp.exp(glast_b - g_b)
    for c in range(nc):
        egl_ref[0, c] = jnp.exp(g_b[c * C + C - 1:(c + 1) * C, :])

    ri = lax.broadcasted_iota(jnp.int32, (C, C), 0)
    ci = lax.broadcasted_iota(jnp.int32, (C, C), 1)
    tri_incl = ri >= ci
    tri_strict = ri > ci
    eye = (ri == ci).astype(F32)
    pair_masks = []
    m = 1
    while m < C:
        pair_masks.append((ri // (2 * m) == ci // (2 * m)) & (ri // m != ci // m))
        m *= 2

    heads = range(DN_HEADS)
    hl = lambda h: slice(h * DN_DK, (h + 1) * DN_DK)
    items = [(c, h) for c in range(nc) for h in heads]
    rows = lambda c: slice(c * C, (c + 1) * C)

    l_prev = [l_scr[rd, i] for i in range(len(items))]
    pinv = [eye - jnp.where(pair_masks[0], l, 0.0) for l in l_prev]
    for mask in pair_masks[1:]:
        p16 = [p.astype(BF16) for p in pinv]
        tmp = [_dot(p16[i], jnp.where(mask, l_prev[i], 0.0).astype(BF16)).astype(BF16) for i in range(len(items))]
        pinv = [pinv[i] - _dot(tmp[i], p16[i]) for i in range(len(items))]
    for i, (c, h) in enumerate(items):
        wu = _dot(pinv[i].astype(BF16), rhs_scr[rd, i])
        w_ref[0, c, :, hl(h)] = wu[:, :DN_DK].astype(BF16)
        u_ref[0, rows(c), hl(h)] = wu[:, DN_DK:]

    qn, kn = [], []
    for h in heads:
        qh = act[:, h * DN_DK:(h + 1) * DN_DK]
        kh = act[:, DN_QK + h * DN_DK:DN_QK + (h + 1) * DN_DK]
        qn.append(qh * lax.rsqrt(jnp.sum(qh * qh, axis=-1, keepdims=True) + EPS) * (DN_DK ** -0.5))
        kn.append(kh * lax.rsqrt(jnp.sum(kh * kh, axis=-1, keepdims=True) + EPS))

    k16 = [kn[h].astype(BF16) for h in heads]
    q16 = [qn[h].astype(BF16) for h in heads]
    kb = [kn[h] * beta_b[:, hl(h)] for h in heads]
    kb16 = [kb[h].astype(BF16) for h in heads]
    kbe16 = [(kb[h] * eg_b[:, hl(h)]).astype(BF16) for h in heads]
    vb16 = [(act[:, 2 * DN_QK + h * DN_DV:2 * DN_QK + (h + 1) * DN_DV] * beta_b[:, hl(h)]).astype(BF16)
            for h in heads]
    qe16 = [(qn[h] * eg_b[:, hl(h)]).astype(BF16) for h in heads]
    for h in heads:
        kd_ref[0, :, hl(h)] = (kn[h] * kdf_b[:, hl(h)]).astype(BF16)

    attn_ref[...] = jnp.zeros_like(attn_ref)
    decay, qk = [], []
    for i, (c, h) in enumerate(items):
        r = rows(c)
        gc = g_b[r, h * DN_DK:h * DN_DK + C]
        decay.append(jnp.exp(jnp.where(tri_incl, gc - gcum_t[h:h + 1, r], -1e30)))
        rhs_scr[wr, i] = jnp.concatenate([kbe16[h][r], vb16[h][r]], axis=1)
        qe_ref[0, c, :, hl(h)] = qe16[h][r]
        qk.append(_dot_nt(jnp.concatenate([kb16[h][r], q16[h][r]], axis=0), k16[h][r]))

    for i, (c, h) in enumerate(items):
        l_scr[wr, i] = jnp.where(tri_strict, qk[i][:C] * decay[i], 0.0)
        attn_ref[0, rows(c), h * DN_DK:h * DN_DK + C] = (qk[i][C:] * decay[i]).astype(BF16)


def _dn_scan_kernel(w_ref, qe_ref, u_ref, kd_ref, attn_ref, egl_ref, za_ref, s0_ref, onorm_ref,
                    ua_ref, sfin_ref, s_ref, *, C):
    n = pl.program_id(1)
    bg, G = w_ref.shape[0], w_ref.shape[1]

    @pl.when(n == 0)
    def _():
        s_ref[...] = s0_ref[...]

    chains = [(b, h) for b in range(bg) for h in range(DN_HEADS)]
    for g in range(G):
        r = slice(g * C, (g + 1) * C)
        s_old = [s_ref[b, h] for b, h in chains]
        s16 = [s.astype(BF16) for s in s_old]
        ws = [_dot(jnp.concatenate([w_ref[b, g, :, h * DN_DK:(h + 1) * DN_DK],
                                    qe_ref[b, g, :, h * DN_DK:(h + 1) * DN_DK]], axis=0), s16[i])
              for i, (b, h) in enumerate(chains)]
        v16 = [(u_ref[b, r, h * DN_DV:(h + 1) * DN_DV] - ws[i][:C]).astype(BF16)
               for i, (b, h) in enumerate(chains)]
        for i, (b, h) in enumerate(chains):
            lo = h * DN_DK
            s_ref[b, h] = s_old[i] * egl_ref[b, g, :, lo:lo + DN_DK] + _dot_tn(kd_ref[b, r, lo:lo + DN_DK], v16[i])
        for i, (b, h) in enumerate(chains):
            lo = h * DN_DV
            o = ws[i][C:] + _dot(attn_ref[b, r, lo:lo + C], v16[i])
            o = o * lax.rsqrt(jnp.mean(o * o, axis=-1, keepdims=True) + EPS) * onorm_ref[...]
            z = za_ref[b, r, lo:lo + DN_DV].astype(F32)
            ua_ref[b, r, lo:lo + DN_DV] = (o * (z * jax.nn.sigmoid(z))).astype(BF16)

    @pl.when(n == pl.num_programs(1) - 1)
    def _():
        sfin_ref[...] = s_ref[...]


def _deltanet(qkv, small, z_a, conv_state, s0, w_conv, alog_v, dtb_v, onorm, tm, bg, G):
    B, T, _ = qkv.shape
    C = min(CHUNK, T)
    N = T // C
    nc = tm // C
    hist_rows = 16
    cs = jnp.pad(conv_state, ((0, 0), (hist_rows - (CONV_W - 1), 0), (0, 0)))
    nt = T // tm
    n_tiles = B * nt

    def cur(s):
        s = jnp.minimum(s, n_tiles - 1)
        return s // nt, s % nt

    def done(s):
        s = jnp.maximum(s - 1, 0)
        return s // nt, s % nt

    tile = lambda s: (*cur(s), 0)
    tile4 = lambda s: (*cur(s), 0, 0)
    prev = lambda s: (cur(s)[0], jnp.maximum(cur(s)[1] * (tm // hist_rows) - 1, 0), 0)
    w, u, qe, kd, attn, egl = pl.pallas_call(
        functools.partial(_dn_prep_kernel, C=C, tiles_per_seq=nt),
        grid=(n_tiles + 1,),
        in_specs=[pl.BlockSpec((1, tm, DN_CONV_CH), tile),
                  pl.BlockSpec((1, hist_rows, DN_CONV_CH), prev),
                  pl.BlockSpec((1, hist_rows, DN_CONV_CH), lambda s: (cur(s)[0], 0, 0)),
                  pl.BlockSpec((1, tm, LANES), tile),
                  _resident((CONV_W, DN_CONV_CH)),
                  _resident((1, LANES)),
                  _resident((1, LANES))],
        out_specs=[pl.BlockSpec((1, nc, C, DN_QK), lambda s: (*done(s), 0, 0)),
                   pl.BlockSpec((1, tm, DN_V), lambda s: (*done(s), 0)),
                   pl.BlockSpec((1, nc, C, DN_QK), tile4),
                   pl.BlockSpec((1, tm, DN_QK), tile),
                   pl.BlockSpec((1, tm, DN_V), tile),
                   pl.BlockSpec((1, nc, 1, DN_QK), tile4)],
        out_shape=[jax.ShapeDtypeStruct((B, N, C, DN_QK), BF16),
                   jax.ShapeDtypeStruct((B, T, DN_V), F32),
                   jax.ShapeDtypeStruct((B, N, C, DN_QK), BF16),
                   jax.ShapeDtypeStruct((B, T, DN_QK), BF16),
                   jax.ShapeDtypeStruct((B, T, DN_V), BF16),
                   jax.ShapeDtypeStruct((B, N, 1, DN_QK), F32)],
        scratch_shapes=[pltpu.VMEM((2, nc * DN_HEADS, C, C), F32),
                        pltpu.VMEM((2, nc * DN_HEADS, C, DN_DK + DN_DV), BF16)],
        compiler_params=_cparams(("arbitrary",)),
        name="dn_prep",
    )(qkv, qkv, cs, small, w_conv, alog_v, dtb_v)

    grp = lambda b, n: (b, n, 0)
    grp4 = lambda b, n: (b, n, 0, 0)
    state = pl.BlockSpec((bg, DN_HEADS, DN_DK, DN_DV), lambda b, n: (b, 0, 0, 0))
    u_a, s_new = pl.pallas_call(
        functools.partial(_dn_scan_kernel, C=C),
        grid=(B // bg, N // G),
        in_specs=[pl.BlockSpec((bg, G, C, DN_QK), grp4),
                  pl.BlockSpec((bg, G, C, DN_QK), grp4),
                  pl.BlockSpec((bg, G * C, DN_V), grp),
                  pl.BlockSpec((bg, G * C, DN_QK), grp),
                  pl.BlockSpec((bg, G * C, DN_V), grp),
                  pl.BlockSpec((bg, G, 1, DN_QK), grp4),
                  pl.BlockSpec((bg, G * C, DN_V), grp),
                  state,
                  _resident((1, DN_DV))],
        out_specs=[pl.BlockSpec((bg, G * C, DN_V), grp), state],
        out_shape=[jax.ShapeDtypeStruct((B, T, DN_V), BF16),
                   jax.ShapeDtypeStruct((B, DN_HEADS, DN_DK, DN_DV), F32)],
        scratch_shapes=[pltpu.VMEM((bg, DN_HEADS, DN_DK, DN_DV), F32)],
        compiler_params=_cparams(("arbitrary", "arbitrary")),
        name="dn_scan",
    )(w, qe, u, kd, attn, egl, z_a, s0, onorm)
    conv_new = qkv[:, T - (CONV_W - 1):, :].astype(F32)
    return u_a, s_new, conv_new


def _head_rms(x, gain_row, width):
    outs = []
    for h in range(x.shape[1] // width):
        xh = x[:, h * width:(h + 1) * width]
        outs.append(xh * lax.rsqrt(jnp.mean(xh * xh, axis=-1, keepdims=True) + EPS) * gain_row)
    return outs


def _rope_inv_freq(shape):
    half = QK_ROPE // 2
    fidx = (lax.broadcasted_iota(jnp.int32, shape, 1) % half).astype(F32)
    return jnp.exp(fidx * (-math.log(ROPE_THETA) / half))


def _chunk_mask(qpos0, kpos0, tq, tk):
    qc = (lax.broadcasted_iota(jnp.int32, (tq, tk), 0) + qpos0) // CHUNK
    kc = (lax.broadcasted_iota(jnp.int32, (tq, tk), 1) + kpos0) // CHUNK
    return kc <= qc


def _attn_prompt_kernel(q_ref, k_ref, v_ref, o_ref, m_ref, l_ref, acc_ref, *, tq, tk, sub):
    i = pl.program_id(2)
    nsub = tq // sub
    ratio = tq // tk
    m_ref[...] = jnp.full_like(m_ref, -1e30)
    l_ref[...] = jnp.zeros_like(l_ref)
    acc_ref[...] = jnp.zeros_like(acc_ref)

    def scores(r, k):
        return _dot_nt(q_ref[0, r * sub:(r + 1) * sub, :], k)

    def softmax_pv(r, sr, v, mask):
        rows = slice(r * sub, (r + 1) * sub)
        if mask is not None:
            sr = jnp.where(mask, sr, -1e30)
        m_old = m_ref[rows, :]
        m_new = jnp.maximum(m_old, jnp.max(sr, axis=-1, keepdims=True))
        alpha = jnp.exp2(m_old - m_new)
        p = jnp.exp2(sr - jnp.tile(m_new, (1, tk // LANES)))
        psum = p[:, 0:LANES]
        for c in range(1, tk // LANES):
            psum = psum + p[:, c * LANES:(c + 1) * LANES]
        l_ref[rows, :] = alpha * l_ref[rows, :] + psum
        acc_ref[rows, :] = alpha * acc_ref[rows, :] + _dot(p.astype(BF16), v)
        m_ref[rows, :] = m_new

    def run(j0, items):
        kv = {}
        for d in sorted({d for d, _, _ in items}):
            start = pl.multiple_of((j0 + d) * tk, tk)
            kv[d] = (k_ref[0, pl.ds(start, tk), :], v_ref[0, pl.ds(start, tk), :])
        s = {n: scores(items[n][1], kv[items[n][0]][0]) for n in range(min(ATTN_LOOKAHEAD, len(items)))}
        for n, (d, r, mask) in enumerate(items):
            ahead = n + ATTN_LOOKAHEAD
            if ahead < len(items):
                s[ahead] = scores(items[ahead][1], kv[items[ahead][0]][0])
            softmax_pv(r, s.pop(n), kv[d][1], mask)

    per_trip = math.gcd(ratio, ATTN_TILES_PER_TRIP)
    full = [(d, r, None) for d in range(per_trip) for r in range(nsub)]

    def body(jj, carry):
        run(jj * per_trip, full)
        return carry

    lax.fori_loop(0, i * (ratio // per_trip), body, 0)
    diag = []
    for d in range(ratio):
        for r in range(nsub):
            q_lo, q_hi = (r * sub) // CHUNK, (r * sub + sub - 1) // CHUNK
            k_lo, k_hi = (d * tk) // CHUNK, (d * tk + tk - 1) // CHUNK
            if k_lo > q_hi:
                continue
            diag.append((d, r, None if k_hi <= q_lo else _chunk_mask(r * sub, d * tk, sub, tk)))
    run(i * ratio, diag)
    l = jnp.sum(l_ref[...], axis=-1, keepdims=True)
    o_ref[0] = (acc_ref[...] / l).astype(BF16)


def _attn_prompt(qcat, kcat, v, tq, tk):
    B, T, _ = v.shape
    return pl.pallas_call(
        functools.partial(_attn_prompt_kernel, tq=tq, tk=tk, sub=ATTN_SUB),
        grid=(B, MLA_HEADS, T // tq),
        in_specs=[pl.BlockSpec((1, tq, QK_CAT), lambda b, h, i: (b, i, h)),
                  pl.BlockSpec((1, T, QK_CAT), lambda b, h, i: (b, 0, h)),
                  pl.BlockSpec((1, T, V_HEAD), lambda b, h, i: (b, 0, h))],
        out_specs=pl.BlockSpec((1, tq, V_HEAD), lambda b, h, i: (b, i, h)),
        out_shape=jax.ShapeDtypeStruct((B, T, MLA_V), BF16),
        scratch_shapes=[pltpu.VMEM((tq, LANES), F32), pltpu.VMEM((tq, LANES), F32),
                        pltpu.VMEM((tq, V_HEAD), F32)],
        compiler_params=_cparams(("arbitrary", "arbitrary", "arbitrary")),
        name="attn_prompt",
    )(qcat, kcat, v)


def _attn_sample_kernel(q_ref, kn_ref, vn_ref, ckv_ref, kr_ref, kng_ref, wuk_ref, wuv_ref, o_ref,
                        m_ref, l_ref, acc_ref, *, tk):
    T = q_ref.shape[1]
    P = ckv_ref.shape[1]
    R = MLA_HEADS * T
    q = q_ref[0]
    row_head = lax.broadcasted_iota(jnp.int32, (R, 1), 0) // T
    q_chunk = (lax.broadcasted_iota(jnp.int32, (R, 1), 0) % T + P) // CHUNK

    def stacked(x, width):
        lane_head = lax.broadcasted_iota(jnp.int32, (R, x.shape[1]), 1) // width
        return jnp.where(lane_head == row_head, jnp.concatenate([x] * MLA_HEADS, axis=0), jnp.zeros((), x.dtype))

    q_nope = stacked(jnp.concatenate([q[:, h * QK_CAT:h * QK_CAT + QK_NOPE] for h in range(MLA_HEADS)], axis=1),
                     QK_NOPE)
    q_rope = jnp.concatenate([q[:, h * QK_CAT + QK_NOPE:h * QK_CAT + QK_HEAD] for h in range(MLA_HEADS)],
                             axis=0)

    def update(s, k_chunk):
        s = jnp.where(k_chunk <= q_chunk, s, -1e30)
        m_old = m_ref[...]
        m_new = jnp.maximum(m_old, jnp.max(s, axis=-1, keepdims=True))
        alpha = jnp.exp2(m_old - m_new)
        p = jnp.exp2(s - m_new)
        l_ref[...] = alpha * l_ref[...] + jnp.sum(p, axis=-1, keepdims=True)
        m_ref[...] = m_new
        return alpha, p.astype(BF16)

    m_ref[...] = jnp.full_like(m_ref, -1e30)
    l_ref[...] = jnp.zeros_like(l_ref)
    acc_ref[...] = jnp.zeros_like(acc_ref)

    def body(c, carry):
        start = pl.multiple_of(c * tk, tk)
        ckv16 = ckv_ref[0, pl.ds(start, tk), :].astype(BF16)
        kn = _head_rms(_dot(ckv16, wuk_ref[...]), kng_ref[...], QK_NOPE)
        kn16 = jnp.concatenate([x.astype(BF16) for x in kn], axis=1)
        s = _dot_nt(q_nope, kn16) + _dot_nt(q_rope, kr_ref[0, pl.ds(start, tk), :].astype(BF16))
        k_chunk = (lax.broadcasted_iota(jnp.int32, (1, tk), 1) + start) // CHUNK
        alpha, p16 = update(s, k_chunk)
        acc_ref[...] = alpha * acc_ref[...] + _dot(p16, ckv16)
        return carry

    lax.fori_loop(0, P // tk, body, 0)

    s_new = _dot_nt(stacked(q, QK_CAT), kn_ref[0])
    alpha, p16 = update(s_new, (lax.broadcasted_iota(jnp.int32, (1, T), 1) + P) // CHUNK)
    pc16 = (alpha * acc_ref[...]).astype(BF16)
    inv_l = 1.0 / l_ref[...]
    for h in range(MLA_HEADS):
        rows = slice(h * T, (h + 1) * T)
        lanes = slice(h * V_HEAD, (h + 1) * V_HEAD)
        o = _dot(pc16[rows], wuv_ref[:, lanes]) + _dot(p16[rows], vn_ref[0, :, lanes])
        o_ref[0, :, lanes] = (o * inv_l[rows]).astype(BF16)


def _attn_sample(qcat, kcat_new, v_new, past_ckv, past_kr, kn_gain, w_uk16, w_uv16, tk):
    B, T, _ = v_new.shape
    P = past_ckv.shape[1]
    R = MLA_HEADS * T
    perb = lambda b: (b, 0, 0)
    return pl.pallas_call(
        functools.partial(_attn_sample_kernel, tk=tk),
        grid=(B,),
        in_specs=[pl.BlockSpec((1, T, MLA_HEADS * QK_CAT), perb),
                  pl.BlockSpec((1, T, MLA_HEADS * QK_CAT), perb),
                  pl.BlockSpec((1, T, MLA_V), perb),
                  pl.BlockSpec((1, P, KV_RANK), perb),
                  pl.BlockSpec((1, P, QK_ROPE), perb),
                  _resident((1, QK_NOPE)),
                  _resident((KV_RANK, MLA_HEADS * QK_NOPE)),
                  _resident((KV_RANK, MLA_V))],
        out_specs=pl.BlockSpec((1, T, MLA_V), perb),
        out_shape=jax.ShapeDtypeStruct((B, T, MLA_V), BF16),
        scratch_shapes=[pltpu.VMEM((R, 1), F32), pltpu.VMEM((R, 1), F32), pltpu.VMEM((R, KV_RANK), F32)],
        compiler_params=_cparams(("arbitrary",)),
        name="attn_sample",
    )(qcat, kcat_new, v_new, past_ckv, past_kr, kn_gain, w_uk16, w_uv16)


def _out_kernel(x_ref, mod_ref, ua_ref, ob_ref, zb_ref, ga_ref, gb_ref, wdn_ref, wmla_ref, wout_ref, y_ref):
    bb, tm, d = x_ref.shape
    rows = bb * tm
    zb = zb_ref[...].astype(F32)
    ub = (ob_ref[...].astype(F32) * (zb * jax.nn.sigmoid(zb))).astype(BF16).reshape(rows, d)
    ya = _dot(ua_ref[...].reshape(rows, d), wdn_ref[...])
    yb = _dot(ub, wmla_ref[...])
    ga = jax.nn.sigmoid(ga_ref[...].astype(F32)).reshape(rows, d)
    gb = jax.nn.sigmoid(gb_ref[...].astype(F32)).reshape(rows, d)
    merged = (ga * ya + gb * yb).astype(BF16)
    out = _dot(merged, wout_ref[...]).reshape(bb, tm, d)
    gate = mod_ref[:, :, 2 * d:3 * d]
    y_ref[...] = x_ref[...] + gate * out


def _out_proj(x, mod3, u_a, o_b, z_b, g_a, g_b, w_dn16, w_mla16, w_out16, bb, tm):
    B, T, _ = x.shape
    row = lambda b, t: (b, t, 0)
    act = pl.BlockSpec((bb, tm, D_MODEL), row)
    return pl.pallas_call(
        _out_kernel,
        grid=(B // bb, T // tm),
        in_specs=[act, pl.BlockSpec((bb, 1, 3 * D_MODEL), lambda b, t: (b, 0, 0)), act, act, act, act, act,
                  _resident((D_MODEL, D_MODEL)), _resident((D_MODEL, D_MODEL)), _resident((D_MODEL, D_MODEL))],
        out_specs=act,
        out_shape=jax.ShapeDtypeStruct((B, T, D_MODEL), F32),
        compiler_params=_cparams(("arbitrary", "arbitrary")),
        name="out_proj",
    )(x, mod3, u_a, o_b, z_b, g_a, g_b, w_dn16, w_mla16, w_out16)


def _lane_vec(v, off):
    return jnp.zeros((1, LANES), F32).at[0, off:off + v.shape[0]].set(v)


def _tiles(B, T, cached):
    if cached:
        whole = (B, T)
        return dict(proj=whole, out=whole, dn=dict(tm=T, bg=2, G=1), attn_tk=512)
    return dict(proj=(1, 256), out=(1, 512), dn=dict(tm=256, bg=B, G=4), attn=(2048, 512))


def _layer(x, mod, conv_state, s0, past, prm, q_off):
    B, T, _ = x.shape
    tiles = _tiles(B, T, past is not None)
    mod3 = mod.reshape(B, 1, 3 * D_MODEL)
    mla = (prm["q_nope_norm"], prm["qr_gain"], prm["kv_norm"], prm["kr_gain"], prm["k_nope_norm"],
           prm["w_uk"], prm["w_uv"])
    small, qkv, z_a, z_b, g_a, g_b, qcat, kcat, v, ckv_new, kr_new = _in_proj(
        x, mod3, prm["norm_gain"], prm["w_pack"], mla, *tiles["proj"], q_off)
    u_a, s_new, conv_new = _deltanet(qkv, small, z_a, conv_state, s0, prm["w_conv"], prm["alog_v"],
                                     prm["dtb_v"], prm["dn_out_norm"], **tiles["dn"])
    if past is None:
        o_b = _attn_prompt(qcat, kcat, v, *tiles["attn"])
    else:
        past_ckv, past_kr = past
        o_b = _attn_sample(qcat, kcat, v, past_ckv, past_kr, prm["k_nope_norm"], prm["w_uk"], prm["w_uv"],
                           tiles["attn_tk"])
    y = _out_proj(x, mod3, u_a, o_b, z_b, g_a, g_b, prm["w_o_dn"], prm["w_o_mla"], prm["w_out"], *tiles["out"])
    return y, conv_new, s_new, ckv_new, kr_new


def kernel(x_prompt, x_sample, c_prompt, c_sample, cache_ckv, cache_krope, state_delta, state_conv, norm_gain, w_ada, b_ada, w_in, w_conv, a_log, dt_bias, dn_out_norm, q_nope_norm, q_rope_norm, k_nope_norm, k_rope_norm, kv_norm, w_uk, w_uv, w_o_dn, w_o_mla, w_out):
    depth = w_in.shape[0]
    assert depth == 1, "single-layer configuration"
    l = 0
    B, T, _ = x_prompt.shape
    Bs, Ts, _ = x_sample.shape
    past_len = cache_ckv.shape[2]

    row = lambda v: v.reshape(1, -1).astype(F32)
    prm = dict(
        norm_gain=row(norm_gain[l]),
        w_pack=_pack_w_in(w_in[l]),
        w_conv=w_conv[l],
        alog_v=_lane_vec(a_log[l], ALPHA_OFF),
        dtb_v=_lane_vec(dt_bias[l], ALPHA_OFF),
        dn_out_norm=row(dn_out_norm[l]),
        q_nope_norm=row(q_nope_norm[l]),
        qr_gain=jnp.tile(row(q_rope_norm[l]), (1, LANES // QK_ROPE)),
        kv_norm=row(kv_norm[l]),
        kr_gain=_lane_vec(k_rope_norm[l], KR_OFF),
        k_nope_norm=row(k_nope_norm[l]),
        w_uk=w_uk[l].astype(BF16),
        w_uv=w_uv[l].astype(BF16),
        w_o_dn=w_o_dn[l].astype(BF16),
        w_o_mla=w_o_mla[l].astype(BF16),
        w_out=w_out[l].astype(BF16),
    )

    rows = B + Bs
    rows_pad = -(-rows // 8) * 8
    c_all = jnp.concatenate([c_prompt, c_sample, jnp.zeros((rows_pad - rows, D_MODEL), F32)], axis=0)
    mod = _ada(c_all, w_ada[l], b_ada[l].reshape(1, -1))

    zeros_conv = jnp.zeros((B, CONV_W - 1, DN_CONV_CH), F32)
    zeros_state = jnp.zeros((B, DN_HEADS, DN_DK, DN_DV), F32)
    yp, cvp, sdp, kvp, krp = _layer(x_prompt, mod[:B], zeros_conv, zeros_state, None, prm, q_off=0)
    ys, cvs, sds, kvs, krs = _layer(x_sample, mod[B:rows], state_conv[l], state_delta[l],
                                    (cache_ckv[l], cache_krope[l]), prm, q_off=past_len)
    st = lambda a: a[None]
    return (yp, ys, st(kvp), st(krp), st(sdp), st(cvp), st(kvs), st(krs), st(sds), st(cvs))
```

```python
import functools
import math

import jax
import jax.numpy as jnp
from jax import lax
from jax.experimental import pallas as pl
from jax.experimental.pallas import tpu as pltpu

D_MODEL = 1024
CHUNK = 64
EPS = 1e-6
DN_HEADS = 8
DN_DK = 128
DN_DV = 128
DN_QK = DN_HEADS * DN_DK
DN_V = DN_HEADS * DN_DV
DN_CONV_CH = 2 * DN_QK + DN_V
CONV_W = 4
MLA_HEADS = 8
QK_NOPE = 128
QK_ROPE = 64
QK_HEAD = QK_NOPE + QK_ROPE
V_HEAD = 128
KV_RANK = 512
MLA_Q = MLA_HEADS * QK_HEAD
MLA_V = MLA_HEADS * V_HEAD
ROPE_THETA = 10000.0

LANES = 128
QK_CAT = 256
CONV_ROWS = 128
ATTN_SUB = 256
ATTN_LOOKAHEAD = 2
ATTN_TILES_PER_TRIP = 4
KR_OFF = 0
BETA_OFF = QK_ROPE
ALPHA_OFF = QK_ROPE + DN_HEADS
VMEM_LIMIT = 56 * 1024 * 1024

F32 = jnp.float32
BF16 = jnp.bfloat16
HI = lax.Precision.HIGHEST


def _dot(a, b):
    return jnp.dot(a, b, preferred_element_type=F32)


def _dot_nt(a, b, precision=None):
    return lax.dot_general(a, b, (((1,), (1,)), ((), ())), preferred_element_type=F32, precision=precision)


def _dot_tn(a, b):
    return lax.dot_general(a, b, (((0,), (0,)), ((), ())), preferred_element_type=F32)


def _cparams(sem):
    return pltpu.CompilerParams(dimension_semantics=sem, vmem_limit_bytes=VMEM_LIMIT)


def _resident(shape):
    nd = len(shape)
    return pl.BlockSpec(shape, lambda *_: (0,) * nd, pipeline_mode=pl.Buffered(1))


def _ada_kernel(c_ref, w_ref, b_ref, o_ref):
    o_ref[...] = jnp.dot(c_ref[...], w_ref[...], preferred_element_type=F32, precision=HI) + b_ref[...]


def _ada(c_all, w_ada, b_ada):
    rows = c_all.shape[0]
    tn = 512
    return pl.pallas_call(
        _ada_kernel,
        grid=(3 * D_MODEL // tn,),
        in_specs=[pl.BlockSpec((rows, D_MODEL), lambda j: (0, 0)),
                  pl.BlockSpec((D_MODEL, tn), lambda j: (0, j)),
                  pl.BlockSpec((1, tn), lambda j: (0, j))],
        out_specs=pl.BlockSpec((rows, tn), lambda j: (0, j)),
        out_shape=jax.ShapeDtypeStruct((rows, 3 * D_MODEL), F32),
        compiler_params=_cparams(("arbitrary",)),
        name="ada",
    )(c_all, w_ada, b_ada)


_PROJ_WIDTHS = (LANES, KV_RANK, MLA_HEADS * QK_ROPE, MLA_HEADS * QK_NOPE, DN_CONV_CH, DN_V, MLA_V, D_MODEL, D_MODEL)


def _pack_w_in(w_in):
    o = 0
    qkv = w_in[:, o:o + DN_CONV_CH]; o += DN_CONV_CH
    z_a = w_in[:, o:o + DN_V]; o += DN_V
    beta = w_in[:, o:o + DN_HEADS]; o += DN_HEADS
    alpha = w_in[:, o:o + DN_HEADS]; o += DN_HEADS
    q = w_in[:, o:o + MLA_Q].reshape(D_MODEL, MLA_HEADS, QK_HEAD); o += MLA_Q
    ckv = w_in[:, o:o + KV_RANK]; o += KV_RANK
    kr = w_in[:, o:o + QK_ROPE]; o += QK_ROPE
    z_b = w_in[:, o:o + MLA_V]; o += MLA_V
    g_a = w_in[:, o:o + D_MODEL]; o += D_MODEL
    g_b = w_in[:, o:o + D_MODEL]
    qn = q[:, :, :QK_NOPE].reshape(D_MODEL, MLA_HEADS * QK_NOPE)
    qr = q[:, :, QK_NOPE:].reshape(D_MODEL, MLA_HEADS * QK_ROPE)
    pad = jnp.zeros((D_MODEL, LANES - QK_ROPE - 2 * DN_HEADS), w_in.dtype)
    small = jnp.concatenate([kr, beta, alpha, pad], axis=1)
    return tuple(w.astype(BF16) for w in (small, ckv, qr, qn, qkv, z_a, z_b, g_a, g_b))


def _in_proj_kernel(x_ref, mod_ref, gain_ref,
                    w_small, w_ckv, w_qr, w_qn, w_qkv, w_za, w_zb, w_ga, w_gb,
                    qng_ref, qrg_ref, kvg_ref, krg_ref, kng_ref, wuk_ref, wuv_ref,
                    small_ref, qkv_ref, za_ref, zb_ref, ga_ref, gb_ref,
                    qcat_ref, kcat_ref, v_ref, ckvn_ref, krn_ref, rope_ref, *, q_off):
    bb, tm, d = x_ref.shape
    rows = bb * tm

    @pl.when((pl.program_id(0) == 0) & (pl.program_id(1) == 0))
    def _():
        off = (lax.broadcasted_iota(jnp.int32, (rows, LANES), 0) % tm).astype(F32) * _rope_inv_freq((rows, LANES))
        rope_ref[0] = jnp.cos(off)
        rope_ref[1] = jnp.sin(off)

    x = x_ref[...]
    ms = jnp.mean(x * x, axis=-1, keepdims=True)
    y = x * lax.rsqrt(ms + EPS) * gain_ref[...]
    shift = mod_ref[:, :, 0:d]
    scale = mod_ref[:, :, d:2 * d]
    h = (y * (1.0 + scale) + shift).astype(BF16).reshape(rows, d)

    def project(w_ref, o_ref):
        o_ref[...] = _dot(h, w_ref[...]).astype(o_ref.dtype).reshape(o_ref.shape)

    sm = _dot(h, w_small[...])
    small_ref[...] = sm.reshape(bb, tm, LANES)
    ckv = _dot(h, w_ckv[...])
    qr_all = _dot(h, w_qr[...])
    qn_all = _dot(h, w_qn[...])
    project(w_qkv, qkv_ref)

    t0 = pl.program_id(1) * tm
    qscale = QK_HEAD ** -0.5 * math.log2(math.e)
    base = (t0 + q_off).astype(F32) * _rope_inv_freq((1, LANES))
    cos_a, sin_a = jnp.cos(base), jnp.sin(base)
    cos = cos_a * rope_ref[0] - sin_a * rope_ref[1]
    sin = sin_a * rope_ref[0] + cos_a * rope_ref[1]
    lane = lax.broadcasted_iota(jnp.int32, (rows, LANES), 1)
    low_half = lane < QK_ROPE
    first = (lane % QK_ROPE) < (QK_ROPE // 2)

    def rope(z):
        rot = jnp.where(first, -pltpu.roll(z, LANES - QK_ROPE // 2, 1), pltpu.roll(z, QK_ROPE // 2, 1))
        return z * cos + rot * sin

    def rms64(z):
        zz = z * z
        s_lo = jnp.sum(jnp.where(low_half, zz, 0.0), axis=-1, keepdims=True)
        s_hi = jnp.sum(jnp.where(low_half, 0.0, zz), axis=-1, keepdims=True)
        return lax.rsqrt(jnp.where(low_half, s_lo, s_hi) * (1.0 / QK_ROPE) + EPS)

    ckvn = ckv * lax.rsqrt(jnp.mean(ckv * ckv, axis=-1, keepdims=True) + EPS) * kvg_ref[...]
    ckvn_ref[...] = ckvn.reshape(bb, tm, KV_RANK)
    ckvn16 = ckvn.astype(BF16)
    kr = rope(sm * rms64(sm) * krg_ref[...])
    krn_ref[...] = kr[:, :QK_ROPE].reshape(bb, tm, QK_ROPE)
    kr_pad16 = jnp.where(low_half, kr, 0.0).astype(BF16).reshape(bb, tm, LANES)

    project(w_za, za_ref)
    k_raw = _dot(ckvn16, wuk_ref[...])
    v_ref[...] = _dot(ckvn16, wuv_ref[...]).astype(BF16).reshape(bb, tm, MLA_V)
    project(w_zb, zb_ref)

    qn = _head_rms(qn_all, qng_ref[...] * qscale, QK_NOPE)
    for c in range(MLA_HEADS // 2):
        z = qr_all[:, c * LANES:(c + 1) * LANES]
        z = rope(z * rms64(z) * qrg_ref[...]) * qscale
        even = jnp.where(low_half, z, 0.0)
        odd = jnp.where(low_half, pltpu.roll(z, QK_ROPE, 1), 0.0)
        for hh, part in ((2 * c, even), (2 * c + 1, odd)):
            qcat_ref[:, :, hh * QK_CAT:hh * QK_CAT + QK_NOPE] = qn[hh].astype(BF16).reshape(bb, tm, QK_NOPE)
            qcat_ref[:, :, hh * QK_CAT + QK_NOPE:(hh + 1) * QK_CAT] = part.astype(BF16).reshape(bb, tm, LANES)
    project(w_ga, ga_ref)

    kn = _head_rms(k_raw, kng_ref[...], QK_NOPE)
    for hh in range(MLA_HEADS):
        kcat_ref[:, :, hh * QK_CAT:hh * QK_CAT + QK_NOPE] = kn[hh].astype(BF16).reshape(bb, tm, QK_NOPE)
        kcat_ref[:, :, hh * QK_CAT + QK_NOPE:(hh + 1) * QK_CAT] = kr_pad16
    project(w_gb, gb_ref)


def _in_proj(x, mod3, gain, w_pack, mla, bb, tm, q_off):
    B, T, _ = x.shape
    row = lambda b, t: (b, t, 0)
    outs = ((LANES, F32), (DN_CONV_CH, BF16), (DN_V, BF16), (MLA_V, BF16), (D_MODEL, BF16), (D_MODEL, BF16),
            (MLA_HEADS * QK_CAT, BF16), (MLA_HEADS * QK_CAT, BF16), (MLA_V, BF16), (KV_RANK, F32), (QK_ROPE, F32))
    return pl.pallas_call(
        functools.partial(_in_proj_kernel, q_off=q_off),
        grid=(B // bb, T // tm),
        in_specs=[pl.BlockSpec((bb, tm, D_MODEL), row),
                  pl.BlockSpec((bb, 1, 3 * D_MODEL), lambda b, t: (b, 0, 0)),
                  _resident((1, D_MODEL))]
        + [_resident((D_MODEL, w)) for w in _PROJ_WIDTHS]
        + [_resident((1, QK_NOPE)), _resident((1, LANES)), _resident((1, KV_RANK)), _resident((1, LANES)),
           _resident((1, QK_NOPE)), _resident((KV_RANK, MLA_HEADS * QK_NOPE)), _resident((KV_RANK, MLA_V))],
        out_specs=[pl.BlockSpec((bb, tm, w), row) for w, _ in outs],
        out_shape=[jax.ShapeDtypeStruct((B, T, w), dt) for w, dt in outs],
        scratch_shapes=[pltpu.VMEM((2, bb * tm, LANES), F32)],
        compiler_params=_cparams(("arbitrary", "arbitrary")),
        name="in_proj",
    )(x, mod3, gain, *w_pack, *mla)


def _softplus(x):
    return jnp.maximum(x, 0.0) + jnp.log(1.0 + jnp.exp(-jnp.abs(x)))


def _dn_prep_kernel(qkv_ref, prev_ref, cs_ref, small_ref, wconv_ref, alog_ref, dtb_ref,
                    w_ref, u_ref, qe_ref, kd_ref, attn_ref, egl_ref, l_scr, rhs_scr, *, C, tiles_per_seq):
    step = pl.program_id(0)
    tile_idx = jnp.minimum(step, pl.num_programs(0) - 2) % tiles_per_seq
    tm = qkv_ref.shape[1]
    nc = tm // C
    pad = prev_ref.shape[1]

    wr = step % 2
    rd = 1 - wr

    @pl.when(step == 0)
    def _():
        l_scr[...] = jnp.zeros_like(l_scr)
        rhs_scr[...] = jnp.zeros_like(rhs_scr)

    x16 = qkv_ref[0]
    hist = jnp.where(tile_idx == 0, cs_ref[0], prev_ref[0].astype(F32))
    hist_hi = hist.astype(BF16)
    rem = hist - hist_hi.astype(F32)
    hist_mid = rem.astype(BF16)
    hist_lo = (rem - hist_mid.astype(F32)).astype(BF16)
    n_sh = CONV_W - 1
    rs = min(tm, CONV_ROWS)

    def shifted_taps(pieces, xs):
        npc = len(pieces)
        full16 = jnp.concatenate(list(pieces) + [xs], axis=0)
        srow = lax.broadcasted_iota(jnp.int32, (n_sh * rs, npc * pad + rs), 0)
        scol = lax.broadcasted_iota(jnp.int32, (n_sh * rs, npc * pad + rs), 1)
        src = srow % rs + srow // rs + pad - n_sh
        sel = scol == src + (npc - 1) * pad
        for p in range(npc - 1):
            sel = sel | ((scol == src + p * pad) & (scol < (p + 1) * pad))
        return _dot(sel.astype(BF16), full16)

    conv_parts = []
    for j in range(tm // rs):
        xs = x16[j * rs:(j + 1) * rs]
        pieces = (hist_hi, hist_mid, hist_lo) if j == 0 else (x16[j * rs - pad:j * rs],)
        shifted = shifted_taps(pieces, xs)
        part = xs.astype(F32) * wconv_ref[n_sh:CONV_W, :]
        for i in range(n_sh):
            part = part + shifted[i * rs:(i + 1) * rs, :] * wconv_ref[i:i + 1, :]
        conv_parts.append(part)
    conv = conv_parts[0] if len(conv_parts) == 1 else jnp.concatenate(conv_parts, axis=0)
    act = conv * jax.nn.sigmoid(conv)

    sm = small_ref[0]
    beta_all = jax.nn.sigmoid(sm)
    g_all = -jnp.exp(alog_ref[...]) * _softplus(sm + dtb_ref[...])
    rt = lax.broadcasted_iota(jnp.int32, (tm, tm), 0)
    ct = lax.broadcasted_iota(jnp.int32, (tm, tm), 1)
    chunk_tri = ((rt // C == ct // C) & (rt >= ct)).astype(F32)
    gcum = jnp.dot(chunk_tri, g_all, preferred_element_type=F32, precision=HI)
    sel = (lax.broadcasted_iota(jnp.int32, (DN_HEADS, LANES), 1)
           == lax.broadcasted_iota(jnp.int32, (DN_HEADS, LANES), 0) + ALPHA_OFF).astype(F32)
    gcum_t = _dot_nt(sel, gcum, precision=HI)

    def per_head_lanes(x, off):
        hi = x.astype(BF16)
        r1 = x - hi.astype(F32)
        mid = r1.astype(BF16)
        lo = (r1 - mid.astype(F32)).astype(BF16)
        erow = lax.broadcasted_iota(jnp.int32, (3 * LANES, DN_QK), 0) % LANES
        ecol = lax.broadcasted_iota(jnp.int32, (3 * LANES, DN_QK), 1) // DN_DK
        return _dot(jnp.concatenate([hi, mid, lo], axis=1), (erow == ecol + off).astype(BF16))

    g_b = per_head_lanes(gcum, ALPHA_OFF)
    beta_b = per_head_lanes(beta_all, BETA_OFF)
    glast_b = jnp.concatenate(
        [jnp.broadcast_to(g_b[c * C + C - 1:(c + 1) * C, :], (C, DN_QK)) for c in range(nc)], axis=0)
    eg_b = jnp.exp(g_b)
    kdf_b = jnp.exp(glast_b - g_b)
    for c in range(nc):
        egl_ref[0, c] = jnp.exp(g_b[c * C + C - 1:(c + 1) * C, :])

    ri = lax.broadcasted_iota(jnp.int32, (C, C), 0)
    ci = lax.broadcasted_iota(jnp.int32, (C, C), 1)
    tri_incl = ri >= ci
    tri_strict = ri > ci
    eye = (ri == ci).astype(F32)
    pair_masks = []
    m = 1
    while m < C:
        pair_masks.append((ri // (2 * m) == ci // (2 * m)) & (ri // m != ci // m))
        m *= 2

    heads = range(DN_HEADS)
    hl = lambda h: slice(h * DN_DK, (h + 1) * DN_DK)
    items = [(c, h) for c in range(nc) for h in heads]
    rows = lambda c: slice(c * C, (c + 1) * C)

    l_prev = [l_scr[rd, i] for i in range(len(items))]
    pinv = [eye - jnp.where(pair_masks[0], l, 0.0) for l in l_prev]
    for mask in pair_masks[1:]:
        p16 = [p.astype(BF16) for p in pinv]
        tmp = [_dot(p16[i], jnp.where(mask, l_prev[i], 0.0).astype(BF16)).astype(BF16) for i in range(len(items))]
        pinv = [pinv[i] - _dot(tmp[i], p16[i]) for i in range(len(items))]
    for i, (c, h) in enumerate(items):
        wu = _dot(pinv[i].astype(BF16), rhs_scr[rd, i])
        w_ref[0, c, :, hl(h)] = wu[:, :DN_DK].astype(BF16)
        u_ref[0, rows(c), hl(h)] = wu[:, DN_DK:]

    qn, kn = [], []
    for h in heads:
        qh = act[:, h * DN_DK:(h + 1) * DN_DK]
        kh = act[:, DN_QK + h * DN_DK:DN_QK + (h + 1) * DN_DK]
        qn.append(qh * lax.rsqrt(jnp.sum(qh * qh, axis=-1, keepdims=True) + EPS) * (DN_DK ** -0.5))
        kn.append(kh * lax.rsqrt(jnp.sum(kh * kh, axis=-1, keepdims=True) + EPS))

    k16 = [kn[h].astype(BF16) for h in heads]
    q16 = [qn[h].astype(BF16) for h in heads]
    kb = [kn[h] * beta_b[:, hl(h)] for h in heads]
    kb16 = [kb[h].astype(BF16) for h in heads]
    kbe16 = [(kb[h] * eg_b[:, hl(h)]).astype(BF16) for h in heads]
    vb16 = [(act[:, 2 * DN_QK + h * DN_DV:2 * DN_QK + (h + 1) * DN_DV] * beta_b[:, hl(h)]).astype(BF16)
            for h in heads]
    qe16 = [(qn[h] * eg_b[:, hl(h)]).astype(BF16) for h in heads]
    for h in heads:
        kd_ref[0, :, hl(h)] = (kn[h] * kdf_b[:, hl(h)]).astype(BF16)

    attn_ref[...] = jnp.zeros_like(attn_ref)
    decay, qk = [], []
    for i, (c, h) in enumerate(items):
        r = rows(c)
        gc = g_b[r, h * DN_DK:h * DN_DK + C]
        decay.append(jnp.exp(jnp.where(tri_incl, gc - gcum_t[h:h + 1, r], -1e30)))
        rhs_scr[wr, i] = jnp.concatenate([kbe16[h][r], vb16[h][r]], axis=1)
        qe_ref[0, c, :, hl(h)] = qe16[h][r]
        qk.append(_dot_nt(jnp.concatenate([kb16[h][r], q16[h][r]], axis=0), k16[h][r]))

    for i, (c, h) in enumerate(items):
        l_scr[wr, i] = jnp.where(tri_strict, qk[i][:C] * decay[i], 0.0)
        attn_ref[0, rows(c), h * DN_DK:h * DN_DK + C] = (qk[i][C:] * decay[i]).astype(BF16)


def _dn_scan_kernel(w_ref, qe_ref, u_ref, kd_ref, attn_ref, egl_ref, za_ref, s0_ref, onorm_ref,
                    ua_ref, sfin_ref, s_ref, *, C):
    n = pl.program_id(1)
    bg, G = w_ref.shape[0], w_ref.shape[1]

    @pl.when(n == 0)
    def _():
        s_ref[...] = s0_ref[...]

    chains = [(b, h) for b in range(bg) for h in range(DN_HEADS)]
    for g in range(G):
        r = slice(g * C, (g + 1) * C)
        s_old = [s_ref[b, h] for b, h in chains]
        s16 = [s.astype(BF16) for s in s_old]
        ws = [_dot(jnp.concatenate([w_ref[b, g, :, h * DN_DK:(h + 1) * DN_DK],
                                    qe_ref[b, g, :, h * DN_DK:(h + 1) * DN_DK]], axis=0), s16[i])
              for i, (b, h) in enumerate(chains)]
        v16 = [(u_ref[b, r, h * DN_DV:(h + 1) * DN_DV] - ws[i][:C]).astype(BF16)
               for i, (b, h) in enumerate(chains)]
        for i, (b, h) in enumerate(chains):
            lo = h * DN_DK
            s_ref[b, h] = s_old[i] * egl_ref[b, g, :, lo:lo + DN_DK] + _dot_tn(kd_ref[b, r, lo:lo + DN_DK], v16[i])
        for i, (b, h) in enumerate(chains):
            lo = h * DN_DV
            o = ws[i][C:] + _dot(attn_ref[b, r, lo:lo + C], v16[i])
            o = o * lax.rsqrt(jnp.mean(o * o, axis=-1, keepdims=True) + EPS) * onorm_ref[...]
            z = za_ref[b, r, lo:lo + DN_DV].astype(F32)
            ua_ref[b, r, lo:lo + DN_DV] = (o * (z * jax.nn.sigmoid(z))).astype(BF16)

    @pl.when(n == pl.num_programs(1) - 1)
    def _():
        sfin_ref[...] = s_ref[...]


def _deltanet(qkv, small, z_a, conv_state, s0, w_conv, alog_v, dtb_v, onorm, tm, bg, G):
    B, T, _ = qkv.shape
    C = min(CHUNK, T)
    N = T // C
    nc = tm // C
    hist_rows = 16
    cs = jnp.pad(conv_state, ((0, 0), (hist_rows - (CONV_W - 1), 0), (0, 0)))
    nt = T // tm
    n_tiles = B * nt

    def cur(s):
        s = jnp.minimum(s, n_tiles - 1)
        return s // nt, s % nt

    def done(s):
        s = jnp.maximum(s - 1, 0)
        return s // nt, s % nt

    tile = lambda s: (*cur(s), 0)
    tile4 = lambda s: (*cur(s), 0, 0)
    prev = lambda s: (cur(s)[0], jnp.maximum(cur(s)[1] * (tm // hist_rows) - 1, 0), 0)
    w, u, qe, kd, attn, egl = pl.pallas_call(
        functools.partial(_dn_prep_kernel, C=C, tiles_per_seq=nt),
        grid=(n_tiles + 1,),
        in_specs=[pl.BlockSpec((1, tm, DN_CONV_CH), tile),
                  pl.BlockSpec((1, hist_rows, DN_CONV_CH), prev),
                  pl.BlockSpec((1, hist_rows, DN_CONV_CH), lambda s: (cur(s)[0], 0, 0)),
                  pl.BlockSpec((1, tm, LANES), tile),
                  _resident((CONV_W, DN_CONV_CH)),
                  _resident((1, LANES)),
                  _resident((1, LANES))],
        out_specs=[pl.BlockSpec((1, nc, C, DN_QK), lambda s: (*done(s), 0, 0)),
                   pl.BlockSpec((1, tm, DN_V), lambda s: (*done(s), 0)),
                   pl.BlockSpec((1, nc, C, DN_QK), tile4),
                   pl.BlockSpec((1, tm, DN_QK), tile),
                   pl.BlockSpec((1, tm, DN_V), tile),
                   pl.BlockSpec((1, nc, 1, DN_QK), tile4)],
        out_shape=[jax.ShapeDtypeStruct((B, N, C, DN_QK), BF16),
                   jax.ShapeDtypeStruct((B, T, DN_V), F32),
                   jax.ShapeDtypeStruct((B, N, C, DN_QK), BF16),
                   jax.ShapeDtypeStruct((B, T, DN_QK), BF16),
                   jax.ShapeDtypeStruct((B, T, DN_V), BF16),
                   jax.ShapeDtypeStruct((B, N, 1, DN_QK), F32)],
        scratch_shapes=[pltpu.VMEM((2, nc * DN_HEADS, C, C), F32),
                        pltpu.VMEM((2, nc * DN_HEADS, C, DN_DK + DN_DV), BF16)],
        compiler_params=_cparams(("arbitrary",)),
        name="dn_prep",
    )(qkv, qkv, cs, small, w_conv, alog_v, dtb_v)

    grp = lambda b, n: (b, n, 0)
    grp4 = lambda b, n: (b, n, 0, 0)
    state = pl.BlockSpec((bg, DN_HEADS, DN_DK, DN_DV), lambda b, n: (b, 0, 0, 0))
    u_a, s_new = pl.pallas_call(
        functools.partial(_dn_scan_kernel, C=C),
        grid=(B // bg, N // G),
        in_specs=[pl.BlockSpec((bg, G, C, DN_QK), grp4),
                  pl.BlockSpec((bg, G, C, DN_QK), grp4),
                  pl.BlockSpec((bg, G * C, DN_V), grp),
                  pl.BlockSpec((bg, G * C, DN_QK), grp),
                  pl.BlockSpec((bg, G * C, DN_V), grp),
                  pl.BlockSpec((bg, G, 1, DN_QK), grp4),
                  pl.BlockSpec((bg, G * C, DN_V), grp),
                  state,
                  _resident((1, DN_DV))],
        out_specs=[pl.BlockSpec((bg, G * C, DN_V), grp), state],
        out_shape=[jax.ShapeDtypeStruct((B, T, DN_V), BF16),
                   jax.ShapeDtypeStruct((B, DN_HEADS, DN_DK, DN_DV), F32)],
        scratch_shapes=[pltpu.VMEM((bg, DN_HEADS, DN_DK, DN_DV), F32)],
        compiler_params=_cparams(("arbitrary", "arbitrary")),
        name="dn_scan",
    )(w, qe, u, kd, attn, egl, z_a, s0, onorm)
    conv_new = qkv[:, T - (CONV_W - 1):, :].astype(F32)
    return u_a, s_new, conv_new


def _head_rms(x, gain_row, width):
    outs = []
    for h in range(x.shape[1] // width):
        xh = x[:, h * width:(h + 1) * width]
        outs.append(xh * lax.rsqrt(jnp.mean(xh * xh, axis=-1, keepdims=True) + EPS) * gain_row)
    return outs


def _rope_inv_freq(shape):
    half = QK_ROPE // 2
    fidx = (lax.broadcasted_iota(jnp.int32, shape, 1) % half).astype(F32)
    return jnp.exp(fidx * (-math.log(ROPE_THETA) / half))


def _chunk_mask(qpos0, kpos0, tq, tk):
    qc = (lax.broadcasted_iota(jnp.int32, (tq, tk), 0) + qpos0) // CHUNK
    kc = (lax.broadcasted_iota(jnp.int32, (tq, tk), 1) + kpos0) // CHUNK
    return kc <= qc


def _attn_prompt_kernel(q_ref, k_ref, v_ref, o_ref, m_ref, l_ref, acc_ref, *, tq, tk, sub):
    i = pl.program_id(2)
    nsub = tq // sub
    ratio = tq // tk
    m_ref[...] = jnp.full_like(m_ref, -1e30)
    l_ref[...] = jnp.zeros_like(l_ref)
    acc_ref[...] = jnp.zeros_like(acc_ref)

    def scores(r, k):
        return _dot_nt(q_ref[0, r * sub:(r + 1) * sub, :], k)

    def softmax_pv(r, sr, v, mask):
        rows = slice(r * sub, (r + 1) * sub)
        if mask is not None:
            sr = jnp.where(mask, sr, -1e30)
        m_old = m_ref[rows, :]
        m_new = jnp.maximum(m_old, jnp.max(sr, axis=-1, keepdims=True))
        alpha = jnp.exp2(m_old - m_new)
        p = jnp.exp2(sr - jnp.tile(m_new, (1, tk // LANES)))
        psum = p[:, 0:LANES]
        for c in range(1, tk // LANES):
            psum = psum + p[:, c * LANES:(c + 1) * LANES]
        l_ref[rows, :] = alpha * l_ref[rows, :] + psum
        acc_ref[rows, :] = alpha * acc_ref[rows, :] + _dot(p.astype(BF16), v)
        m_ref[rows, :] = m_new

    def run(j0, items):
        kv = {}
        for d in sorted({d for d, _, _ in items}):
            start = pl.multiple_of((j0 + d) * tk, tk)
            kv[d] = (k_ref[0, pl.ds(start, tk), :], v_ref[0, pl.ds(start, tk), :])
        s = {n: scores(items[n][1], kv[items[n][0]][0]) for n in range(min(ATTN_LOOKAHEAD, len(items)))}
        for n, (d, r, mask) in enumerate(items):
            ahead = n + ATTN_LOOKAHEAD
            if ahead < len(items):
                s[ahead] = scores(items[ahead][1], kv[items[ahead][0]][0])
            softmax_pv(r, s.pop(n), kv[d][1], mask)

    per_trip = math.gcd(ratio, ATTN_TILES_PER_TRIP)
    full = [(d, r, None) for d in range(per_trip) for r in range(nsub)]

    def body(jj, carry):
        run(jj * per_trip, full)
        return carry

    lax.fori_loop(0, i * (ratio // per_trip), body, 0)
    diag = []
    for d in range(ratio):
        for r in range(nsub):
            q_lo, q_hi = (r * sub) // CHUNK, (r * sub + sub - 1) // CHUNK
            k_lo, k_hi = (d * tk) // CHUNK, (d * tk + tk - 1) // CHUNK
            if k_lo > q_hi:
                continue
            diag.append((d, r, None if k_hi <= q_lo else _chunk_mask(r * sub, d * tk, sub, tk)))
    run(i * ratio, diag)
    l = jnp.sum(l_ref[...], axis=-1, keepdims=True)
    o_ref[0] = (acc_ref[...] / l).astype(BF16)


def _attn_prompt(qcat, kcat, v, tq, tk):
    B, T, _ = v.shape
    return pl.pallas_call(
        functools.partial(_attn_prompt_kernel, tq=tq, tk=tk, sub=ATTN_SUB),
        grid=(B, MLA_HEADS, T // tq),
        in_specs=[pl.BlockSpec((1, tq, QK_CAT), lambda b, h, i: (b, i, h)),
                  pl.BlockSpec((1, T, QK_CAT), lambda b, h, i: (b, 0, h)),
                  pl.BlockSpec((1, T, V_HEAD), lambda b, h, i: (b, 0, h))],
        out_specs=pl.BlockSpec((1, tq, V_HEAD), lambda b, h, i: (b, i, h)),
        out_shape=jax.ShapeDtypeStruct((B, T, MLA_V), BF16),
        scratch_shapes=[pltpu.VMEM((tq, LANES), F32), pltpu.VMEM((tq, LANES), F32),
                        pltpu.VMEM((tq, V_HEAD), F32)],
        compiler_params=_cparams(("arbitrary", "arbitrary", "arbitrary")),
        name="attn_prompt",
    )(qcat, kcat, v)


def _attn_sample_kernel(q_ref, kn_ref, vn_ref, ckv_ref, kr_ref, kng_ref, wuk_ref, wuv_ref, o_ref,
                        m_ref, l_ref, acc_ref, *, tk):
    T = q_ref.shape[1]
    P = ckv_ref.shape[1]
    R = MLA_HEADS * T
    q = q_ref[0]
    row_head = lax.broadcasted_iota(jnp.int32, (R, 1), 0) // T
    q_chunk = (lax.broadcasted_iota(jnp.int32, (R, 1), 0) % T + P) // CHUNK

    def stacked(x, width):
        lane_head = lax.broadcasted_iota(jnp.int32, (R, x.shape[1]), 1) // width
        return jnp.where(lane_head == row_head, jnp.concatenate([x] * MLA_HEADS, axis=0), jnp.zeros((), x.dtype))

    q_nope = stacked(jnp.concatenate([q[:, h * QK_CAT:h * QK_CAT + QK_NOPE] for h in range(MLA_HEADS)], axis=1),
                     QK_NOPE)
    q_rope = jnp.concatenate([q[:, h * QK_CAT + QK_NOPE:h * QK_CAT + QK_HEAD] for h in range(MLA_HEADS)],
                             axis=0)

    def update(s, k_chunk):
        s = jnp.where(k_chunk <= q_chunk, s, -1e30)
        m_old = m_ref[...]
        m_new = jnp.maximum(m_old, jnp.max(s, axis=-1, keepdims=True))
        alpha = jnp.exp2(m_old - m_new)
        p = jnp.exp2(s - m_new)
        l_ref[...] = alpha * l_ref[...] + jnp.sum(p, axis=-1, keepdims=True)
        m_ref[...] = m_new
        return alpha, p.astype(BF16)

    m_ref[...] = jnp.full_like(m_ref, -1e30)
    l_ref[...] = jnp.zeros_like(l_ref)
    acc_ref[...] = jnp.zeros_like(acc_ref)

    def body(c, carry):
        start = pl.multiple_of(c * tk, tk)
        ckv16 = ckv_ref[0, pl.ds(start, tk), :].astype(BF16)
        kn = _head_rms(_dot(ckv16, wuk_ref[...]), kng_ref[...], QK_NOPE)
        kn16 = jnp.concatenate([x.astype(BF16) for x in kn], axis=1)
        s = _dot_nt(q_nope, kn16) + _dot_nt(q_rope, kr_ref[0, pl.ds(start, tk), :].astype(BF16))
        k_chunk = (lax.broadcasted_iota(jnp.int32, (1, tk), 1) + start) // CHUNK
        alpha, p16 = update(s, k_chunk)
        acc_ref[...] = alpha * acc_ref[...] + _dot(p16, ckv16)
        return carry

    lax.fori_loop(0, P // tk, body, 0)

    s_new = _dot_nt(stacked(q, QK_CAT), kn_ref[0])
    alpha, p16 = update(s_new, (lax.broadcasted_iota(jnp.int32, (1, T), 1) + P) // CHUNK)
    pc16 = (alpha * acc_ref[...]).astype(BF16)
    inv_l = 1.0 / l_ref[...]
    for h in range(MLA_HEADS):
        rows = slice(h * T, (h + 1) * T)
        lanes = slice(h * V_HEAD, (h + 1) * V_HEAD)
        o = _dot(pc16[rows], wuv_ref[:, lanes]) + _dot(p16[rows], vn_ref[0, :, lanes])
        o_ref[0, :, lanes] = (o * inv_l[rows]).astype(BF16)


def _attn_sample(qcat, kcat_new, v_new, past_ckv, past_kr, kn_gain, w_uk16, w_uv16, tk):
    B, T, _ = v_new.shape
    P = past_ckv.shape[1]
    R = MLA_HEADS * T
    perb = lambda b: (b, 0, 0)
    return pl.pallas_call(
        functools.partial(_attn_sample_kernel, tk=tk),
        grid=(B,),
        in_specs=[pl.BlockSpec((1, T, MLA_HEADS * QK_CAT), perb),
                  pl.BlockSpec((1, T, MLA_HEADS * QK_CAT), perb),
                  pl.BlockSpec((1, T, MLA_V), perb),
                  pl.BlockSpec((1, P, KV_RANK), perb),
                  pl.BlockSpec((1, P, QK_ROPE), perb),
                  _resident((1, QK_NOPE)),
                  _resident((KV_RANK, MLA_HEADS * QK_NOPE)),
                  _resident((KV_RANK, MLA_V))],
        out_specs=pl.BlockSpec((1, T, MLA_V), perb),
        out_shape=jax.ShapeDtypeStruct((B, T, MLA_V), BF16),
        scratch_shapes=[pltpu.VMEM((R, 1), F32), pltpu.VMEM((R, 1), F32), pltpu.VMEM((R, KV_RANK), F32)],
        compiler_params=_cparams(("arbitrary",)),
        name="attn_sample",
    )(qcat, kcat_new, v_new, past_ckv, past_kr, kn_gain, w_uk16, w_uv16)


def _out_kernel(x_ref, mod_ref, ua_ref, ob_ref, zb_ref, ga_ref, gb_ref, wdn_ref, wmla_ref, wout_ref, y_ref):
    bb, tm, d = x_ref.shape
    rows = bb * tm
    zb = zb_ref[...].astype(F32)
    ub = (ob_ref[...].astype(F32) * (zb * jax.nn.sigmoid(zb))).astype(BF16).reshape(rows, d)
    ya = _dot(ua_ref[...].reshape(rows, d), wdn_ref[...])
    yb = _dot(ub, wmla_ref[...])
    ga = jax.nn.sigmoid(ga_ref[...].astype(F32)).reshape(rows, d)
    gb = jax.nn.sigmoid(gb_ref[...].astype(F32)).reshape(rows, d)
    merged = (ga * ya + gb * yb).astype(BF16)
    out = _dot(merged, wout_ref[...]).reshape(bb, tm, d)
    gate = mod_ref[:, :, 2 * d:3 * d]
    y_ref[...] = x_ref[...] + gate * out


def _out_proj(x, mod3, u_a, o_b, z_b, g_a, g_b, w_dn16, w_mla16, w_out16, bb, tm):
    B, T, _ = x.shape
    row = lambda b, t: (b, t, 0)
    act = pl.BlockSpec((bb, tm, D_MODEL), row)
    return pl.pallas_call(
        _out_kernel,
        grid=(B // bb, T // tm),
        in_specs=[act, pl.BlockSpec((bb, 1, 3 * D_MODEL), lambda b, t: (b, 0, 0)), act, act, act, act, act,
                  _resident((D_MODEL, D_MODEL)), _resident((D_MODEL, D_MODEL)), _resident((D_MODEL, D_MODEL))],
        out_specs=act,
        out_shape=jax.ShapeDtypeStruct((B, T, D_MODEL), F32),
        compiler_params=_cparams(("arbitrary", "arbitrary")),
        name="out_proj",
    )(x, mod3, u_a, o_b, z_b, g_a, g_b, w_dn16, w_mla16, w_out16)


def _lane_vec(v, off):
    return jnp.zeros((1, LANES), F32).at[0, off:off + v.shape[0]].set(v)


def _tiles(B, T, cached):
    if cached:
        whole = (B, T)
        return dict(proj=whole, out=whole, dn=dict(tm=T, bg=2, G=1), attn_tk=1024)
    return dict(proj=(1, 256), out=(1, 512), dn=dict(tm=256, bg=B, G=4), attn=(2048, 512))


def _layer(x, mod, conv_state, s0, past, prm, q_off):
    B, T, _ = x.shape
    tiles = _tiles(B, T, past is not None)
    mod3 = mod.reshape(B, 1, 3 * D_MODEL)
    mla = (prm["q_nope_norm"], prm["qr_gain"], prm["kv_norm"], prm["kr_gain"], prm["k_nope_norm"],
           prm["w_uk"], prm["w_uv"])
    small, qkv, z_a, z_b, g_a, g_b, qcat, kcat, v, ckv_new, kr_new = _in_proj(
        x, mod3, prm["norm_gain"], prm["w_pack"], mla, *tiles["proj"], q_off)
    u_a, s_new, conv_new = _deltanet(qkv, small, z_a, conv_state, s0, prm["w_conv"], prm["alog_v"],
                                     prm["dtb_v"], prm["dn_out_norm"], **tiles["dn"])
    if past is None:
        o_b = _attn_prompt(qcat, kcat, v, *tiles["attn"])
    else:
        past_ckv, past_kr = past
        o_b = _attn_sample(qcat, kcat, v, past_ckv, past_kr, prm["k_nope_norm"], prm["w_uk"], prm["w_uv"],
                           tiles["attn_tk"])
    y = _out_proj(x, mod3, u_a, o_b, z_b, g_a, g_b, prm["w_o_dn"], prm["w_o_mla"], prm["w_out"], *tiles["out"])
    return y, conv_new, s_new, ckv_new, kr_new


def kernel(x_prompt, x_sample, c_prompt, c_sample, cache_ckv, cache_krope, state_delta, state_conv, norm_gain, w_ada, b_ada, w_in, w_conv, a_log, dt_bias, dn_out_norm, q_nope_norm, q_rope_norm, k_nope_norm, k_rope_norm, kv_norm, w_uk, w_uv, w_o_dn, w_o_mla, w_out):
    depth = w_in.shape[0]
    assert depth == 1, "single-layer configuration"
    l = 0
    B, T, _ = x_prompt.shape
    Bs, Ts, _ = x_sample.shape
    past_len = cache_ckv.shape[2]

    row = lambda v: v.reshape(1, -1).astype(F32)
    prm = dict(
        norm_gain=row(norm_gain[l]),
        w_pack=_pack_w_in(w_in[l]),
        w_conv=w_conv[l],
        alog_v=_lane_vec(a_log[l], ALPHA_OFF),
        dtb_v=_lane_vec(dt_bias[l], ALPHA_OFF),
        dn_out_norm=row(dn_out_norm[l]),
        q_nope_norm=row(q_nope_norm[l]),
        qr_gain=jnp.tile(row(q_rope_norm[l]), (1, LANES // QK_ROPE)),
        kv_norm=row(kv_norm[l]),
        kr_gain=_lane_vec(k_rope_norm[l], KR_OFF),
        k_nope_norm=row(k_nope_norm[l]),
        w_uk=w_uk[l].astype(BF16),
        w_uv=w_uv[l].astype(BF16),
        w_o_dn=w_o_dn[l].astype(BF16),
        w_o_mla=w_o_mla[l].astype(BF16),
        w_out=w_out[l].astype(BF16),
    )

    rows = B + Bs
    rows_pad = -(-rows // 8) * 8
    c_all = jnp.concatenate([c_prompt, c_sample, jnp.zeros((rows_pad - rows, D_MODEL), F32)], axis=0)
    mod = _ada(c_all, w_ada[l], b_ada[l].reshape(1, -1))

    zeros_conv = jnp.zeros((B, CONV_W - 1, DN_CONV_CH), F32)
    zeros_state = jnp.zeros((B, DN_HEADS, DN_DK, DN_DV), F32)
    yp, cvp, sdp, kvp, krp = _layer(x_prompt, mod[:B], zeros_conv, zeros_state, None, prm, q_off=0)
    ys, cvs, sds, kvs, krs = _layer(x_sample, mod[B:rows], state_conv[l], state_delta[l],
                                    (cache_ckv[l], cache_krope[l]), prm, q_off=past_len)
    st = lambda a: a[None]
    return (yp, ys, st(kvp), st(krp), st(sdp), st(cvp), st(kvs), st(krs), st(sds), st(cvs))
```

```python
import functools
import math

import jax
import jax.numpy as jnp
from jax import lax
from jax.experimental import pallas as pl
from jax.experimental.pallas import tpu as pltpu

D_MODEL = 1024
CHUNK = 64
EPS = 1e-6
DN_HEADS = 8
DN_DK = 128
DN_DV = 128
DN_QK = DN_HEADS * DN_DK
DN_V = DN_HEADS * DN_DV
DN_CONV_CH = 2 * DN_QK + DN_V
CONV_W = 4
MLA_HEADS = 8
QK_NOPE = 128
QK_ROPE = 64
QK_HEAD = QK_NOPE + QK_ROPE
V_HEAD = 128
KV_RANK = 512
MLA_Q = MLA_HEADS * QK_HEAD
MLA_V = MLA_HEADS * V_HEAD
ROPE_THETA = 10000.0

LANES = 128
QK_CAT = 256
CONV_ROWS = 128
ATTN_SUB = 256
ATTN_LOOKAHEAD = 2
ATTN_TILES_PER_TRIP = 4
KR_OFF = 0
BETA_OFF = QK_ROPE
ALPHA_OFF = QK_ROPE + DN_HEADS
VMEM_LIMIT = 56 * 1024 * 1024

F32 = jnp.float32
BF16 = jnp.bfloat16
HI = lax.Precision.HIGHEST


def _dot(a, b):
    return jnp.dot(a, b, preferred_element_type=F32)


def _dot_nt(a, b, precision=None):
    return lax.dot_general(a, b, (((1,), (1,)), ((), ())), preferred_element_type=F32, precision=precision)


def _dot_tn(a, b):
    return lax.dot_general(a, b, (((0,), (0,)), ((), ())), preferred_element_type=F32)


def _cparams(sem):
    return pltpu.CompilerParams(dimension_semantics=sem, vmem_limit_bytes=VMEM_LIMIT)


def _resident(shape):
    nd = len(shape)
    return pl.BlockSpec(shape, lambda *_: (0,) * nd, pipeline_mode=pl.Buffered(1))


def _ada_kernel(c_ref, w_ref, b_ref, o_ref):
    o_ref[...] = jnp.dot(c_ref[...], w_ref[...], preferred_element_type=F32, precision=HI) + b_ref[...]


def _ada(c_all, w_ada, b_ada):
    rows = c_all.shape[0]
    tn = 512
    return pl.pallas_call(
        _ada_kernel,
        grid=(3 * D_MODEL // tn,),
        in_specs=[pl.BlockSpec((rows, D_MODEL), lambda j: (0, 0)),
                  pl.BlockSpec((D_MODEL, tn), lambda j: (0, j)),
                  pl.BlockSpec((1, tn), lambda j: (0, j))],
        out_specs=pl.BlockSpec((rows, tn), lambda j: (0, j)),
        out_shape=jax.ShapeDtypeStruct((rows, 3 * D_MODEL), F32),
        compiler_params=_cparams(("arbitrary",)),
        name="ada",
    )(c_all, w_ada, b_ada)


_PROJ_WIDTHS = (LANES, KV_RANK, MLA_HEADS * QK_ROPE, MLA_HEADS * QK_NOPE, DN_CONV_CH, DN_V, MLA_V, D_MODEL, D_MODEL)


def _pack_w_in(w_in):
    o = 0
    qkv = w_in[:, o:o + DN_CONV_CH]; o += DN_CONV_CH
    z_a = w_in[:, o:o + DN_V]; o += DN_V
    beta = w_in[:, o:o + DN_HEADS]; o += DN_HEADS
    alpha = w_in[:, o:o + DN_HEADS]; o += DN_HEADS
    q = w_in[:, o:o + MLA_Q].reshape(D_MODEL, MLA_HEADS, QK_HEAD); o += MLA_Q
    ckv = w_in[:, o:o + KV_RANK]; o += KV_RANK
    kr = w_in[:, o:o + QK_ROPE]; o += QK_ROPE
    z_b = w_in[:, o:o + MLA_V]; o += MLA_V
    g_a = w_in[:, o:o + D_MODEL]; o += D_MODEL
    g_b = w_in[:, o:o + D_MODEL]
    qn = q[:, :, :QK_NOPE].reshape(D_MODEL, MLA_HEADS * QK_NOPE)
    qr = q[:, :, QK_NOPE:].reshape(D_MODEL, MLA_HEADS * QK_ROPE)
    pad = jnp.zeros((D_MODEL, LANES - QK_ROPE - 2 * DN_HEADS), w_in.dtype)
    small = jnp.concatenate([kr, beta, alpha, pad], axis=1)
    return tuple(w.astype(BF16) for w in (small, ckv, qr, qn, qkv, z_a, z_b, g_a, g_b))


def _in_proj_kernel(x_ref, mod_ref, gain_ref,
                    w_small, w_ckv, w_qr, w_qn, w_qkv, w_za, w_zb, w_ga, w_gb,
                    qng_ref, qrg_ref, kvg_ref, krg_ref, kng_ref, wuk_ref, wuv_ref,
                    small_ref, qkv_ref, za_ref, zb_ref, ga_ref, gb_ref,
                    qcat_ref, kcat_ref, v_ref, ckvn_ref, krn_ref, rope_ref, *, q_off):
    bb, tm, d = x_ref.shape
    rows = bb * tm

    @pl.when((pl.program_id(0) == 0) & (pl.program_id(1) == 0))
    def _():
        off = (lax.broadcasted_iota(jnp.int32, (rows, LANES), 0) % tm).astype(F32) * _rope_inv_freq((rows, LANES))
        rope_ref[0] = jnp.cos(off)
        rope_ref[1] = jnp.sin(off)

    x = x_ref[...]
    ms = jnp.mean(x * x, axis=-1, keepdims=True)
    y = x * lax.rsqrt(ms + EPS) * gain_ref[...]
    shift = mod_ref[:, :, 0:d]
    scale = mod_ref[:, :, d:2 * d]
    h = (y * (1.0 + scale) + shift).astype(BF16).reshape(rows, d)

    def project(w_ref, o_ref):
        o_ref[...] = _dot(h, w_ref[...]).astype(o_ref.dtype).reshape(o_ref.shape)

    sm = _dot(h, w_small[...])
    small_ref[...] = sm.reshape(bb, tm, LANES)
    ckv = _dot(h, w_ckv[...])
    qr_all = _dot(h, w_qr[...])
    qn_all = _dot(h, w_qn[...])
    project(w_qkv, qkv_ref)

    t0 = pl.program_id(1) * tm
    qscale = QK_HEAD ** -0.5 * math.log2(math.e)
    base = (t0 + q_off).astype(F32) * _rope_inv_freq((1, LANES))
    cos_a, sin_a = jnp.cos(base), jnp.sin(base)
    cos = cos_a * rope_ref[0] - sin_a * rope_ref[1]
    sin = sin_a * rope_ref[0] + cos_a * rope_ref[1]
    lane = lax.broadcasted_iota(jnp.int32, (rows, LANES), 1)
    low_half = lane < QK_ROPE
    first = (lane % QK_ROPE) < (QK_ROPE // 2)

    def rope(z):
        rot = jnp.where(first, -pltpu.roll(z, LANES - QK_ROPE // 2, 1), pltpu.roll(z, QK_ROPE // 2, 1))
        return z * cos + rot * sin

    def rms64(z):
        zz = z * z
        s_lo = jnp.sum(jnp.where(low_half, zz, 0.0), axis=-1, keepdims=True)
        s_hi = jnp.sum(jnp.where(low_half, 0.0, zz), axis=-1, keepdims=True)
        return lax.rsqrt(jnp.where(low_half, s_lo, s_hi) * (1.0 / QK_ROPE) + EPS)

    ckvn = ckv * lax.rsqrt(jnp.mean(ckv * ckv, axis=-1, keepdims=True) + EPS) * kvg_ref[...]
    ckvn_ref[...] = ckvn.reshape(bb, tm, KV_RANK)
    ckvn16 = ckvn.astype(BF16)
    kr = rope(sm * rms64(sm) * krg_ref[...])
    krn_ref[...] = kr[:, :QK_ROPE].reshape(bb, tm, QK_ROPE)
    kr_pad16 = jnp.where(low_half, kr, 0.0).astype(BF16).reshape(bb, tm, LANES)

    project(w_za, za_ref)
    k_raw = _dot(ckvn16, wuk_ref[...])
    v_ref[...] = _dot(ckvn16, wuv_ref[...]).astype(BF16).reshape(bb, tm, MLA_V)
    project(w_zb, zb_ref)

    qn = _head_rms(qn_all, qng_ref[...] * qscale, QK_NOPE)
    for c in range(MLA_HEADS // 2):
        z = qr_all[:, c * LANES:(c + 1) * LANES]
        z = rope(z * rms64(z) * qrg_ref[...]) * qscale
        even = jnp.where(low_half, z, 0.0)
        odd = jnp.where(low_half, pltpu.roll(z, QK_ROPE, 1), 0.0)
        for hh, part in ((2 * c, even), (2 * c + 1, odd)):
            qcat_ref[:, :, hh * QK_CAT:hh * QK_CAT + QK_NOPE] = qn[hh].astype(BF16).reshape(bb, tm, QK_NOPE)
            qcat_ref[:, :, hh * QK_CAT + QK_NOPE:(hh + 1) * QK_CAT] = part.astype(BF16).reshape(bb, tm, LANES)
    project(w_ga, ga_ref)

    kn = _head_rms(k_raw, kng_ref[...], QK_NOPE)
    for hh in range(MLA_HEADS):
        kcat_ref[:, :, hh * QK_CAT:hh * QK_CAT + QK_NOPE] = kn[hh].astype(BF16).reshape(bb, tm, QK_NOPE)
        kcat_ref[:, :, hh * QK_CAT + QK_NOPE:(hh + 1) * QK_CAT] = kr_pad16
    project(w_gb, gb_ref)


def _in_proj(x, mod3, gain, w_pack, mla, bb, tm, q_off):
    B, T, _ = x.shape
    row = lambda b, t: (b, t, 0)
    outs = ((LANES, F32), (DN_CONV_CH, BF16), (DN_V, BF16), (MLA_V, BF16), (D_MODEL, BF16), (D_MODEL, BF16),
            (MLA_HEADS * QK_CAT, BF16), (MLA_HEADS * QK_CAT, BF16), (MLA_V, BF16), (KV_RANK, F32), (QK_ROPE, F32))
    return pl.pallas_call(
        functools.partial(_in_proj_kernel, q_off=q_off),
        grid=(B // bb, T // tm),
        in_specs=[pl.BlockSpec((bb, tm, D_MODEL), row),
                  pl.BlockSpec((bb, 1, 3 * D_MODEL), lambda b, t: (b, 0, 0)),
                  _resident((1, D_MODEL))]
        + [_resident((D_MODEL, w)) for w in _PROJ_WIDTHS]
        + [_resident((1, QK_NOPE)), _resident((1, LANES)), _resident((1, KV_RANK)), _resident((1, LANES)),
           _resident((1, QK_NOPE)), _resident((KV_RANK, MLA_HEADS * QK_NOPE)), _resident((KV_RANK, MLA_V))],
        out_specs=[pl.BlockSpec((bb, tm, w), row) for w, _ in outs],
        out_shape=[jax.ShapeDtypeStruct((B, T, w), dt) for w, dt in outs],
        scratch_shapes=[pltpu.VMEM((2, bb * tm, LANES), F32)],
        compiler_params=_cparams(("arbitrary", "arbitrary")),
        name="in_proj",
    )(x, mod3, gain, *w_pack, *mla)


def _softplus(x):
    return jnp.maximum(x, 0.0) + jnp.log(1.0 + jnp.exp(-jnp.abs(x)))


def _dn_prep_kernel(qkv_ref, prev_ref, cs_ref, small_ref, wconv_ref, alog_ref, dtb_ref,
                    w_ref, u_ref, qe_ref, kd_ref, attn_ref, egl_ref, l_scr, rhs_scr, *, C, tiles_per_seq):
    step = pl.program_id(0)
    tile_idx = jnp.minimum(step, pl.num_programs(0) - 2) % tiles_per_seq
    tm = qkv_ref.shape[1]
    nc = tm // C
    pad = prev_ref.shape[1]

    wr = step % 2
    rd = 1 - wr

    @pl.when(step == 0)
    def _():
        l_scr[...] = jnp.zeros_like(l_scr)
        rhs_scr[...] = jnp.zeros_like(rhs_scr)

    x16 = qkv_ref[0]
    hist = jnp.where(tile_idx == 0, cs_ref[0], prev_ref[0].astype(F32))
    hist_hi = hist.astype(BF16)
    rem = hist - hist_hi.astype(F32)
    hist_mid = rem.astype(BF16)
    hist_lo = (rem - hist_mid.astype(F32)).astype(BF16)
    n_sh = CONV_W - 1
    rs = min(tm, CONV_ROWS)

    def shifted_taps(pieces, xs):
        npc = len(pieces)
        full16 = jnp.concatenate(list(pieces) + [xs], axis=0)
        srow = lax.broadcasted_iota(jnp.int32, (n_sh * rs, npc * pad + rs), 0)
        scol = lax.broadcasted_iota(jnp.int32, (n_sh * rs, npc * pad + rs), 1)
        src = srow % rs + srow // rs + pad - n_sh
        sel = scol == src + (npc - 1) * pad
        for p in range(npc - 1):
            sel = sel | ((scol == src + p * pad) & (scol < (p + 1) * pad))
        return _dot(sel.astype(BF16), full16)

    conv_parts = []
    for j in range(tm // rs):
        xs = x16[j * rs:(j + 1) * rs]
        pieces = (hist_hi, hist_mid, hist_lo) if j == 0 else (x16[j * rs - pad:j * rs],)
        shifted = shifted_taps(pieces, xs)
        part = xs.astype(F32) * wconv_ref[n_sh:CONV_W, :]
        for i in range(n_sh):
            part = part + shifted[i * rs:(i + 1) * rs, :] * wconv_ref[i:i + 1, :]
        conv_parts.append(part)
    conv = conv_parts[0] if len(conv_parts) == 1 else jnp.concatenate(conv_parts, axis=0)
    act = conv * jax.nn.sigmoid(conv)

    sm = small_ref[0]
    beta_all = jax.nn.sigmoid(sm)
    g_all = -jnp.exp(alog_ref[...]) * _softplus(sm + dtb_ref[...])
    rt = lax.broadcasted_iota(jnp.int32, (tm, tm), 0)
    ct = lax.broadcasted_iota(jnp.int32, (tm, tm), 1)
    chunk_tri = ((rt // C == ct // C) & (rt >= ct)).astype(F32)
    gcum = jnp.dot(chunk_tri, g_all, preferred_element_type=F32, precision=HI)
    sel = (lax.broadcasted_iota(jnp.int32, (DN_HEADS, LANES), 1)
           == lax.broadcasted_iota(jnp.int32, (DN_HEADS, LANES), 0) + ALPHA_OFF).astype(F32)
    gcum_t = _dot_nt(sel, gcum, precision=HI)

    def per_head_lanes(x, off):
        hi = x.astype(BF16)
        r1 = x - hi.astype(F32)
        mid = r1.astype(BF16)
        lo = (r1 - mid.astype(F32)).astype(BF16)
        erow = lax.broadcasted_iota(jnp.int32, (3 * LANES, DN_QK), 0) % LANES
        ecol = lax.broadcasted_iota(jnp.int32, (3 * LANES, DN_QK), 1) // DN_DK
        return _dot(jnp.concatenate([hi, mid, lo], axis=1), (erow == ecol + off).astype(BF16))

    g_b = per_head_lanes(gcum, ALPHA_OFF)
    beta_b = per_head_lanes(beta_all, BETA_OFF)
    glast_b = jnp.concatenate(
        [jnp.broadcast_to(g_b[c * C + C - 1:(c + 1) * C, :], (C, DN_QK)) for c in range(nc)], axis=0)
    eg_b = jnp.exp(g_b)
    kdf_b = jnp.exp(glast_b - g_b)
    for c in range(nc):
        egl_ref[0, c] = jnp.exp(g_b[c * C + C - 1:(c + 1) * C, :])

    ri = lax.broadcasted_iota(jnp.int32, (C, C), 0)
    ci = lax.broadcasted_iota(jnp.int32, (C, C), 1)
    tri_incl = ri >= ci
    tri_strict = ri > ci
    eye = (ri == ci).astype(F32)
    pair_masks = []
    m = 1
    while m < C:
        pair_masks.append((ri // (2 * m) == ci // (2 * m)) & (ri // m != ci // m))
        m *= 2

    heads = range(DN_HEADS)
    hl = lambda h: slice(h * DN_DK, (h + 1) * DN_DK)
    items = [(c, h) for c in range(nc) for h in heads]
    rows = lambda c: slice(c * C, (c + 1) * C)

    l_prev = [l_scr[rd, i] for i in range(len(items))]
    pinv = [eye - jnp.where(pair_masks[0], l, 0.0) for l in l_prev]
    for mask in pair_masks[1:]:
        p16 = [p.astype(BF16) for p in pinv]
        tmp = [_dot(p16[i], jnp.where(mask, l_prev[i], 0.0).astype(BF16)).astype(BF16) for i in range(len(items))]
        pinv = [pinv[i] - _dot(tmp[i], p16[i]) for i in range(len(items))]
    for i, (c, h) in enumerate(items):
        wu = _dot(pinv[i].astype(BF16), rhs_scr[rd, i])
        w_ref[0, c, :, hl(h)] = wu[:, :DN_DK].astype(BF16)
        u_ref[0, rows(c), hl(h)] = wu[:, DN_DK:]

    qn, kn = [], []
    for h in heads:
        qh = act[:, h * DN_DK:(h + 1) * DN_DK]
        kh = act[:, DN_QK + h * DN_DK:DN_QK + (h + 1) * DN_DK]
        qn.append(qh * lax.rsqrt(jnp.sum(qh * qh, axis=-1, keepdims=True) + EPS) * (DN_DK ** -0.5))
        kn.append(kh * lax.rsqrt(jnp.sum(kh * kh, axis=-1, keepdims=True) + EPS))

    k16 = [kn[h].astype(BF16) for h in heads]
    q16 = [qn[h].astype(BF16) for h in heads]
    kb = [kn[h] * beta_b[:, hl(h)] for h in heads]
    kb16 = [kb[h].astype(BF16) for h in heads]
    kbe16 = [(kb[h] * eg_b[:, hl(h)]).astype(BF16) for h in heads]
    vb16 = [(act[:, 2 * DN_QK + h * DN_DV:2 * DN_QK + (h + 1) * DN_DV] * beta_b[:, hl(h)]).astype(BF16)
            for h in heads]
    qe16 = [(qn[h] * eg_b[:, hl(h)]).astype(BF16) for h in heads]
    for h in heads:
        kd_ref[0, :, hl(h)] = (kn[h] * kdf_b[:, hl(h)]).astype(BF16)

    attn_ref[...] = jnp.zeros_like(attn_ref)
    decay, qk = [], []
    for i, (c, h) in enumerate(items):
        r = rows(c)
        gc = g_b[r, h * DN_DK:h * DN_DK + C]
        decay.append(jnp.exp(jnp.where(tri_incl, gc - gcum_t[h:h + 1, r], -1e30)))
        rhs_scr[wr, i] = jnp.concatenate([kbe16[h][r], vb16[h][r]], axis=1)
        qe_ref[0, c, :, hl(h)] = qe16[h][r]
        qk.append(_dot_nt(jnp.concatenate([kb16[h][r], q16[h][r]], axis=0), k16[h][r]))

    for i, (c, h) in enumerate(items):
        l_scr[wr, i] = jnp.where(tri_strict, qk[i][:C] * decay[i], 0.0)
        attn_ref[0, rows(c), h * DN_DK:h * DN_DK + C] = (qk[i][C:] * decay[i]).astype(BF16)


def _dn_scan_kernel(w_ref, qe_ref, u_ref, kd_ref, attn_ref, egl_ref, za_ref, s0_ref, onorm_ref,
                    ua_ref, sfin_ref, s_ref, *, C):
    n = pl.program_id(1)
    bg, G = w_ref.shape[0], w_ref.shape[1]

    @pl.when(n == 0)
    def _():
        s_ref[...] = s0_ref[...]

    chains = [(b, h) for b in range(bg) for h in range(DN_HEADS)]
    for g in range(G):
        r = slice(g * C, (g + 1) * C)
        s_old = [s_ref[b, h] for b, h in chains]
        s16 = [s.astype(BF16) for s in s_old]
        ws = [_dot(jnp.concatenate([w_ref[b, g, :, h * DN_DK:(h + 1) * DN_DK],
                                    qe_ref[b, g, :, h * DN_DK:(h + 1) * DN_DK]], axis=0), s16[i])
              for i, (b, h) in enumerate(chains)]
        v16 = [(u_ref[b, r, h * DN_DV:(h + 1) * DN_DV] - ws[i][:C]).astype(BF16)
               for i, (b, h) in enumerate(chains)]
        for i, (b, h) in enumerate(chains):
            lo = h * DN_DK
            s_ref[b, h] = s_old[i] * egl_ref[b, g, :, lo:lo + DN_DK] + _dot_tn(kd_ref[b, r, lo:lo + DN_DK], v16[i])
        for i, (b, h) in enumerate(chains):
            lo = h * DN_DV
            o = ws[i][C:] + _dot(attn_ref[b, r, lo:lo + C], v16[i])
            o = o * lax.rsqrt(jnp.mean(o * o, axis=-1, keepdims=True) + EPS) * onorm_ref[...]
            z = za_ref[b, r, lo:lo + DN_DV].astype(F32)
            ua_ref[b, r, lo:lo + DN_DV] = (o * (z * jax.nn.sigmoid(z))).astype(BF16)

    @pl.when(n == pl.num_programs(1) - 1)
    def _():
        sfin_ref[...] = s_ref[...]


def _deltanet(qkv, small, z_a, conv_state, s0, w_conv, alog_v, dtb_v, onorm, tm, bg, G):
    B, T, _ = qkv.shape
    C = min(CHUNK, T)
    N = T // C
    nc = tm // C
    hist_rows = 16
    cs = jnp.pad(conv_state, ((0, 0), (hist_rows - (CONV_W - 1), 0), (0, 0)))
    nt = T // tm
    n_tiles = B * nt

    def cur(s):
        s = jnp.minimum(s, n_tiles - 1)
        return s // nt, s % nt

    def done(s):
        s = jnp.maximum(s - 1, 0)
        return s // nt, s % nt

    tile = lambda s: (*cur(s), 0)
    tile4 = lambda s: (*cur(s), 0, 0)
    prev = lambda s: (cur(s)[0], jnp.maximum(cur(s)[1] * (tm // hist_rows) - 1, 0), 0)
    w, u, qe, kd, attn, egl = pl.pallas_call(
        functools.partial(_dn_prep_kernel, C=C, tiles_per_seq=nt),
        grid=(n_tiles + 1,),
        in_specs=[pl.BlockSpec((1, tm, DN_CONV_CH), tile),
                  pl.BlockSpec((1, hist_rows, DN_CONV_CH), prev),
                  pl.BlockSpec((1, hist_rows, DN_CONV_CH), lambda s: (cur(s)[0], 0, 0)),
                  pl.BlockSpec((1, tm, LANES), tile),
                  _resident((CONV_W, DN_CONV_CH)),
                  _resident((1, LANES)),
                  _resident((1, LANES))],
        out_specs=[pl.BlockSpec((1, nc, C, DN_QK), lambda s: (*done(s), 0, 0)),
                   pl.BlockSpec((1, tm, DN_V), lambda s: (*done(s), 0)),
                   pl.BlockSpec((1, nc, C, DN_QK), tile4),
                   pl.BlockSpec((1, tm, DN_QK), tile),
                   pl.BlockSpec((1, tm, DN_V), tile),
                   pl.BlockSpec((1, nc, 1, DN_QK), tile4)],
        out_shape=[jax.ShapeDtypeStruct((B, N, C, DN_QK), BF16),
                   jax.ShapeDtypeStruct((B, T, DN_V), F32),
                   jax.ShapeDtypeStruct((B, N, C, DN_QK), BF16),
                   jax.ShapeDtypeStruct((B, T, DN_QK), BF16),
                   jax.ShapeDtypeStruct((B, T, DN_V), BF16),
                   jax.ShapeDtypeStruct((B, N, 1, DN_QK), F32)],
        scratch_shapes=[pltpu.VMEM((2, nc * DN_HEADS, C, C), F32),
                        pltpu.VMEM((2, nc * DN_HEADS, C, DN_DK + DN_DV), BF16)],
        compiler_params=_cparams(("arbitrary",)),
        name="dn_prep",
    )(qkv, qkv, cs, small, w_conv, alog_v, dtb_v)

    grp = lambda b, n: (b, n, 0)
    grp4 = lambda b, n: (b, n, 0, 0)
    state = pl.BlockSpec((bg, DN_HEADS, DN_DK, DN_DV), lambda b, n: (b, 0, 0, 0))
    u_a, s_new = pl.pallas_call(
        functools.partial(_dn_scan_kernel, C=C),
        grid=(B // bg, N // G),
        in_specs=[pl.BlockSpec((bg, G, C, DN_QK), grp4),
                  pl.BlockSpec((bg, G, C, DN_QK), grp4),
                  pl.BlockSpec((bg, G * C, DN_V), grp),
                  pl.BlockSpec((bg, G * C, DN_QK), grp),
                  pl.BlockSpec((bg, G * C, DN_V), grp),
                  pl.BlockSpec((bg, G, 1, DN_QK), grp4),
                  pl.BlockSpec((bg, G * C, DN_V), grp),
                  state,
                  _resident((1, DN_DV))],
        out_specs=[pl.BlockSpec((bg, G * C, DN_V), grp), state],
        out_shape=[jax.ShapeDtypeStruct((B, T, DN_V), BF16),
                   jax.ShapeDtypeStruct((B, DN_HEADS, DN_DK, DN_DV), F32)],
        scratch_shapes=[pltpu.VMEM((bg, DN_HEADS, DN_DK, DN_DV), F32)],
        compiler_params=_cparams(("arbitrary", "arbitrary")),
        name="dn_scan",
    )(w, qe, u, kd, attn, egl, z_a, s0, onorm)
    conv_new = qkv[:, T - (CONV_W - 1):, :].astype(F32)
    return u_a, s_new, conv_new


def _head_rms(x, gain_row, width):
    outs = []
    for h in range(x.shape[1] // width):
        xh = x[:, h * width:(h + 1) * width]
        outs.append(xh * lax.rsqrt(jnp.mean(xh * xh, axis=-1, keepdims=True) + EPS) * gain_row)
    return outs


def _rope_inv_freq(shape):
    half = QK_ROPE // 2
    fidx = (lax.broadcasted_iota(jnp.int32, shape, 1) % half).astype(F32)
    return jnp.exp(fidx * (-math.log(ROPE_THETA) / half))


def _chunk_mask(qpos0, kpos0, tq, tk):
    qc = (lax.broadcasted_iota(jnp.int32, (tq, tk), 0) + qpos0) // CHUNK
    kc = (lax.broadcasted_iota(jnp.int32, (tq, tk), 1) + kpos0) // CHUNK
    return kc <= qc


def _attn_prompt_kernel(q_ref, k_ref, v_ref, o_ref, m_ref, l_ref, acc_ref, *, tq, tk, sub):
    i = pl.program_id(2)
    nsub = tq // sub
    ratio = tq // tk
    m_ref[...] = jnp.full_like(m_ref, -1e30)
    l_ref[...] = jnp.zeros_like(l_ref)
    acc_ref[...] = jnp.zeros_like(acc_ref)

    def scores(r, k):
        return _dot_nt(q_ref[0, r * sub:(r + 1) * sub, :], k)

    def softmax_pv(r, sr, v, mask):
        rows = slice(r * sub, (r + 1) * sub)
        if mask is not None:
            sr = jnp.where(mask, sr, -1e30)
        m_old = m_ref[rows, :]
        m_new = jnp.maximum(m_old, jnp.max(sr, axis=-1, keepdims=True))
        alpha = jnp.exp2(m_old - m_new)
        p = jnp.exp2(sr - jnp.tile(m_new, (1, tk // LANES)))
        psum = p[:, 0:LANES]
        for c in range(1, tk // LANES):
            psum = psum + p[:, c * LANES:(c + 1) * LANES]
        l_ref[rows, :] = alpha * l_ref[rows, :] + psum
        acc_ref[rows, :] = alpha * acc_ref[rows, :] + _dot(p.astype(BF16), v)
        m_ref[rows, :] = m_new

    def run(j0, items):
        kv = {}
        for d in sorted({d for d, _, _ in items}):
            start = pl.multiple_of((j0 + d) * tk, tk)
            kv[d] = (k_ref[0, pl.ds(start, tk), :], v_ref[0, pl.ds(start, tk), :])
        s = {n: scores(items[n][1], kv[items[n][0]][0]) for n in range(min(ATTN_LOOKAHEAD, len(items)))}
        for n, (d, r, mask) in enumerate(items):
            ahead = n + ATTN_LOOKAHEAD
            if ahead < len(items):
                s[ahead] = scores(items[ahead][1], kv[items[ahead][0]][0])
            softmax_pv(r, s.pop(n), kv[d][1], mask)

    per_trip = math.gcd(ratio, ATTN_TILES_PER_TRIP)
    full = [(d, r, None) for d in range(per_trip) for r in range(nsub)]

    def body(jj, carry):
        run(jj * per_trip, full)
        return carry

    lax.fori_loop(0, i * (ratio // per_trip), body, 0)
    diag = []
    for d in range(ratio):
        for r in range(nsub):
            q_lo, q_hi = (r * sub) // CHUNK, (r * sub + sub - 1) // CHUNK
            k_lo, k_hi = (d * tk) // CHUNK, (d * tk + tk - 1) // CHUNK
            if k_lo > q_hi:
                continue
            diag.append((d, r, None if k_hi <= q_lo else _chunk_mask(r * sub, d * tk, sub, tk)))
    run(i * ratio, diag)
    l = jnp.sum(l_ref[...], axis=-1, keepdims=True)
    o_ref[0] = (acc_ref[...] / l).astype(BF16)


def _attn_prompt(qcat, kcat, v, tq, tk):
    B, T, _ = v.shape
    return pl.pallas_call(
        functools.partial(_attn_prompt_kernel, tq=tq, tk=tk, sub=ATTN_SUB),
        grid=(B, MLA_HEADS, T // tq),
        in_specs=[pl.BlockSpec((1, tq, QK_CAT), lambda b, h, i: (b, i, h)),
                  pl.BlockSpec((1, T, QK_CAT), lambda b, h, i: (b, 0, h)),
                  pl.BlockSpec((1, T, V_HEAD), lambda b, h, i: (b, 0, h))],
        out_specs=pl.BlockSpec((1, tq, V_HEAD), lambda b, h, i: (b, i, h)),
        out_shape=jax.ShapeDtypeStruct((B, T, MLA_V), BF16),
        scratch_shapes=[pltpu.VMEM((tq, LANES), F32), pltpu.VMEM((tq, LANES), F32),
                        pltpu.VMEM((tq, V_HEAD), F32)],
        compiler_params=_cparams(("arbitrary", "arbitrary", "arbitrary")),
        name="attn_prompt",
    )(qcat, kcat, v)


def _attn_sample_kernel(q_ref, kn_ref, vn_ref, ckv_ref, kr_ref, kng_ref, wuk_ref, wuv_ref, o_ref,
                        m_ref, l_ref, acc_ref, *, tk):
    T = q_ref.shape[1]
    P = ckv_ref.shape[1]
    R = MLA_HEADS * T
    q = q_ref[0]
    row_head = lax.broadcasted_iota(jnp.int32, (R, 1), 0) // T
    q_chunk = (lax.broadcasted_iota(jnp.int32, (R, 1), 0) % T + P) // CHUNK

    def stacked(x, width):
        lane_head = lax.broadcasted_iota(jnp.int32, (R, x.shape[1]), 1) // width
        return jnp.where(lane_head == row_head, jnp.concatenate([x] * MLA_HEADS, axis=0), jnp.zeros((), x.dtype))

    q_nope = stacked(jnp.concatenate([q[:, h * QK_CAT:h * QK_CAT + QK_NOPE] for h in range(MLA_HEADS)], axis=1),
                     QK_NOPE)
    q_rope = jnp.concatenate([q[:, h * QK_CAT + QK_NOPE:h * QK_CAT + QK_HEAD] for h in range(MLA_HEADS)],
                             axis=0)

    def update(s, k_chunk):
        s = jnp.where(k_chunk <= q_chunk, s, -1e30)
        m_old = m_ref[...]
        m_new = jnp.maximum(m_old, jnp.max(s, axis=-1, keepdims=True))
        alpha = jnp.exp2(m_old - m_new)
        p = jnp.exp2(s - m_new)
        l_ref[...] = alpha * l_ref[...] + jnp.sum(p, axis=-1, keepdims=True)
        m_ref[...] = m_new
        return alpha, p.astype(BF16)

    m_ref[...] = jnp.full_like(m_ref, -1e30)
    l_ref[...] = jnp.zeros_like(l_ref)
    acc_ref[...] = jnp.zeros_like(acc_ref)

    def body(c, carry):
        start = pl.multiple_of(c * tk, tk)
        ckv16 = ckv_ref[0, pl.ds(start, tk), :].astype(BF16)
        kn = _head_rms(_dot(ckv16, wuk_ref[...]), kng_ref[...], QK_NOPE)
        kn16 = jnp.concatenate([x.astype(BF16) for x in kn], axis=1)
        s = _dot_nt(q_nope, kn16) + _dot_nt(q_rope, kr_ref[0, pl.ds(start, tk), :].astype(BF16))
        k_chunk = (lax.broadcasted_iota(jnp.int32, (1, tk), 1) + start) // CHUNK
        alpha, p16 = update(s, k_chunk)
        acc_ref[...] = alpha * acc_ref[...] + _dot(p16, ckv16)
        return carry

    lax.fori_loop(0, P // tk, body, 0)

    s_new = _dot_nt(stacked(q, QK_CAT), kn_ref[0])
    alpha, p16 = update(s_new, (lax.broadcasted_iota(jnp.int32, (1, T), 1) + P) // CHUNK)
    pc16 = (alpha * acc_ref[...]).astype(BF16)
    inv_l = 1.0 / l_ref[...]
    for h in range(MLA_HEADS):
        rows = slice(h * T, (h + 1) * T)
        lanes = slice(h * V_HEAD, (h + 1) * V_HEAD)
        o = _dot(pc16[rows], wuv_ref[:, lanes]) + _dot(p16[rows], vn_ref[0, :, lanes])
        o_ref[0, :, lanes] = (o * inv_l[rows]).astype(BF16)


def _attn_sample(qcat, kcat_new, v_new, past_ckv, past_kr, kn_gain, w_uk16, w_uv16, tk):
    B, T, _ = v_new.shape
    P = past_ckv.shape[1]
    R = MLA_HEADS * T
    perb = lambda b: (b, 0, 0)
    return pl.pallas_call(
        functools.partial(_attn_sample_kernel, tk=tk),
        grid=(B,),
        in_specs=[pl.BlockSpec((1, T, MLA_HEADS * QK_CAT), perb),
                  pl.BlockSpec((1, T, MLA_HEADS * QK_CAT), perb),
                  pl.BlockSpec((1, T, MLA_V), perb),
                  pl.BlockSpec((1, P, KV_RANK), perb),
                  pl.BlockSpec((1, P, QK_ROPE), perb),
                  _resident((1, QK_NOPE)),
                  _resident((KV_RANK, MLA_HEADS * QK_NOPE)),
                  _resident((KV_RANK, MLA_V))],
        out_specs=pl.BlockSpec((1, T, MLA_V), perb),
        out_shape=jax.ShapeDtypeStruct((B, T, MLA_V), BF16),
        scratch_shapes=[pltpu.VMEM((R, 1), F32), pltpu.VMEM((R, 1), F32), pltpu.VMEM((R, KV_RANK), F32)],
        compiler_params=_cparams(("arbitrary",)),
        name="attn_sample",
    )(qcat, kcat_new, v_new, past_ckv, past_kr, kn_gain, w_uk16, w_uv16)


def _out_kernel(x_ref, mod_ref, ua_ref, ob_ref, zb_ref, ga_ref, gb_ref, wdn_ref, wmla_ref, wout_ref, y_ref):
    bb, tm, d = x_ref.shape
    rows = bb * tm
    zb = zb_ref[...].astype(F32)
    ub = (ob_ref[...].astype(F32) * (zb * jax.nn.sigmoid(zb))).astype(BF16).reshape(rows, d)
    ya = _dot(ua_ref[...].reshape(rows, d), wdn_ref[...])
    yb = _dot(ub, wmla_ref[...])
    ga = jax.nn.sigmoid(ga_ref[...].astype(F32)).reshape(rows, d)
    gb = jax.nn.sigmoid(gb_ref[...].astype(F32)).reshape(rows, d)
    merged = (ga * ya + gb * yb).astype(BF16)
    out = _dot(merged, wout_ref[...]).reshape(bb, tm, d)
    gate = mod_ref[:, :, 2 * d:3 * d]
    y_ref[...] = x_ref[...] + gate * out


def _out_proj(x, mod3, u_a, o_b, z_b, g_a, g_b, w_dn16, w_mla16, w_out16, bb, tm):
    B, T, _ = x.shape
    row = lambda b, t: (b, t, 0)
    act = pl.BlockSpec((bb, tm, D_MODEL), row)
    return pl.pallas_call(
        _out_kernel,
        grid=(B // bb, T // tm),
        in_specs=[act, pl.BlockSpec((bb, 1, 3 * D_MODEL), lambda b, t: (b, 0, 0)), act, act, act, act, act,
                  _resident((D_MODEL, D_MODEL)), _resident((D_MODEL, D_MODEL)), _resident((D_MODEL, D_MODEL))],
        out_specs=act,
        out_shape=jax.ShapeDtypeStruct((B, T, D_MODEL), F32),
        compiler_params=_cparams(("arbitrary", "arbitrary")),
        name="out_proj",
    )(x, mod3, u_a, o_b, z_b, g_a, g_b, w_dn16, w_mla16, w_out16)


def _lane_vec(v, off):
    return jnp.zeros((1, LANES), F32).at[0, off:off + v.shape[0]].set(v)


def _tiles(B, T, cached):
    if cached:
        whole = (B, T)
        return dict(proj=whole, out=whole, dn=dict(tm=T, bg=2, G=1), attn_tk=1024)
    return dict(proj=(1, 256), out=(1, 1024), dn=dict(tm=256, bg=B, G=8), attn=(2048, 512))


def _layer(x, mod, conv_state, s0, past, prm, q_off):
    B, T, _ = x.shape
    tiles = _tiles(B, T, past is not None)
    mod3 = mod.reshape(B, 1, 3 * D_MODEL)
    mla = (prm["q_nope_norm"], prm["qr_gain"], prm["kv_norm"], prm["kr_gain"], prm["k_nope_norm"],
           prm["w_uk"], prm["w_uv"])
    small, qkv, z_a, z_b, g_a, g_b, qcat, kcat, v, ckv_new, kr_new = _in_proj(
        x, mod3, prm["norm_gain"], prm["w_pack"], mla, *tiles["proj"], q_off)
    u_a, s_new, conv_new = _deltanet(qkv, small, z_a, conv_state, s0, prm["w_conv"], prm["alog_v"],
                                     prm["dtb_v"], prm["dn_out_norm"], **tiles["dn"])
    if past is None:
        o_b = _attn_prompt(qcat, kcat, v, *tiles["attn"])
    else:
        past_ckv, past_kr = past
        o_b = _attn_sample(qcat, kcat, v, past_ckv, past_kr, prm["k_nope_norm"], prm["w_uk"], prm["w_uv"],
                           tiles["attn_tk"])
    y = _out_proj(x, mod3, u_a, o_b, z_b, g_a, g_b, prm["w_o_dn"], prm["w_o_mla"], prm["w_out"], *tiles["out"])
    return y, conv_new, s_new, ckv_new, kr_new


def kernel(x_prompt, x_sample, c_prompt, c_sample, cache_ckv, cache_krope, state_delta, state_conv, norm_gain, w_ada, b_ada, w_in, w_conv, a_log, dt_bias, dn_out_norm, q_nope_norm, q_rope_norm, k_nope_norm, k_rope_norm, kv_norm, w_uk, w_uv, w_o_dn, w_o_mla, w_out):
    depth = w_in.shape[0]
    assert depth == 1, "single-layer configuration"
    l = 0
    B, T, _ = x_prompt.shape
    Bs, Ts, _ = x_sample.shape
    past_len = cache_ckv.shape[2]

    row = lambda v: v.reshape(1, -1).astype(F32)
    prm = dict(
        norm_gain=row(norm_gain[l]),
        w_pack=_pack_w_in(w_in[l]),
        w_conv=w_conv[l],
        alog_v=_lane_vec(a_log[l], ALPHA_OFF),
        dtb_v=_lane_vec(dt_bias[l], ALPHA_OFF),
        dn_out_norm=row(dn_out_norm[l]),
        q_nope_norm=row(q_nope_norm[l]),
        qr_gain=jnp.tile(row(q_rope_norm[l]), (1, LANES // QK_ROPE)),
        kv_norm=row(kv_norm[l]),
        kr_gain=_lane_vec(k_rope_norm[l], KR_OFF),
        k_nope_norm=row(k_nope_norm[l]),
        w_uk=w_uk[l].astype(BF16),
        w_uv=w_uv[l].astype(BF16),
        w_o_dn=w_o_dn[l].astype(BF16),
        w_o_mla=w_o_mla[l].astype(BF16),
        w_out=w_out[l].astype(BF16),
    )

    rows = B + Bs
    rows_pad = -(-rows // 8) * 8
    c_all = jnp.concatenate([c_prompt, c_sample, jnp.zeros((rows_pad - rows, D_MODEL), F32)], axis=0)
    mod = _ada(c_all, w_ada[l], b_ada[l].reshape(1, -1))

    zeros_conv = jnp.zeros((B, CONV_W - 1, DN_CONV_CH), F32)
    zeros_state = jnp.zeros((B, DN_HEADS, DN_DK, DN_DV), F32)
    yp, cvp, sdp, kvp, krp = _layer(x_prompt, mod[:B], zeros_conv, zeros_state, None, prm, q_off=0)
    ys, cvs, sds, kvs, krs = _layer(x_sample, mod[B:rows], state_conv[l], state_delta[l],
                                    (cache_ckv[l], cache_krope[l]), prm, q_off=past_len)
    st = lambda a: a[None]
    return (yp, ys, st(kvp), st(krp), st(sdp), st(cvp), st(kvs), st(krs), st(sds), st(cvs))
```

```python
import functools
import math

import jax
import jax.numpy as jnp
from jax import lax
from jax.experimental import pallas as pl
from jax.experimental.pallas import tpu as pltpu

D_MODEL = 1024
CHUNK = 64
EPS = 1e-6
DN_HEADS = 8
DN_DK = 128
DN_DV = 128
DN_QK = DN_HEADS * DN_DK
DN_V = DN_HEADS * DN_DV
DN_CONV_CH = 2 * DN_QK + DN_V
CONV_W = 4
MLA_HEADS = 8
QK_NOPE = 128
QK_ROPE = 64
QK_HEAD = QK_NOPE + QK_ROPE
V_HEAD = 128
KV_RANK = 512
MLA_Q = MLA_HEADS * QK_HEAD
MLA_V = MLA_HEADS * V_HEAD
ROPE_THETA = 10000.0

LANES = 128
QK_CAT = 256
CONV_ROWS = 128
ATTN_SUB = 256
ATTN_LOOKAHEAD = 2
ATTN_TILES_PER_TRIP = 4
KR_OFF = 0
BETA_OFF = QK_ROPE
ALPHA_OFF = QK_ROPE + DN_HEADS
VMEM_LIMIT = 56 * 1024 * 1024

F32 = jnp.float32
BF16 = jnp.bfloat16
HI = lax.Precision.HIGHEST


def _dot(a, b):
    return jnp.dot(a, b, preferred_element_type=F32)


def _dot_nt(a, b, precision=None):
    return lax.dot_general(a, b, (((1,), (1,)), ((), ())), preferred_element_type=F32, precision=precision)


def _dot_tn(a, b):
    return lax.dot_general(a, b, (((0,), (0,)), ((), ())), preferred_element_type=F32)


def _cparams(sem):
    return pltpu.CompilerParams(dimension_semantics=sem, vmem_limit_bytes=VMEM_LIMIT)


def _resident(shape):
    nd = len(shape)
    return pl.BlockSpec(shape, lambda *_: (0,) * nd, pipeline_mode=pl.Buffered(1))


def _ada_kernel(c_ref, w_ref, b_ref, o_ref):
    o_ref[...] = jnp.dot(c_ref[...], w_ref[...], preferred_element_type=F32, precision=HI) + b_ref[...]


def _ada(c_all, w_ada, b_ada):
    rows = c_all.shape[0]
    tn = 1024
    return pl.pallas_call(
        _ada_kernel,
        grid=(3 * D_MODEL // tn,),
        in_specs=[pl.BlockSpec((rows, D_MODEL), lambda j: (0, 0)),
                  pl.BlockSpec((D_MODEL, tn), lambda j: (0, j)),
                  pl.BlockSpec((1, tn), lambda j: (0, j))],
        out_specs=pl.BlockSpec((rows, tn), lambda j: (0, j)),
        out_shape=jax.ShapeDtypeStruct((rows, 3 * D_MODEL), F32),
        compiler_params=_cparams(("arbitrary",)),
        name="ada",
    )(c_all, w_ada, b_ada)


_PROJ_WIDTHS = (LANES, KV_RANK, MLA_HEADS * QK_ROPE, MLA_HEADS * QK_NOPE, DN_CONV_CH, DN_V, MLA_V, D_MODEL, D_MODEL)


def _pack_w_in(w_in):
    o = 0
    qkv = w_in[:, o:o + DN_CONV_CH]; o += DN_CONV_CH
    z_a = w_in[:, o:o + DN_V]; o += DN_V
    beta = w_in[:, o:o + DN_HEADS]; o += DN_HEADS
    alpha = w_in[:, o:o + DN_HEADS]; o += DN_HEADS
    q = w_in[:, o:o + MLA_Q].reshape(D_MODEL, MLA_HEADS, QK_HEAD); o += MLA_Q
    ckv = w_in[:, o:o + KV_RANK]; o += KV_RANK
    kr = w_in[:, o:o + QK_ROPE]; o += QK_ROPE
    z_b = w_in[:, o:o + MLA_V]; o += MLA_V
    g_a = w_in[:, o:o + D_MODEL]; o += D_MODEL
    g_b = w_in[:, o:o + D_MODEL]
    qn = q[:, :, :QK_NOPE].reshape(D_MODEL, MLA_HEADS * QK_NOPE)
    qr = q[:, :, QK_NOPE:].reshape(D_MODEL, MLA_HEADS * QK_ROPE)
    pad = jnp.zeros((D_MODEL, LANES - QK_ROPE - 2 * DN_HEADS), w_in.dtype)
    small = jnp.concatenate([kr, beta, alpha, pad], axis=1)
    return tuple(w.astype(BF16) for w in (small, ckv, qr, qn, qkv, z_a, z_b, g_a, g_b))


def _in_proj_kernel(x_ref, mod_ref, gain_ref,
                    w_small, w_ckv, w_qr, w_qn, w_qkv, w_za, w_zb, w_ga, w_gb,
                    qng_ref, qrg_ref, kvg_ref, krg_ref, kng_ref, wuk_ref, wuv_ref,
                    small_ref, qkv_ref, za_ref, zb_ref, ga_ref, gb_ref,
                    qcat_ref, kcat_ref, v_ref, ckvn_ref, krn_ref, rope_ref, *, q_off):
    bb, tm, d = x_ref.shape
    rows = bb * tm

    @pl.when((pl.program_id(0) == 0) & (pl.program_id(1) == 0))
    def _():
        off = (lax.broadcasted_iota(jnp.int32, (rows, LANES), 0) % tm).astype(F32) * _rope_inv_freq((rows, LANES))
        rope_ref[0] = jnp.cos(off)
        rope_ref[1] = jnp.sin(off)

    x = x_ref[...]
    ms = jnp.mean(x * x, axis=-1, keepdims=True)
    y = x * lax.rsqrt(ms + EPS) * gain_ref[...]
    shift = mod_ref[:, :, 0:d]
    scale = mod_ref[:, :, d:2 * d]
    h = (y * (1.0 + scale) + shift).astype(BF16).reshape(rows, d)

    def project(w_ref, o_ref):
        o_ref[...] = _dot(h, w_ref[...]).astype(o_ref.dtype).reshape(o_ref.shape)

    sm = _dot(h, w_small[...])
    small_ref[...] = sm.reshape(bb, tm, LANES)
    ckv = _dot(h, w_ckv[...])
    qr_all = _dot(h, w_qr[...])
    qn_all = _dot(h, w_qn[...])
    project(w_qkv, qkv_ref)

    t0 = pl.program_id(1) * tm
    qscale = QK_HEAD ** -0.5 * math.log2(math.e)
    base = (t0 + q_off).astype(F32) * _rope_inv_freq((1, LANES))
    cos_a, sin_a = jnp.cos(base), jnp.sin(base)
    cos = cos_a * rope_ref[0] - sin_a * rope_ref[1]
    sin = sin_a * rope_ref[0] + cos_a * rope_ref[1]
    lane = lax.broadcasted_iota(jnp.int32, (rows, LANES), 1)
    low_half = lane < QK_ROPE
    first = (lane % QK_ROPE) < (QK_ROPE // 2)

    def rope(z):
        rot = jnp.where(first, -pltpu.roll(z, LANES - QK_ROPE // 2, 1), pltpu.roll(z, QK_ROPE // 2, 1))
        return z * cos + rot * sin

    def rms64(z):
        zz = z * z
        s_lo = jnp.sum(jnp.where(low_half, zz, 0.0), axis=-1, keepdims=True)
        s_hi = jnp.sum(jnp.where(low_half, 0.0, zz), axis=-1, keepdims=True)
        return lax.rsqrt(jnp.where(low_half, s_lo, s_hi) * (1.0 / QK_ROPE) + EPS)

    ckvn = ckv * lax.rsqrt(jnp.mean(ckv * ckv, axis=-1, keepdims=True) + EPS) * kvg_ref[...]
    ckvn_ref[...] = ckvn.reshape(bb, tm, KV_RANK)
    ckvn16 = ckvn.astype(BF16)
    kr = rope(sm * rms64(sm) * krg_ref[...])
    krn_ref[...] = kr[:, :QK_ROPE].reshape(bb, tm, QK_ROPE)
    kr_pad16 = jnp.where(low_half, kr, 0.0).astype(BF16).reshape(bb, tm, LANES)

    project(w_za, za_ref)
    k_raw = _dot(ckvn16, wuk_ref[...])
    v_ref[...] = _dot(ckvn16, wuv_ref[...]).astype(BF16).reshape(bb, tm, MLA_V)
    project(w_zb, zb_ref)

    qn = _head_rms(qn_all, qng_ref[...] * qscale, QK_NOPE)
    for c in range(MLA_HEADS // 2):
        z = qr_all[:, c * LANES:(c + 1) * LANES]
        z = rope(z * rms64(z) * qrg_ref[...]) * qscale
        even = jnp.where(low_half, z, 0.0)
        odd = jnp.where(low_half, pltpu.roll(z, QK_ROPE, 1), 0.0)
        for hh, part in ((2 * c, even), (2 * c + 1, odd)):
            qcat_ref[:, :, hh * QK_CAT:hh * QK_CAT + QK_NOPE] = qn[hh].astype(BF16).reshape(bb, tm, QK_NOPE)
            qcat_ref[:, :, hh * QK_CAT + QK_NOPE:(hh + 1) * QK_CAT] = part.astype(BF16).reshape(bb, tm, LANES)
    project(w_ga, ga_ref)

    kn = _head_rms(k_raw, kng_ref[...], QK_NOPE)
    for hh in range(MLA_HEADS):
        kcat_ref[:, :, hh * QK_CAT:hh * QK_CAT + QK_NOPE] = kn[hh].astype(BF16).reshape(bb, tm, QK_NOPE)
        kcat_ref[:, :, hh * QK_CAT + QK_NOPE:(hh + 1) * QK_CAT] = kr_pad16
    project(w_gb, gb_ref)


def _in_proj(x, mod3, gain, w_pack, mla, bb, tm, q_off):
    B, T, _ = x.shape
    row = lambda b, t: (b, t, 0)
    outs = ((LANES, F32), (DN_CONV_CH, BF16), (DN_V, BF16), (MLA_V, BF16), (D_MODEL, BF16), (D_MODEL, BF16),
            (MLA_HEADS * QK_CAT, BF16), (MLA_HEADS * QK_CAT, BF16), (MLA_V, BF16), (KV_RANK, F32), (QK_ROPE, F32))
    return pl.pallas_call(
        functools.partial(_in_proj_kernel, q_off=q_off),
        grid=(B // bb, T // tm),
        in_specs=[pl.BlockSpec((bb, tm, D_MODEL), row),
                  pl.BlockSpec((bb, 1, 3 * D_MODEL), lambda b, t: (b, 0, 0)),
                  _resident((1, D_MODEL))]
        + [_resident((D_MODEL, w)) for w in _PROJ_WIDTHS]
        + [_resident((1, QK_NOPE)), _resident((1, LANES)), _resident((1, KV_RANK)), _resident((1, LANES)),
           _resident((1, QK_NOPE)), _resident((KV_RANK, MLA_HEADS * QK_NOPE)), _resident((KV_RANK, MLA_V))],
        out_specs=[pl.BlockSpec((bb, tm, w), row) for w, _ in outs],
        out_shape=[jax.ShapeDtypeStruct((B, T, w), dt) for w, dt in outs],
        scratch_shapes=[pltpu.VMEM((2, bb * tm, LANES), F32)],
        compiler_params=_cparams(("arbitrary", "arbitrary")),
        name="in_proj",
    )(x, mod3, gain, *w_pack, *mla)


def _softplus(x):
    return jnp.maximum(x, 0.0) + jnp.log(1.0 + jnp.exp(-jnp.abs(x)))


def _dn_prep_kernel(qkv_ref, prev_ref, cs_ref, small_ref, wconv_ref, alog_ref, dtb_ref,
                    w_ref, u_ref, qe_ref, kd_ref, attn_ref, egl_ref, l_scr, rhs_scr, *, C, tiles_per_seq):
    step = pl.program_id(0)
    tile_idx = jnp.minimum(step, pl.num_programs(0) - 2) % tiles_per_seq
    tm = qkv_ref.shape[1]
    nc = tm // C
    pad = prev_ref.shape[1]

    wr = step % 2
    rd = 1 - wr

    @pl.when(step == 0)
    def _():
        l_scr[...] = jnp.zeros_like(l_scr)
        rhs_scr[...] = jnp.zeros_like(rhs_scr)

    x16 = qkv_ref[0]
    hist = jnp.where(tile_idx == 0, cs_ref[0], prev_ref[0].astype(F32))
    hist_hi = hist.astype(BF16)
    rem = hist - hist_hi.astype(F32)
    hist_mid = rem.astype(BF16)
    hist_lo = (rem - hist_mid.astype(F32)).astype(BF16)
    n_sh = CONV_W - 1
    rs = min(tm, CONV_ROWS)

    def shifted_taps(pieces, xs):
        npc = len(pieces)
        full16 = jnp.concatenate(list(pieces) + [xs], axis=0)
        srow = lax.broadcasted_iota(jnp.int32, (n_sh * rs, npc * pad + rs), 0)
        scol = lax.broadcasted_iota(jnp.int32, (n_sh * rs, npc * pad + rs), 1)
        src = srow % rs + srow // rs + pad - n_sh
        sel = scol == src + (npc - 1) * pad
        for p in range(npc - 1):
            sel = sel | ((scol == src + p * pad) & (scol < (p + 1) * pad))
        return _dot(sel.astype(BF16), full16)

    conv_parts = []
    for j in range(tm // rs):
        xs = x16[j * rs:(j + 1) * rs]
        pieces = (hist_hi, hist_mid, hist_lo) if j == 0 else (x16[j * rs - pad:j * rs],)
        shifted = shifted_taps(pieces, xs)
        part = xs.astype(F32) * wconv_ref[n_sh:CONV_W, :]
        for i in range(n_sh):
            part = part + shifted[i * rs:(i + 1) * rs, :] * wconv_ref[i:i + 1, :]
        conv_parts.append(part)
    conv = conv_parts[0] if len(conv_parts) == 1 else jnp.concatenate(conv_parts, axis=0)
    act = conv * jax.nn.sigmoid(conv)

    sm = small_ref[0]
    beta_all = jax.nn.sigmoid(sm)
    g_all = -jnp.exp(alog_ref[...]) * _softplus(sm + dtb_ref[...])
    rt = lax.broadcasted_iota(jnp.int32, (tm, tm), 0)
    ct = lax.broadcasted_iota(jnp.int32, (tm, tm), 1)
    chunk_tri = ((rt // C == ct // C) & (rt >= ct)).astype(F32)
    gcum = jnp.dot(chunk_tri, g_all, preferred_element_type=F32, precision=HI)
    sel = (lax.broadcasted_iota(jnp.int32, (DN_HEADS, LANES), 1)
           == lax.broadcasted_iota(jnp.int32, (DN_HEADS, LANES), 0) + ALPHA_OFF).astype(F32)
    gcum_t = _dot_nt(sel, gcum, precision=HI)

    def per_head_lanes(x, off):
        hi = x.astype(BF16)
        r1 = x - hi.astype(F32)
        mid = r1.astype(BF16)
        lo = (r1 - mid.astype(F32)).astype(BF16)
        erow = lax.broadcasted_iota(jnp.int32, (3 * LANES, DN_QK), 0) % LANES
        ecol = lax.broadcasted_iota(jnp.int32, (3 * LANES, DN_QK), 1) // DN_DK
        return _dot(jnp.concatenate([hi, mid, lo], axis=1), (erow == ecol + off).astype(BF16))

    g_b = per_head_lanes(gcum, ALPHA_OFF)
    beta_b = per_head_lanes(beta_all, BETA_OFF)
    glast_b = jnp.concatenate(
        [jnp.broadcast_to(g_b[c * C + C - 1:(c + 1) * C, :], (C, DN_QK)) for c in range(nc)], axis=0)
    eg_b = jnp.exp(g_b)
    kdf_b = jnp.exp(glast_b - g_b)
    for c in range(nc):
        egl_ref[0, c] = jnp.exp(g_b[c * C + C - 1:(c + 1) * C, :])

    ri = lax.broadcasted_iota(jnp.int32, (C, C), 0)
    ci = lax.broadcasted_iota(jnp.int32, (C, C), 1)
    tri_incl = ri >= ci
    tri_strict = ri > ci
    eye = (ri == ci).astype(F32)
    pair_masks = []
    m = 1
    while m < C:
        pair_masks.append((ri // (2 * m) == ci // (2 * m)) & (ri // m != ci // m))
        m *= 2

    heads = range(DN_HEADS)
    hl = lambda h: slice(h * DN_DK, (h + 1) * DN_DK)
    items = [(c, h) for c in range(nc) for h in heads]
    rows = lambda c: slice(c * C, (c + 1) * C)

    l_prev = [l_scr[rd, i] for i in range(len(items))]
    pinv = [eye - jnp.where(pair_masks[0], l, 0.0) for l in l_prev]
    for mask in pair_masks[1:]:
        p16 = [p.astype(BF16) for p in pinv]
        tmp = [_dot(p16[i], jnp.where(mask, l_prev[i], 0.0).astype(BF16)).astype(BF16) for i in range(len(items))]
        pinv = [pinv[i] - _dot(tmp[i], p16[i]) for i in range(len(items))]
    for i, (c, h) in enumerate(items):
        wu = _dot(pinv[i].astype(BF16), rhs_scr[rd, i])
        w_ref[0, c, :, hl(h)] = wu[:, :DN_DK].astype(BF16)
        u_ref[0, rows(c), hl(h)] = wu[:, DN_DK:]

    qn, kn = [], []
    for h in heads:
        qh = act[:, h * DN_DK:(h + 1) * DN_DK]
        kh = act[:, DN_QK + h * DN_DK:DN_QK + (h + 1) * DN_DK]
        qn.append(qh * lax.rsqrt(jnp.sum(qh * qh, axis=-1, keepdims=True) + EPS) * (DN_DK ** -0.5))
        kn.append(kh * lax.rsqrt(jnp.sum(kh * kh, axis=-1, keepdims=True) + EPS))

    k16 = [kn[h].astype(BF16) for h in heads]
    q16 = [qn[h].astype(BF16) for h in heads]
    kb = [kn[h] * beta_b[:, hl(h)] for h in heads]
    kb16 = [kb[h].astype(BF16) for h in heads]
    kbe16 = [(kb[h] * eg_b[:, hl(h)]).astype(BF16) for h in heads]
    vb16 = [(act[:, 2 * DN_QK + h * DN_DV:2 * DN_QK + (h + 1) * DN_DV] * beta_b[:, hl(h)]).astype(BF16)
            for h in heads]
    qe16 = [(qn[h] * eg_b[:, hl(h)]).astype(BF16) for h in heads]
    for h in heads:
        kd_ref[0, :, hl(h)] = (kn[h] * kdf_b[:, hl(h)]).astype(BF16)

    attn_ref[...] = jnp.zeros_like(attn_ref)
    decay, qk = [], []
    for i, (c, h) in enumerate(items):
        r = rows(c)
        gc = g_b[r, h * DN_DK:h * DN_DK + C]
        decay.append(jnp.exp(jnp.where(tri_incl, gc - gcum_t[h:h + 1, r], -1e30)))
        rhs_scr[wr, i] = jnp.concatenate([kbe16[h][r], vb16[h][r]], axis=1)
        qe_ref[0, c, :, hl(h)] = qe16[h][r]
        qk.append(_dot_nt(jnp.concatenate([kb16[h][r], q16[h][r]], axis=0), k16[h][r]))

    for i, (c, h) in enumerate(items):
        l_scr[wr, i] = jnp.where(tri_strict, qk[i][:C] * decay[i], 0.0)
        attn_ref[0, rows(c), h * DN_DK:h * DN_DK + C] = (qk[i][C:] * decay[i]).astype(BF16)


def _dn_scan_kernel(w_ref, qe_ref, u_ref, kd_ref, attn_ref, egl_ref, za_ref, s0_ref, onorm_ref,
                    ua_ref, sfin_ref, s_ref, *, C):
    n = pl.program_id(1)
    bg, G = w_ref.shape[0], w_ref.shape[1]

    @pl.when(n == 0)
    def _():
        s_ref[...] = s0_ref[...]

    chains = [(b, h) for b in range(bg) for h in range(DN_HEADS)]
    for g in range(G):
        r = slice(g * C, (g + 1) * C)
        s_old = [s_ref[b, h] for b, h in chains]
        s16 = [s.astype(BF16) for s in s_old]
        ws = [_dot(jnp.concatenate([w_ref[b, g, :, h * DN_DK:(h + 1) * DN_DK],
                                    qe_ref[b, g, :, h * DN_DK:(h + 1) * DN_DK]], axis=0), s16[i])
              for i, (b, h) in enumerate(chains)]
        v16 = [(u_ref[b, r, h * DN_DV:(h + 1) * DN_DV] - ws[i][:C]).astype(BF16)
               for i, (b, h) in enumerate(chains)]
        for i, (b, h) in enumerate(chains):
            lo = h * DN_DK
            s_ref[b, h] = s_old[i] * egl_ref[b, g, :, lo:lo + DN_DK] + _dot_tn(kd_ref[b, r, lo:lo + DN_DK], v16[i])
        for i, (b, h) in enumerate(chains):
            lo = h * DN_DV
            o = ws[i][C:] + _dot(attn_ref[b, r, lo:lo + C], v16[i])
            o = o * lax.rsqrt(jnp.mean(o * o, axis=-1, keepdims=True) + EPS) * onorm_ref[...]
            z = za_ref[b, r, lo:lo + DN_DV].astype(F32)
            ua_ref[b, r, lo:lo + DN_DV] = (o * (z * jax.nn.sigmoid(z))).astype(BF16)

    @pl.when(n == pl.num_programs(1) - 1)
    def _():
        sfin_ref[...] = s_ref[...]


def _deltanet(qkv, small, z_a, conv_state, s0, w_conv, alog_v, dtb_v, onorm, tm, bg, G):
    B, T, _ = qkv.shape
    C = min(CHUNK, T)
    N = T // C
    nc = tm // C
    hist_rows = 16
    cs = jnp.pad(conv_state, ((0, 0), (hist_rows - (CONV_W - 1), 0), (0, 0)))
    nt = T // tm
    n_tiles = B * nt

    def cur(s):
        s = jnp.minimum(s, n_tiles - 1)
        return s // nt, s % nt

    def done(s):
        s = jnp.maximum(s - 1, 0)
        return s // nt, s % nt

    tile = lambda s: (*cur(s), 0)
    tile4 = lambda s: (*cur(s), 0, 0)
    prev = lambda s: (cur(s)[0], jnp.maximum(cur(s)[1] * (tm // hist_rows) - 1, 0), 0)
    w, u, qe, kd, attn, egl = pl.pallas_call(
        functools.partial(_dn_prep_kernel, C=C, tiles_per_seq=nt),
        grid=(n_tiles + 1,),
        in_specs=[pl.BlockSpec((1, tm, DN_CONV_CH), tile),
                  pl.BlockSpec((1, hist_rows, DN_CONV_CH), prev),
                  pl.BlockSpec((1, hist_rows, DN_CONV_CH), lambda s: (cur(s)[0], 0, 0)),
                  pl.BlockSpec((1, tm, LANES), tile),
                  _resident((CONV_W, DN_CONV_CH)),
                  _resident((1, LANES)),
                  _resident((1, LANES))],
        out_specs=[pl.BlockSpec((1, nc, C, DN_QK), lambda s: (*done(s), 0, 0)),
                   pl.BlockSpec((1, tm, DN_V), lambda s: (*done(s), 0)),
                   pl.BlockSpec((1, nc, C, DN_QK), tile4),
                   pl.BlockSpec((1, tm, DN_QK), tile),
                   pl.BlockSpec((1, tm, DN_V), tile),
                   pl.BlockSpec((1, nc, 1, DN_QK), tile4)],
        out_shape=[jax.ShapeDtypeStruct((B, N, C, DN_QK), BF16),
                   jax.ShapeDtypeStruct((B, T, DN_V), F32),
                   jax.ShapeDtypeStruct((B, N, C, DN_QK), BF16),
                   jax.ShapeDtypeStruct((B, T, DN_QK), BF16),
                   jax.ShapeDtypeStruct((B, T, DN_V), BF16),
                   jax.ShapeDtypeStruct((B, N, 1, DN_QK), F32)],
        scratch_shapes=[pltpu.VMEM((2, nc * DN_HEADS, C, C), F32),
                        pltpu.VMEM((2, nc * DN_HEADS, C, DN_DK + DN_DV), BF16)],
        compiler_params=_cparams(("arbitrary",)),
        name="dn_prep",
    )(qkv, qkv, cs, small, w_conv, alog_v, dtb_v)

    grp = lambda b, n: (b, n, 0)
    grp4 = lambda b, n: (b, n, 0, 0)
    state = pl.BlockSpec((bg, DN_HEADS, DN_DK, DN_DV), lambda b, n: (b, 0, 0, 0))
    u_a, s_new = pl.pallas_call(
        functools.partial(_dn_scan_kernel, C=C),
        grid=(B // bg, N // G),
        in_specs=[pl.BlockSpec((bg, G, C, DN_QK), grp4),
                  pl.BlockSpec((bg, G, C, DN_QK), grp4),
                  pl.BlockSpec((bg, G * C, DN_V), grp),
                  pl.BlockSpec((bg, G * C, DN_QK), grp),
                  pl.BlockSpec((bg, G * C, DN_V), grp),
                  pl.BlockSpec((bg, G, 1, DN_QK), grp4),
                  pl.BlockSpec((bg, G * C, DN_V), grp),
                  state,
                  _resident((1, DN_DV))],
        out_specs=[pl.BlockSpec((bg, G * C, DN_V), grp), state],
        out_shape=[jax.ShapeDtypeStruct((B, T, DN_V), BF16),
                   jax.ShapeDtypeStruct((B, DN_HEADS, DN_DK, DN_DV), F32)],
        scratch_shapes=[pltpu.VMEM((bg, DN_HEADS, DN_DK, DN_DV), F32)],
        compiler_params=_cparams(("arbitrary", "arbitrary")),
        name="dn_scan",
    )(w, qe, u, kd, attn, egl, z_a, s0, onorm)
    conv_new = qkv[:, T - (CONV_W - 1):, :].astype(F32)
    return u_a, s_new, conv_new


def _head_rms(x, gain_row, width):
    outs = []
    for h in range(x.shape[1] // width):
        xh = x[:, h * width:(h + 1) * width]
        outs.append(xh * lax.rsqrt(jnp.mean(xh * xh, axis=-1, keepdims=True) + EPS) * gain_row)
    return outs


def _rope_inv_freq(shape):
    half = QK_ROPE // 2
    fidx = (lax.broadcasted_iota(jnp.int32, shape, 1) % half).astype(F32)
    return jnp.exp(fidx * (-math.log(ROPE_THETA) / half))


def _chunk_mask(qpos0, kpos0, tq, tk):
    qc = (lax.broadcasted_iota(jnp.int32, (tq, tk), 0) + qpos0) // CHUNK
    kc = (lax.broadcasted_iota(jnp.int32, (tq, tk), 1) + kpos0) // CHUNK
    return kc <= qc


def _attn_prompt_kernel(q_ref, k_ref, v_ref, o_ref, m_ref, l_ref, acc_ref, *, tq, tk, sub):
    i = pl.program_id(2)
    nsub = tq // sub
    ratio = tq // tk
    m_ref[...] = jnp.full_like(m_ref, -1e30)
    l_ref[...] = jnp.zeros_like(l_ref)
    acc_ref[...] = jnp.zeros_like(acc_ref)

    def scores(r, k):
        return _dot_nt(q_ref[0, r * sub:(r + 1) * sub, :], k)

    def softmax_pv(r, sr, v, mask):
        rows = slice(r * sub, (r + 1) * sub)
        if mask is not None:
            sr = jnp.where(mask, sr, -1e30)
        m_old = m_ref[rows, :]
        m_new = jnp.maximum(m_old, jnp.max(sr, axis=-1, keepdims=True))
        alpha = jnp.exp2(m_old - m_new)
        p = jnp.exp2(sr - jnp.tile(m_new, (1, tk // LANES)))
        psum = p[:, 0:LANES]
        for c in range(1, tk // LANES):
            psum = psum + p[:, c * LANES:(c + 1) * LANES]
        l_ref[rows, :] = alpha * l_ref[rows, :] + psum
        acc_ref[rows, :] = alpha * acc_ref[rows, :] + _dot(p.astype(BF16), v)
        m_ref[rows, :] = m_new

    def run(j0, items):
        kv = {}
        for d in sorted({d for d, _, _ in items}):
            start = pl.multiple_of((j0 + d) * tk, tk)
            kv[d] = (k_ref[0, pl.ds(start, tk), :], v_ref[0, pl.ds(start, tk), :])
        s = {n: scores(items[n][1], kv[items[n][0]][0]) for n in range(min(ATTN_LOOKAHEAD, len(items)))}
        for n, (d, r, mask) in enumerate(items):
            ahead = n + ATTN_LOOKAHEAD
            if ahead < len(items):
                s[ahead] = scores(items[ahead][1], kv[items[ahead][0]][0])
            softmax_pv(r, s.pop(n), kv[d][1], mask)

    per_trip = math.gcd(ratio, ATTN_TILES_PER_TRIP)
    full = [(d, r, None) for d in range(per_trip) for r in range(nsub)]

    def body(jj, carry):
        run(jj * per_trip, full)
        return carry

    lax.fori_loop(0, i * (ratio // per_trip), body, 0)
    diag = []
    for d in range(ratio):
        for r in range(nsub):
            q_lo, q_hi = (r * sub) // CHUNK, (r * sub + sub - 1) // CHUNK
            k_lo, k_hi = (d * tk) // CHUNK, (d * tk + tk - 1) // CHUNK
            if k_lo > q_hi:
                continue
            diag.append((d, r, None if k_hi <= q_lo else _chunk_mask(r * sub, d * tk, sub, tk)))
    run(i * ratio, diag)
    l = jnp.sum(l_ref[...], axis=-1, keepdims=True)
    o_ref[0] = (acc_ref[...] / l).astype(BF16)


def _attn_prompt(qcat, kcat, v, tq, tk):
    B, T, _ = v.shape
    return pl.pallas_call(
        functools.partial(_attn_prompt_kernel, tq=tq, tk=tk, sub=ATTN_SUB),
        grid=(B, MLA_HEADS, T // tq),
        in_specs=[pl.BlockSpec((1, tq, QK_CAT), lambda b, h, i: (b, i, h)),
                  pl.BlockSpec((1, T, QK_CAT), lambda b, h, i: (b, 0, h)),
                  pl.BlockSpec((1, T, V_HEAD), lambda b, h, i: (b, 0, h))],
        out_specs=pl.BlockSpec((1, tq, V_HEAD), lambda b, h, i: (b, i, h)),
        out_shape=jax.ShapeDtypeStruct((B, T, MLA_V), BF16),
        scratch_shapes=[pltpu.VMEM((tq, LANES), F32), pltpu.VMEM((tq, LANES), F32),
                        pltpu.VMEM((tq, V_HEAD), F32)],
        compiler_params=_cparams(("arbitrary", "arbitrary", "arbitrary")),
        name="attn_prompt",
    )(qcat, kcat, v)


def _attn_sample_kernel(q_ref, kn_ref, vn_ref, ckv_ref, kr_ref, kng_ref, wuk_ref, wuv_ref, o_ref,
                        m_ref, l_ref, acc_ref, *, tk):
    T = q_ref.shape[1]
    P = ckv_ref.shape[1]
    R = MLA_HEADS * T
    q = q_ref[0]
    row_head = lax.broadcasted_iota(jnp.int32, (R, 1), 0) // T
    q_chunk = (lax.broadcasted_iota(jnp.int32, (R, 1), 0) % T + P) // CHUNK

    def stacked(x, width):
        lane_head = lax.broadcasted_iota(jnp.int32, (R, x.shape[1]), 1) // width
        return jnp.where(lane_head == row_head, jnp.concatenate([x] * MLA_HEADS, axis=0), jnp.zeros((), x.dtype))

    q_nope = stacked(jnp.concatenate([q[:, h * QK_CAT:h * QK_CAT + QK_NOPE] for h in range(MLA_HEADS)], axis=1),
                     QK_NOPE)
    q_rope = jnp.concatenate([q[:, h * QK_CAT + QK_NOPE:h * QK_CAT + QK_HEAD] for h in range(MLA_HEADS)],
                             axis=0)

    def update(s, k_chunk):
        s = jnp.where(k_chunk <= q_chunk, s, -1e30)
        m_old = m_ref[...]
        m_new = jnp.maximum(m_old, jnp.max(s, axis=-1, keepdims=True))
        alpha = jnp.exp2(m_old - m_new)
        p = jnp.exp2(s - m_new)
        l_ref[...] = alpha * l_ref[...] + jnp.sum(p, axis=-1, keepdims=True)
        m_ref[...] = m_new
        return alpha, p.astype(BF16)

    m_ref[...] = jnp.full_like(m_ref, -1e30)
    l_ref[...] = jnp.zeros_like(l_ref)
    acc_ref[...] = jnp.zeros_like(acc_ref)

    def body(c, carry):
        start = pl.multiple_of(c * tk, tk)
        ckv16 = ckv_ref[0, pl.ds(start, tk), :].astype(BF16)
        kn = _head_rms(_dot(ckv16, wuk_ref[...]), kng_ref[...], QK_NOPE)
        kn16 = jnp.concatenate([x.astype(BF16) for x in kn], axis=1)
        s = _dot_nt(q_nope, kn16) + _dot_nt(q_rope, kr_ref[0, pl.ds(start, tk), :].astype(BF16))
        k_chunk = (lax.broadcasted_iota(jnp.int32, (1, tk), 1) + start) // CHUNK
        alpha, p16 = update(s, k_chunk)
        acc_ref[...] = alpha * acc_ref[...] + _dot(p16, ckv16)
        return carry

    lax.fori_loop(0, P // tk, body, 0)

    s_new = _dot_nt(stacked(q, QK_CAT), kn_ref[0])
    alpha, p16 = update(s_new, (lax.broadcasted_iota(jnp.int32, (1, T), 1) + P) // CHUNK)
    pc16 = (alpha * acc_ref[...]).astype(BF16)
    inv_l = 1.0 / l_ref[...]
    for h in range(MLA_HEADS):
        rows = slice(h * T, (h + 1) * T)
        lanes = slice(h * V_HEAD, (h + 1) * V_HEAD)
        o = _dot(pc16[rows], wuv_ref[:, lanes]) + _dot(p16[rows], vn_ref[0, :, lanes])
        o_ref[0, :, lanes] = (o * inv_l[rows]).astype(BF16)


def _attn_sample(qcat, kcat_new, v_new, past_ckv, past_kr, kn_gain, w_uk16, w_uv16, tk):
    B, T, _ = v_new.shape
    P = past_ckv.shape[1]
    R = MLA_HEADS * T
    perb = lambda b: (b, 0, 0)
    return pl.pallas_call(
        functools.partial(_attn_sample_kernel, tk=tk),
        grid=(B,),
        in_specs=[pl.BlockSpec((1, T, MLA_HEADS * QK_CAT), perb),
                  pl.BlockSpec((1, T, MLA_HEADS * QK_CAT), perb),
                  pl.BlockSpec((1, T, MLA_V), perb),
                  pl.BlockSpec((1, P, KV_RANK), perb),
                  pl.BlockSpec((1, P, QK_ROPE), perb),
                  _resident((1, QK_NOPE)),
                  _resident((KV_RANK, MLA_HEADS * QK_NOPE)),
                  _resident((KV_RANK, MLA_V))],
        out_specs=pl.BlockSpec((1, T, MLA_V), perb),
        out_shape=jax.ShapeDtypeStruct((B, T, MLA_V), BF16),
        scratch_shapes=[pltpu.VMEM((R, 1), F32), pltpu.VMEM((R, 1), F32), pltpu.VMEM((R, KV_RANK), F32)],
        compiler_params=_cparams(("arbitrary",)),
        name="attn_sample",
    )(qcat, kcat_new, v_new, past_ckv, past_kr, kn_gain, w_uk16, w_uv16)


def _out_kernel(x_ref, mod_ref, ua_ref, ob_ref, zb_ref, ga_ref, gb_ref, wdn_ref, wmla_ref, wout_ref, y_ref):
    bb, tm, d = x_ref.shape
    rows = bb * tm
    zb = zb_ref[...].astype(F32)
    ub = (ob_ref[...].astype(F32) * (zb * jax.nn.sigmoid(zb))).astype(BF16).reshape(rows, d)
    ya = _dot(ua_ref[...].reshape(rows, d), wdn_ref[...])
    yb = _dot(ub, wmla_ref[...])
    ga = jax.nn.sigmoid(ga_ref[...].astype(F32)).reshape(rows, d)
    gb = jax.nn.sigmoid(gb_ref[...].astype(F32)).reshape(rows, d)
    merged = (ga * ya + gb * yb).astype(BF16)
    out = _dot(merged, wout_ref[...]).reshape(bb, tm, d)
    gate = mod_ref[:, :, 2 * d:3 * d]
    y_ref[...] = x_ref[...] + gate * out


def _out_proj(x, mod3, u_a, o_b, z_b, g_a, g_b, w_dn16, w_mla16, w_out16, bb, tm):
    B, T, _ = x.shape
    row = lambda b, t: (b, t, 0)
    act = pl.BlockSpec((bb, tm, D_MODEL), row)
    return pl.pallas_call(
        _out_kernel,
        grid=(B // bb, T // tm),
        in_specs=[act, pl.BlockSpec((bb, 1, 3 * D_MODEL), lambda b, t: (b, 0, 0)), act, act, act, act, act,
                  _resident((D_MODEL, D_MODEL)), _resident((D_MODEL, D_MODEL)), _resident((D_MODEL, D_MODEL))],
        out_specs=act,
        out_shape=jax.ShapeDtypeStruct((B, T, D_MODEL), F32),
        compiler_params=_cparams(("arbitrary", "arbitrary")),
        name="out_proj",
    )(x, mod3, u_a, o_b, z_b, g_a, g_b, w_dn16, w_mla16, w_out16)


def _lane_vec(v, off):
    return jnp.zeros((1, LANES), F32).at[0, off:off + v.shape[0]].set(v)


def _tiles(B, T, cached):
    if cached:
        whole = (B, T)
        return dict(proj=whole, out=whole, dn=dict(tm=T, bg=2, G=1), attn_tk=1024)
    return dict(proj=(1, 256), out=(1, 1024), dn=dict(tm=256, bg=B, G=8), attn=(2048, 512))


def _layer(x, mod, conv_state, s0, past, prm, q_off):
    B, T, _ = x.shape
    tiles = _tiles(B, T, past is not None)
    mod3 = mod.reshape(B, 1, 3 * D_MODEL)
    mla = (prm["q_nope_norm"], prm["qr_gain"], prm["kv_norm"], prm["kr_gain"], prm["k_nope_norm"],
           prm["w_uk"], prm["w_uv"])
    small, qkv, z_a, z_b, g_a, g_b, qcat, kcat, v, ckv_new, kr_new = _in_proj(
        x, mod3, prm["norm_gain"], prm["w_pack"], mla, *tiles["proj"], q_off)
    u_a, s_new, conv_new = _deltanet(qkv, small, z_a, conv_state, s0, prm["w_conv"], prm["alog_v"],
                                     prm["dtb_v"], prm["dn_out_norm"], **tiles["dn"])
    if past is None:
        o_b = _attn_prompt(qcat, kcat, v, *tiles["attn"])
    else:
        past_ckv, past_kr = past
        o_b = _attn_sample(qcat, kcat, v, past_ckv, past_kr, prm["k_nope_norm"], prm["w_uk"], prm["w_uv"],
                           tiles["attn_tk"])
    y = _out_proj(x, mod3, u_a, o_b, z_b, g_a, g_b, prm["w_o_dn"], prm["w_o_mla"], prm["w_out"], *tiles["out"])
    return y, conv_new, s_new, ckv_new, kr_new


def kernel(x_prompt, x_sample, c_prompt, c_sample, cache_ckv, cache_krope, state_delta, state_conv, norm_gain, w_ada, b_ada, w_in, w_conv, a_log, dt_bias, dn_out_norm, q_nope_norm, q_rope_norm, k_nope_norm, k_rope_norm, kv_norm, w_uk, w_uv, w_o_dn, w_o_mla, w_out):
    depth = w_in.shape[0]
    assert depth == 1, "single-layer configuration"
    l = 0
    B, T, _ = x_prompt.shape
    Bs, Ts, _ = x_sample.shape
    past_len = cache_ckv.shape[2]

    row = lambda v: v.reshape(1, -1).astype(F32)
    prm = dict(
        norm_gain=row(norm_gain[l]),
        w_pack=_pack_w_in(w_in[l]),
        w_conv=w_conv[l],
        alog_v=_lane_vec(a_log[l], ALPHA_OFF),
        dtb_v=_lane_vec(dt_bias[l], ALPHA_OFF),
        dn_out_norm=row(dn_out_norm[l]),
        q_nope_norm=row(q_nope_norm[l]),
        qr_gain=jnp.tile(row(q_rope_norm[l]), (1, LANES // QK_ROPE)),
        kv_norm=row(kv_norm[l]),
        kr_gain=_lane_vec(k_rope_norm[l], KR_OFF),
        k_nope_norm=row(k_nope_norm[l]),
        w_uk=w_uk[l].astype(BF16),
        w_uv=w_uv[l].astype(BF16),
        w_o_dn=w_o_dn[l].astype(BF16),
        w_o_mla=w_o_mla[l].astype(BF16),
        w_out=w_out[l].astype(BF16),
    )

    rows = B + Bs
    rows_pad = -(-rows // 8) * 8
    c_all = jnp.concatenate([c_prompt, c_sample, jnp.zeros((rows_pad - rows, D_MODEL), F32)], axis=0)
    mod = _ada(c_all, w_ada[l], b_ada[l].reshape(1, -1))

    zeros_conv = jnp.zeros((B, CONV_W - 1, DN_CONV_CH), F32)
    zeros_state = jnp.zeros((B, DN_HEADS, DN_DK, DN_DV), F32)
    yp, cvp, sdp, kvp, krp = _layer(x_prompt, mod[:B], zeros_conv, zeros_state, None, prm, q_off=0)
    ys, cvs, sds, kvs, krs = _layer(x_sample, mod[B:rows], state_conv[l], state_delta[l],
                                    (cache_ckv[l], cache_krope[l]), prm, q_off=past_len)
    st = lambda a: a[None]
    return (yp, ys, st(kvp), st(krp), st(sdp), st(cvp), st(kvs), st(krs), st(sds), st(cvs))
```

```python
import functools
import math

import jax
import jax.numpy as jnp
from jax import lax
from jax.experimental import pallas as pl
from jax.experimental.pallas import tpu as pltpu

D_MODEL = 1024
CHUNK = 64
EPS = 1e-6
DN_HEADS = 8
DN_DK = 128
DN_DV = 128
DN_QK = DN_HEADS * DN_DK
DN_V = DN_HEADS * DN_DV
DN_CONV_CH = 2 * DN_QK + DN_V
CONV_W = 4
MLA_HEADS = 8
QK_NOPE = 128
QK_ROPE = 64
QK_HEAD = QK_NOPE + QK_ROPE
V_HEAD = 128
KV_RANK = 512
MLA_Q = MLA_HEADS * QK_HEAD
MLA_V = MLA_HEADS * V_HEAD
ROPE_THETA = 10000.0

LANES = 128
QK_CAT = 256
CONV_ROWS = 128
ATTN_SUB = 512
ATTN_LOOKAHEAD = 1
ATTN_TILES_PER_TRIP = 4
KR_OFF = 0
BETA_OFF = QK_ROPE
ALPHA_OFF = QK_ROPE + DN_HEADS
VMEM_LIMIT = 56 * 1024 * 1024

F32 = jnp.float32
BF16 = jnp.bfloat16
HI = lax.Precision.HIGHEST


def _dot(a, b):
    return jnp.dot(a, b, preferred_element_type=F32)


def _dot_nt(a, b, precision=None):
    return lax.dot_general(a, b, (((1,), (1,)), ((), ())), preferred_element_type=F32, precision=precision)


def _dot_tn(a, b):
    return lax.dot_general(a, b, (((0,), (0,)), ((), ())), preferred_element_type=F32)


def _cparams(sem):
    return pltpu.CompilerParams(dimension_semantics=sem, vmem_limit_bytes=VMEM_LIMIT)


def _resident(shape):
    nd = len(shape)
    return pl.BlockSpec(shape, lambda *_: (0,) * nd, pipeline_mode=pl.Buffered(1))


def _ada_kernel(c_ref, w_ref, b_ref, o_ref):
    o_ref[...] = jnp.dot(c_ref[...], w_ref[...], preferred_element_type=F32, precision=HI) + b_ref[...]


def _ada(c_all, w_ada, b_ada):
    rows = c_all.shape[0]
    tn = 1024
    return pl.pallas_call(
        _ada_kernel,
        grid=(3 * D_MODEL // tn,),
        in_specs=[pl.BlockSpec((rows, D_MODEL), lambda j: (0, 0)),
                  pl.BlockSpec((D_MODEL, tn), lambda j: (0, j)),
                  pl.BlockSpec((1, tn), lambda j: (0, j))],
        out_specs=pl.BlockSpec((rows, tn), lambda j: (0, j)),
        out_shape=jax.ShapeDtypeStruct((rows, 3 * D_MODEL), F32),
        compiler_params=_cparams(("arbitrary",)),
        name="ada",
    )(c_all, w_ada, b_ada)


_PROJ_WIDTHS = (LANES, KV_RANK, MLA_HEADS * QK_ROPE, MLA_HEADS * QK_NOPE, DN_CONV_CH, DN_V, MLA_V, D_MODEL, D_MODEL)


def _pack_w_in(w_in):
    o = 0
    qkv = w_in[:, o:o + DN_CONV_CH]; o += DN_CONV_CH
    z_a = w_in[:, o:o + DN_V]; o += DN_V
    beta = w_in[:, o:o + DN_HEADS]; o += DN_HEADS
    alpha = w_in[:, o:o + DN_HEADS]; o += DN_HEADS
    q = w_in[:, o:o + MLA_Q].reshape(D_MODEL, MLA_HEADS, QK_HEAD); o += MLA_Q
    ckv = w_in[:, o:o + KV_RANK]; o += KV_RANK
    kr = w_in[:, o:o + QK_ROPE]; o += QK_ROPE
    z_b = w_in[:, o:o + MLA_V]; o += MLA_V
    g_a = w_in[:, o:o + D_MODEL]; o += D_MODEL
    g_b = w_in[:, o:o + D_MODEL]
    qn = q[:, :, :QK_NOPE].reshape(D_MODEL, MLA_HEADS * QK_NOPE)
    qr = q[:, :, QK_NOPE:].reshape(D_MODEL, MLA_HEADS * QK_ROPE)
    pad = jnp.zeros((D_MODEL, LANES - QK_ROPE - 2 * DN_HEADS), w_in.dtype)
    small = jnp.concatenate([kr, beta, alpha, pad], axis=1)
    return tuple(w.astype(BF16) for w in (small, ckv, qr, qn, qkv, z_a, z_b, g_a, g_b))


def _in_proj_kernel(x_ref, mod_ref, gain_ref,
                    w_small, w_ckv, w_qr, w_qn, w_qkv, w_za, w_zb, w_ga, w_gb,
                    qng_ref, qrg_ref, kvg_ref, krg_ref, kng_ref, wuk_ref, wuv_ref,
                    small_ref, qkv_ref, za_ref, zb_ref, ga_ref, gb_ref,
                    qcat_ref, kcat_ref, v_ref, ckvn_ref, krn_ref, rope_ref, *, q_off):
    bb, tm, d = x_ref.shape
    rows = bb * tm

    @pl.when((pl.program_id(0) == 0) & (pl.program_id(1) == 0))
    def _():
        off = (lax.broadcasted_iota(jnp.int32, (rows, LANES), 0) % tm).astype(F32) * _rope_inv_freq((rows, LANES))
        rope_ref[0] = jnp.cos(off)
        rope_ref[1] = jnp.sin(off)

    x = x_ref[...]
    ms = jnp.mean(x * x, axis=-1, keepdims=True)
    y = x * lax.rsqrt(ms + EPS) * gain_ref[...]
    shift = mod_ref[:, :, 0:d]
    scale = mod_ref[:, :, d:2 * d]
    h = (y * (1.0 + scale) + shift).astype(BF16).reshape(rows, d)

    def project(w_ref, o_ref):
        o_ref[...] = _dot(h, w_ref[...]).astype(o_ref.dtype).reshape(o_ref.shape)

    sm = _dot(h, w_small[...])
    small_ref[...] = sm.reshape(bb, tm, LANES)
    ckv = _dot(h, w_ckv[...])
    qr_all = _dot(h, w_qr[...])
    qn_all = _dot(h, w_qn[...])
    project(w_qkv, qkv_ref)

    t0 = pl.program_id(1) * tm
    qscale = QK_HEAD ** -0.5 * math.log2(math.e)
    base = (t0 + q_off).astype(F32) * _rope_inv_freq((1, LANES))
    cos_a, sin_a = jnp.cos(base), jnp.sin(base)
    cos = cos_a * rope_ref[0] - sin_a * rope_ref[1]
    sin = sin_a * rope_ref[0] + cos_a * rope_ref[1]
    lane = lax.broadcasted_iota(jnp.int32, (rows, LANES), 1)
    low_half = lane < QK_ROPE
    first = (lane % QK_ROPE) < (QK_ROPE // 2)

    def rope(z):
        rot = jnp.where(first, -pltpu.roll(z, LANES - QK_ROPE // 2, 1), pltpu.roll(z, QK_ROPE // 2, 1))
        return z * cos + rot * sin

    def rms64(z):
        zz = z * z
        s_lo = jnp.sum(jnp.where(low_half, zz, 0.0), axis=-1, keepdims=True)
        s_hi = jnp.sum(jnp.where(low_half, 0.0, zz), axis=-1, keepdims=True)
        return lax.rsqrt(jnp.where(low_half, s_lo, s_hi) * (1.0 / QK_ROPE) + EPS)

    ckvn = ckv * lax.rsqrt(jnp.mean(ckv * ckv, axis=-1, keepdims=True) + EPS) * kvg_ref[...]
    ckvn_ref[...] = ckvn.reshape(bb, tm, KV_RANK)
    ckvn16 = ckvn.astype(BF16)
    kr = rope(sm * rms64(sm) * krg_ref[...])
    krn_ref[...] = kr[:, :QK_ROPE].reshape(bb, tm, QK_ROPE)
    kr_pad16 = jnp.where(low_half, kr, 0.0).astype(BF16).reshape(bb, tm, LANES)

    project(w_za, za_ref)
    k_raw = _dot(ckvn16, wuk_ref[...])
    v_ref[...] = _dot(ckvn16, wuv_ref[...]).astype(BF16).reshape(bb, tm, MLA_V)
    project(w_zb, zb_ref)

    qn = _head_rms(qn_all, qng_ref[...] * qscale, QK_NOPE)
    for c in range(MLA_HEADS // 2):
        z = qr_all[:, c * LANES:(c + 1) * LANES]
        z = rope(z * rms64(z) * qrg_ref[...]) * qscale
        even = jnp.where(low_half, z, 0.0)
        odd = jnp.where(low_half, pltpu.roll(z, QK_ROPE, 1), 0.0)
        for hh, part in ((2 * c, even), (2 * c + 1, odd)):
            qcat_ref[:, :, hh * QK_CAT:hh * QK_CAT + QK_NOPE] = qn[hh].astype(BF16).reshape(bb, tm, QK_NOPE)
            qcat_ref[:, :, hh * QK_CAT + QK_NOPE:(hh + 1) * QK_CAT] = part.astype(BF16).reshape(bb, tm, LANES)
    project(w_ga, ga_ref)

    kn = _head_rms(k_raw, kng_ref[...], QK_NOPE)
    for hh in range(MLA_HEADS):
        kcat_ref[:, :, hh * QK_CAT:hh * QK_CAT + QK_NOPE] = kn[hh].astype(BF16).reshape(bb, tm, QK_NOPE)
        kcat_ref[:, :, hh * QK_CAT + QK_NOPE:(hh + 1) * QK_CAT] = kr_pad16
    project(w_gb, gb_ref)


def _in_proj(x, mod3, gain, w_pack, mla, bb, tm, q_off):
    B, T, _ = x.shape
    row = lambda b, t: (b, t, 0)
    outs = ((LANES, F32), (DN_CONV_CH, BF16), (DN_V, BF16), (MLA_V, BF16), (D_MODEL, BF16), (D_MODEL, BF16),
            (MLA_HEADS * QK_CAT, BF16), (MLA_HEADS * QK_CAT, BF16), (MLA_V, BF16), (KV_RANK, F32), (QK_ROPE, F32))
    return pl.pallas_call(
        functools.partial(_in_proj_kernel, q_off=q_off),
        grid=(B // bb, T // tm),
        in_specs=[pl.BlockSpec((bb, tm, D_MODEL), row),
                  pl.BlockSpec((bb, 1, 3 * D_MODEL), lambda b, t: (b, 0, 0)),
                  _resident((1, D_MODEL))]
        + [_resident((D_MODEL, w)) for w in _PROJ_WIDTHS]
        + [_resident((1, QK_NOPE)), _resident((1, LANES)), _resident((1, KV_RANK)), _resident((1, LANES)),
           _resident((1, QK_NOPE)), _resident((KV_RANK, MLA_HEADS * QK_NOPE)), _resident((KV_RANK, MLA_V))],
        out_specs=[pl.BlockSpec((bb, tm, w), row) for w, _ in outs],
        out_shape=[jax.ShapeDtypeStruct((B, T, w), dt) for w, dt in outs],
        scratch_shapes=[pltpu.VMEM((2, bb * tm, LANES), F32)],
        compiler_params=_cparams(("arbitrary", "arbitrary")),
        name="in_proj",
    )(x, mod3, gain, *w_pack, *mla)


def _softplus(x):
    return jnp.maximum(x, 0.0) + jnp.log(1.0 + jnp.exp(-jnp.abs(x)))


def _dn_prep_kernel(qkv_ref, prev_ref, cs_ref, small_ref, wconv_ref, alog_ref, dtb_ref,
                    w_ref, u_ref, qe_ref, kd_ref, attn_ref, egl_ref, l_scr, rhs_scr, *, C, tiles_per_seq):
    step = pl.program_id(0)
    tile_idx = jnp.minimum(step, pl.num_programs(0) - 2) % tiles_per_seq
    tm = qkv_ref.shape[1]
    nc = tm // C
    pad = prev_ref.shape[1]

    wr = step % 2
    rd = 1 - wr

    @pl.when(step == 0)
    def _():
        l_scr[...] = jnp.zeros_like(l_scr)
        rhs_scr[...] = jnp.zeros_like(rhs_scr)

    x16 = qkv_ref[0]
    hist = jnp.where(tile_idx == 0, cs_ref[0], prev_ref[0].astype(F32))
    hist_hi = hist.astype(BF16)
    rem = hist - hist_hi.astype(F32)
    hist_mid = rem.astype(BF16)
    hist_lo = (rem - hist_mid.astype(F32)).astype(BF16)
    n_sh = CONV_W - 1
    rs = min(tm, CONV_ROWS)

    def shifted_taps(pieces, xs):
        npc = len(pieces)
        full16 = jnp.concatenate(list(pieces) + [xs], axis=0)
        srow = lax.broadcasted_iota(jnp.int32, (n_sh * rs, npc * pad + rs), 0)
        scol = lax.broadcasted_iota(jnp.int32, (n_sh * rs, npc * pad + rs), 1)
        src = srow % rs + srow // rs + pad - n_sh
        sel = scol == src + (npc - 1) * pad
        for p in range(npc - 1):
            sel = sel | ((scol == src + p * pad) & (scol < (p + 1) * pad))
        return _dot(sel.astype(BF16), full16)

    conv_parts = []
    for j in range(tm // rs):
        xs = x16[j * rs:(j + 1) * rs]
        pieces = (hist_hi, hist_mid, hist_lo) if j == 0 else (x16[j * rs - pad:j * rs],)
        shifted = shifted_taps(pieces, xs)
        part = xs.astype(F32) * wconv_ref[n_sh:CONV_W, :]
        for i in range(n_sh):
            part = part + shifted[i * rs:(i + 1) * rs, :] * wconv_ref[i:i + 1, :]
        conv_parts.append(part)
    conv = conv_parts[0] if len(conv_parts) == 1 else jnp.concatenate(conv_parts, axis=0)
    act = conv * jax.nn.sigmoid(conv)

    sm = small_ref[0]
    beta_all = jax.nn.sigmoid(sm)
    g_all = -jnp.exp(alog_ref[...]) * _softplus(sm + dtb_ref[...])
    rt = lax.broadcasted_iota(jnp.int32, (tm, tm), 0)
    ct = lax.broadcasted_iota(jnp.int32, (tm, tm), 1)
    chunk_tri = ((rt // C == ct // C) & (rt >= ct)).astype(F32)
    gcum = jnp.dot(chunk_tri, g_all, preferred_element_type=F32, precision=HI)
    sel = (lax.broadcasted_iota(jnp.int32, (DN_HEADS, LANES), 1)
           == lax.broadcasted_iota(jnp.int32, (DN_HEADS, LANES), 0) + ALPHA_OFF).astype(F32)
    gcum_t = _dot_nt(sel, gcum, precision=HI)

    def per_head_lanes(x, off):
        hi = x.astype(BF16)
        r1 = x - hi.astype(F32)
        mid = r1.astype(BF16)
        lo = (r1 - mid.astype(F32)).astype(BF16)
        erow = lax.broadcasted_iota(jnp.int32, (3 * LANES, DN_QK), 0) % LANES
        ecol = lax.broadcasted_iota(jnp.int32, (3 * LANES, DN_QK), 1) // DN_DK
        return _dot(jnp.concatenate([hi, mid, lo], axis=1), (erow == ecol + off).astype(BF16))

    g_b = per_head_lanes(gcum, ALPHA_OFF)
    beta_b = per_head_lanes(beta_all, BETA_OFF)
    glast_b = jnp.concatenate(
        [jnp.broadcast_to(g_b[c * C + C - 1:(c + 1) * C, :], (C, DN_QK)) for c in range(nc)], axis=0)
    eg_b = jnp.exp(g_b)
    kdf_b = jnp.exp(glast_b - g_b)
    for c in range(nc):
        egl_ref[0, c] = jnp.exp(g_b[c * C + C - 1:(c + 1) * C, :])

    ri = lax.broadcasted_iota(jnp.int32, (C, C), 0)
    ci = lax.broadcasted_iota(jnp.int32, (C, C), 1)
    tri_incl = ri >= ci
    tri_strict = ri > ci
    eye = (ri == ci).astype(F32)
    pair_masks = []
    m = 1
    while m < C:
        pair_masks.append((ri // (2 * m) == ci // (2 * m)) & (ri // m != ci // m))
        m *= 2

    heads = range(DN_HEADS)
    hl = lambda h: slice(h * DN_DK, (h + 1) * DN_DK)
    items = [(c, h) for c in range(nc) for h in heads]
    rows = lambda c: slice(c * C, (c + 1) * C)

    l_prev = [l_scr[rd, i] for i in range(len(items))]
    pinv = [eye - jnp.where(pair_masks[0], l, 0.0) for l in l_prev]
    for mask in pair_masks[1:]:
        p16 = [p.astype(BF16) for p in pinv]
        tmp = [_dot(p16[i], jnp.where(mask, l_prev[i], 0.0).astype(BF16)).astype(BF16) for i in range(len(items))]
        pinv = [pinv[i] - _dot(tmp[i], p16[i]) for i in range(len(items))]
    for i, (c, h) in enumerate(items):
        wu = _dot(pinv[i].astype(BF16), rhs_scr[rd, i])
        w_ref[0, c, :, hl(h)] = wu[:, :DN_DK].astype(BF16)
        u_ref[0, rows(c), hl(h)] = wu[:, DN_DK:]

    qn, kn = [], []
    for h in heads:
        qh = act[:, h * DN_DK:(h + 1) * DN_DK]
        kh = act[:, DN_QK + h * DN_DK:DN_QK + (h + 1) * DN_DK]
        qn.append(qh * lax.rsqrt(jnp.sum(qh * qh, axis=-1, keepdims=True) + EPS) * (DN_DK ** -0.5))
        kn.append(kh * lax.rsqrt(jnp.sum(kh * kh, axis=-1, keepdims=True) + EPS))

    k16 = [kn[h].astype(BF16) for h in heads]
    q16 = [qn[h].astype(BF16) for h in heads]
    kb = [kn[h] * beta_b[:, hl(h)] for h in heads]
    kb16 = [kb[h].astype(BF16) for h in heads]
    kbe16 = [(kb[h] * eg_b[:, hl(h)]).astype(BF16) for h in heads]
    vb16 = [(act[:, 2 * DN_QK + h * DN_DV:2 * DN_QK + (h + 1) * DN_DV] * beta_b[:, hl(h)]).astype(BF16)
            for h in heads]
    qe16 = [(qn[h] * eg_b[:, hl(h)]).astype(BF16) for h in heads]
    for h in heads:
        kd_ref[0, :, hl(h)] = (kn[h] * kdf_b[:, hl(h)]).astype(BF16)

    attn_ref[...] = jnp.zeros_like(attn_ref)
    decay, qk = [], []
    for i, (c, h) in enumerate(items):
        r = rows(c)
        gc = g_b[r, h * DN_DK:h * DN_DK + C]
        decay.append(jnp.exp(jnp.where(tri_incl, gc - gcum_t[h:h + 1, r], -1e30)))
        rhs_scr[wr, i] = jnp.concatenate([kbe16[h][r], vb16[h][r]], axis=1)
        qe_ref[0, c, :, hl(h)] = qe16[h][r]
        qk.append(_dot_nt(jnp.concatenate([kb16[h][r], q16[h][r]], axis=0), k16[h][r]))

    for i, (c, h) in enumerate(items):
        l_scr[wr, i] = jnp.where(tri_strict, qk[i][:C] * decay[i], 0.0)
        attn_ref[0, rows(c), h * DN_DK:h * DN_DK + C] = (qk[i][C:] * decay[i]).astype(BF16)


def _dn_scan_kernel(w_ref, qe_ref, u_ref, kd_ref, attn_ref, egl_ref, za_ref, s0_ref, onorm_ref,
                    ua_ref, sfin_ref, s_ref, *, C):
    n = pl.program_id(1)
    bg, G = w_ref.shape[0], w_ref.shape[1]

    @pl.when(n == 0)
    def _():
        s_ref[...] = s0_ref[...]

    chains = [(b, h) for b in range(bg) for h in range(DN_HEADS)]
    for g in range(G):
        r = slice(g * C, (g + 1) * C)
        s_old = [s_ref[b, h] for b, h in chains]
        s16 = [s.astype(BF16) for s in s_old]
        ws = [_dot(jnp.concatenate([w_ref[b, g, :, h * DN_DK:(h + 1) * DN_DK],
                                    qe_ref[b, g, :, h * DN_DK:(h + 1) * DN_DK]], axis=0), s16[i])
              for i, (b, h) in enumerate(chains)]
        v16 = [(u_ref[b, r, h * DN_DV:(h + 1) * DN_DV] - ws[i][:C]).astype(BF16)
               for i, (b, h) in enumerate(chains)]
        for i, (b, h) in enumerate(chains):
            lo = h * DN_DK
            s_ref[b, h] = s_old[i] * egl_ref[b, g, :, lo:lo + DN_DK] + _dot_tn(kd_ref[b, r, lo:lo + DN_DK], v16[i])
        for i, (b, h) in enumerate(chains):
            lo = h * DN_DV
            o = ws[i][C:] + _dot(attn_ref[b, r, lo:lo + C], v16[i])
            o = o * lax.rsqrt(jnp.mean(o * o, axis=-1, keepdims=True) + EPS) * onorm_ref[...]
            z = za_ref[b, r, lo:lo + DN_DV].astype(F32)
            ua_ref[b, r, lo:lo + DN_DV] = (o * (z * jax.nn.sigmoid(z))).astype(BF16)

    @pl.when(n == pl.num_programs(1) - 1)
    def _():
        sfin_ref[...] = s_ref[...]


def _deltanet(qkv, small, z_a, conv_state, s0, w_conv, alog_v, dtb_v, onorm, tm, bg, G):
    B, T, _ = qkv.shape
    C = min(CHUNK, T)
    N = T // C
    nc = tm // C
    hist_rows = 16
    cs = jnp.pad(conv_state, ((0, 0), (hist_rows - (CONV_W - 1), 0), (0, 0)))
    nt = T // tm
    n_tiles = B * nt

    def cur(s):
        s = jnp.minimum(s, n_tiles - 1)
        return s // nt, s % nt

    def done(s):
        s = jnp.maximum(s - 1, 0)
        return s // nt, s % nt

    tile = lambda s: (*cur(s), 0)
    tile4 = lambda s: (*cur(s), 0, 0)
    prev = lambda s: (cur(s)[0], jnp.maximum(cur(s)[1] * (tm // hist_rows) - 1, 0), 0)
    w, u, qe, kd, attn, egl = pl.pallas_call(
        functools.partial(_dn_prep_kernel, C=C, tiles_per_seq=nt),
        grid=(n_tiles + 1,),
        in_specs=[pl.BlockSpec((1, tm, DN_CONV_CH), tile),
                  pl.BlockSpec((1, hist_rows, DN_CONV_CH), prev),
                  pl.BlockSpec((1, hist_rows, DN_CONV_CH), lambda s: (cur(s)[0], 0, 0)),
                  pl.BlockSpec((1, tm, LANES), tile),
                  _resident((CONV_W, DN_CONV_CH)),
                  _resident((1, LANES)),
                  _resident((1, LANES))],
        out_specs=[pl.BlockSpec((1, nc, C, DN_QK), lambda s: (*done(s), 0, 0)),
                   pl.BlockSpec((1, tm, DN_V), lambda s: (*done(s), 0)),
                   pl.BlockSpec((1, nc, C, DN_QK), tile4),
                   pl.BlockSpec((1, tm, DN_QK), tile),
                   pl.BlockSpec((1, tm, DN_V), tile),
                   pl.BlockSpec((1, nc, 1, DN_QK), tile4)],
        out_shape=[jax.ShapeDtypeStruct((B, N, C, DN_QK), BF16),
                   jax.ShapeDtypeStruct((B, T, DN_V), F32),
                   jax.ShapeDtypeStruct((B, N, C, DN_QK), BF16),
                   jax.ShapeDtypeStruct((B, T, DN_QK), BF16),
                   jax.ShapeDtypeStruct((B, T, DN_V), BF16),
                   jax.ShapeDtypeStruct((B, N, 1, DN_QK), F32)],
        scratch_shapes=[pltpu.VMEM((2, nc * DN_HEADS, C, C), F32),
                        pltpu.VMEM((2, nc * DN_HEADS, C, DN_DK + DN_DV), BF16)],
        compiler_params=_cparams(("arbitrary",)),
        name="dn_prep",
    )(qkv, qkv, cs, small, w_conv, alog_v, dtb_v)

    grp = lambda b, n: (b, n, 0)
    grp4 = lambda b, n: (b, n, 0, 0)
    state = pl.BlockSpec((bg, DN_HEADS, DN_DK, DN_DV), lambda b, n: (b, 0, 0, 0))
    u_a, s_new = pl.pallas_call(
        functools.partial(_dn_scan_kernel, C=C),
        grid=(B // bg, N // G),
        in_specs=[pl.BlockSpec((bg, G, C, DN_QK), grp4),
                  pl.BlockSpec((bg, G, C, DN_QK), grp4),
                  pl.BlockSpec((bg, G * C, DN_V), grp),
                  pl.BlockSpec((bg, G * C, DN_QK), grp),
                  pl.BlockSpec((bg, G * C, DN_V), grp),
                  pl.BlockSpec((bg, G, 1, DN_QK), grp4),
                  pl.BlockSpec((bg, G * C, DN_V), grp),
                  state,
                  _resident((1, DN_DV))],
        out_specs=[pl.BlockSpec((bg, G * C, DN_V), grp), state],
        out_shape=[jax.ShapeDtypeStruct((B, T, DN_V), BF16),
                   jax.ShapeDtypeStruct((B, DN_HEADS, DN_DK, DN_DV), F32)],
        scratch_shapes=[pltpu.VMEM((bg, DN_HEADS, DN_DK, DN_DV), F32)],
        compiler_params=_cparams(("arbitrary", "arbitrary")),
        name="dn_scan",
    )(w, qe, u, kd, attn, egl, z_a, s0, onorm)
    conv_new = qkv[:, T - (CONV_W - 1):, :].astype(F32)
    return u_a, s_new, conv_new


def _head_rms(x, gain_row, width):
    outs = []
    for h in range(x.shape[1] // width):
        xh = x[:, h * width:(h + 1) * width]
        outs.append(xh * lax.rsqrt(jnp.mean(xh * xh, axis=-1, keepdims=True) + EPS) * gain_row)
    return outs


def _rope_inv_freq(shape):
    half = QK_ROPE // 2
    fidx = (lax.broadcasted_iota(jnp.int32, shape, 1) % half).astype(F32)
    return jnp.exp(fidx * (-math.log(ROPE_THETA) / half))


def _chunk_mask(qpos0, kpos0, tq, tk):
    qc = (lax.broadcasted_iota(jnp.int32, (tq, tk), 0) + qpos0) // CHUNK
    kc = (lax.broadcasted_iota(jnp.int32, (tq, tk), 1) + kpos0) // CHUNK
    return kc <= qc


def _attn_prompt_kernel(q_ref, k_ref, v_ref, o_ref, m_ref, l_ref, acc_ref, *, tq, tk, sub):
    i = pl.program_id(2)
    nsub = tq // sub
    ratio = tq // tk
    m_ref[...] = jnp.full_like(m_ref, -1e30)
    l_ref[...] = jnp.zeros_like(l_ref)
    acc_ref[...] = jnp.zeros_like(acc_ref)

    def scores(r, k):
        return _dot_nt(q_ref[0, r * sub:(r + 1) * sub, :], k)

    def softmax_pv(r, sr, v, mask):
        rows = slice(r * sub, (r + 1) * sub)
        if mask is not None:
            sr = jnp.where(mask, sr, -1e30)
        m_old = m_ref[rows, :]
        m_new = jnp.maximum(m_old, jnp.max(sr, axis=-1, keepdims=True))
        alpha = jnp.exp2(m_old - m_new)
        p = jnp.exp2(sr - jnp.tile(m_new, (1, tk // LANES)))
        psum = p[:, 0:LANES]
        for c in range(1, tk // LANES):
            psum = psum + p[:, c * LANES:(c + 1) * LANES]
        l_ref[rows, :] = alpha * l_ref[rows, :] + psum
        acc_ref[rows, :] = alpha * acc_ref[rows, :] + _dot(p.astype(BF16), v)
        m_ref[rows, :] = m_new

    def run(j0, items):
        kv = {}
        for d in sorted({d for d, _, _ in items}):
            start = pl.multiple_of((j0 + d) * tk, tk)
            kv[d] = (k_ref[0, pl.ds(start, tk), :], v_ref[0, pl.ds(start, tk), :])
        s = {n: scores(items[n][1], kv[items[n][0]][0]) for n in range(min(ATTN_LOOKAHEAD, len(items)))}
        for n, (d, r, mask) in enumerate(items):
            ahead = n + ATTN_LOOKAHEAD
            if ahead < len(items):
                s[ahead] = scores(items[ahead][1], kv[items[ahead][0]][0])
            softmax_pv(r, s.pop(n), kv[d][1], mask)

    per_trip = math.gcd(ratio, ATTN_TILES_PER_TRIP)
    full = [(d, r, None) for d in range(per_trip) for r in range(nsub)]

    def body(jj, carry):
        run(jj * per_trip, full)
        return carry

    lax.fori_loop(0, i * (ratio // per_trip), body, 0)
    diag = []
    for d in range(ratio):
        for r in range(nsub):
            q_lo, q_hi = (r * sub) // CHUNK, (r * sub + sub - 1) // CHUNK
            k_lo, k_hi = (d * tk) // CHUNK, (d * tk + tk - 1) // CHUNK
            if k_lo > q_hi:
                continue
            diag.append((d, r, None if k_hi <= q_lo else _chunk_mask(r * sub, d * tk, sub, tk)))
    run(i * ratio, diag)
    l = jnp.sum(l_ref[...], axis=-1, keepdims=True)
    o_ref[0] = (acc_ref[...] / l).astype(BF16)


def _attn_prompt(qcat, kcat, v, tq, tk):
    B, T, _ = v.shape
    return pl.pallas_call(
        functools.partial(_attn_prompt_kernel, tq=tq, tk=tk, sub=ATTN_SUB),
        grid=(B, MLA_HEADS, T // tq),
        in_specs=[pl.BlockSpec((1, tq, QK_CAT), lambda b, h, i: (b, i, h)),
                  pl.BlockSpec((1, T, QK_CAT), lambda b, h, i: (b, 0, h)),
                  pl.BlockSpec((1, T, V_HEAD), lambda b, h, i: (b, 0, h))],
        out_specs=pl.BlockSpec((1, tq, V_HEAD), lambda b, h, i: (b, i, h)),
        out_shape=jax.ShapeDtypeStruct((B, T, MLA_V), BF16),
        scratch_shapes=[pltpu.VMEM((tq, LANES), F32), pltpu.VMEM((tq, LANES), F32),
                        pltpu.VMEM((tq, V_HEAD), F32)],
        compiler_params=_cparams(("arbitrary", "arbitrary", "arbitrary")),
        name="attn_prompt",
    )(qcat, kcat, v)


def _attn_sample_kernel(q_ref, kn_ref, vn_ref, ckv_ref, kr_ref, kng_ref, wuk_ref, wuv_ref, o_ref,
                        m_ref, l_ref, acc_ref, *, tk):
    T = q_ref.shape[1]
    P = ckv_ref.shape[1]
    R = MLA_HEADS * T
    q = q_ref[0]
    row_head = lax.broadcasted_iota(jnp.int32, (R, 1), 0) // T
    q_chunk = (lax.broadcasted_iota(jnp.int32, (R, 1), 0) % T + P) // CHUNK

    def stacked(x, width):
        lane_head = lax.broadcasted_iota(jnp.int32, (R, x.shape[1]), 1) // width
        return jnp.where(lane_head == row_head, jnp.concatenate([x] * MLA_HEADS, axis=0), jnp.zeros((), x.dtype))

    q_nope = stacked(jnp.concatenate([q[:, h * QK_CAT:h * QK_CAT + QK_NOPE] for h in range(MLA_HEADS)], axis=1),
                     QK_NOPE)
    q_rope = jnp.concatenate([q[:, h * QK_CAT + QK_NOPE:h * QK_CAT + QK_HEAD] for h in range(MLA_HEADS)],
                             axis=0)

    def update(s, k_chunk):
        s = jnp.where(k_chunk <= q_chunk, s, -1e30)
        m_old = m_ref[...]
        m_new = jnp.maximum(m_old, jnp.max(s, axis=-1, keepdims=True))
        alpha = jnp.exp2(m_old - m_new)
        p = jnp.exp2(s - m_new)
        l_ref[...] = alpha * l_ref[...] + jnp.sum(p, axis=-1, keepdims=True)
        m_ref[...] = m_new
        return alpha, p.astype(BF16)

    m_ref[...] = jnp.full_like(m_ref, -1e30)
    l_ref[...] = jnp.zeros_like(l_ref)
    acc_ref[...] = jnp.zeros_like(acc_ref)

    def body(c, carry):
        start = pl.multiple_of(c * tk, tk)
        ckv16 = ckv_ref[0, pl.ds(start, tk), :].astype(BF16)
        kn = _head_rms(_dot(ckv16, wuk_ref[...]), kng_ref[...], QK_NOPE)
        kn16 = jnp.concatenate([x.astype(BF16) for x in kn], axis=1)
        s = _dot_nt(q_nope, kn16) + _dot_nt(q_rope, kr_ref[0, pl.ds(start, tk), :].astype(BF16))
        k_chunk = (lax.broadcasted_iota(jnp.int32, (1, tk), 1) + start) // CHUNK
        alpha, p16 = update(s, k_chunk)
        acc_ref[...] = alpha * acc_ref[...] + _dot(p16, ckv16)
        return carry

    lax.fori_loop(0, P // tk, body, 0)

    s_new = _dot_nt(stacked(q, QK_CAT), kn_ref[0])
    alpha, p16 = update(s_new, (lax.broadcasted_iota(jnp.int32, (1, T), 1) + P) // CHUNK)
    pc16 = (alpha * acc_ref[...]).astype(BF16)
    inv_l = 1.0 / l_ref[...]
    for h in range(MLA_HEADS):
        rows = slice(h * T, (h + 1) * T)
        lanes = slice(h * V_HEAD, (h + 1) * V_HEAD)
        o = _dot(pc16[rows], wuv_ref[:, lanes]) + _dot(p16[rows], vn_ref[0, :, lanes])
        o_ref[0, :, lanes] = (o * inv_l[rows]).astype(BF16)


def _attn_sample(qcat, kcat_new, v_new, past_ckv, past_kr, kn_gain, w_uk16, w_uv16, tk):
    B, T, _ = v_new.shape
    P = past_ckv.shape[1]
    R = MLA_HEADS * T
    perb = lambda b: (b, 0, 0)
    return pl.pallas_call(
        functools.partial(_attn_sample_kernel, tk=tk),
        grid=(B,),
        in_specs=[pl.BlockSpec((1, T, MLA_HEADS * QK_CAT), perb),
                  pl.BlockSpec((1, T, MLA_HEADS * QK_CAT), perb),
                  pl.BlockSpec((1, T, MLA_V), perb),
                  pl.BlockSpec((1, P, KV_RANK), perb),
                  pl.BlockSpec((1, P, QK_ROPE), perb),
                  _resident((1, QK_NOPE)),
                  _resident((KV_RANK, MLA_HEADS * QK_NOPE)),
                  _resident((KV_RANK, MLA_V))],
        out_specs=pl.BlockSpec((1, T, MLA_V), perb),
        out_shape=jax.ShapeDtypeStruct((B, T, MLA_V), BF16),
        scratch_shapes=[pltpu.VMEM((R, 1), F32), pltpu.VMEM((R, 1), F32), pltpu.VMEM((R, KV_RANK), F32)],
        compiler_params=_cparams(("arbitrary",)),
        name="attn_sample",
    )(qcat, kcat_new, v_new, past_ckv, past_kr, kn_gain, w_uk16, w_uv16)


def _out_kernel(x_ref, mod_ref, ua_ref, ob_ref, zb_ref, ga_ref, gb_ref, wdn_ref, wmla_ref, wout_ref, y_ref):
    bb, tm, d = x_ref.shape
    rows = bb * tm
    zb = zb_ref[...].astype(F32)
    ub = (ob_ref[...].astype(F32) * (zb * jax.nn.sigmoid(zb))).astype(BF16).reshape(rows, d)
    ya = _dot(ua_ref[...].reshape(rows, d), wdn_ref[...])
    yb = _dot(ub, wmla_ref[...])
    ga = jax.nn.sigmoid(ga_ref[...].astype(F32)).reshape(rows, d)
    gb = jax.nn.sigmoid(gb_ref[...].astype(F32)).reshape(rows, d)
    merged = (ga * ya + gb * yb).astype(BF16)
    out = _dot(merged, wout_ref[...]).reshape(bb, tm, d)
    gate = mod_ref[:, :, 2 * d:3 * d]
    y_ref[...] = x_ref[...] + gate * out


def _out_proj(x, mod3, u_a, o_b, z_b, g_a, g_b, w_dn16, w_mla16, w_out16, bb, tm):
    B, T, _ = x.shape
    row = lambda b, t: (b, t, 0)
    act = pl.BlockSpec((bb, tm, D_MODEL), row)
    return pl.pallas_call(
        _out_kernel,
        grid=(B // bb, T // tm),
        in_specs=[act, pl.BlockSpec((bb, 1, 3 * D_MODEL), lambda b, t: (b, 0, 0)), act, act, act, act, act,
                  _resident((D_MODEL, D_MODEL)), _resident((D_MODEL, D_MODEL)), _resident((D_MODEL, D_MODEL))],
        out_specs=act,
        out_shape=jax.ShapeDtypeStruct((B, T, D_MODEL), F32),
        compiler_params=_cparams(("arbitrary", "arbitrary")),
        name="out_proj",
    )(x, mod3, u_a, o_b, z_b, g_a, g_b, w_dn16, w_mla16, w_out16)


def _lane_vec(v, off):
    return jnp.zeros((1, LANES), F32).at[0, off:off + v.shape[0]].set(v)


def _tiles(B, T, cached):
    if cached:
        whole = (B, T)
        return dict(proj=whole, out=whole, dn=dict(tm=T, bg=2, G=1), attn_tk=1024)
    return dict(proj=(1, 256), out=(1, 1024), dn=dict(tm=256, bg=B, G=8), attn=(2048, 512))


def _layer(x, mod, conv_state, s0, past, prm, q_off):
    B, T, _ = x.shape
    tiles = _tiles(B, T, past is not None)
    mod3 = mod.reshape(B, 1, 3 * D_MODEL)
    mla = (prm["q_nope_norm"], prm["qr_gain"], prm["kv_norm"], prm["kr_gain"], prm["k_nope_norm"],
           prm["w_uk"], prm["w_uv"])
    small, qkv, z_a, z_b, g_a, g_b, qcat, kcat, v, ckv_new, kr_new = _in_proj(
        x, mod3, prm["norm_gain"], prm["w_pack"], mla, *tiles["proj"], q_off)
    u_a, s_new, conv_new = _deltanet(qkv, small, z_a, conv_state, s0, prm["w_conv"], prm["alog_v"],
                                     prm["dtb_v"], prm["dn_out_norm"], **tiles["dn"])
    if past is None:
        o_b = _attn_prompt(qcat, kcat, v, *tiles["attn"])
    else:
        past_ckv, past_kr = past
        o_b = _attn_sample(qcat, kcat, v, past_ckv, past_kr, prm["k_nope_norm"], prm["w_uk"], prm["w_uv"],
                           tiles["attn_tk"])
    y = _out_proj(x, mod3, u_a, o_b, z_b, g_a, g_b, prm["w_o_dn"], prm["w_o_mla"], prm["w_out"], *tiles["out"])
    return y, conv_new, s_new, ckv_new, kr_new


def kernel(x_prompt, x_sample, c_prompt, c_sample, cache_ckv, cache_krope, state_delta, state_conv, norm_gain, w_ada, b_ada, w_in, w_conv, a_log, dt_bias, dn_out_norm, q_nope_norm, q_rope_norm, k_nope_norm, k_rope_norm, kv_norm, w_uk, w_uv, w_o_dn, w_o_mla, w_out):
    depth = w_in.shape[0]
    assert depth == 1, "single-layer configuration"
    l = 0
    B, T, _ = x_prompt.shape
    Bs, Ts, _ = x_sample.shape
    past_len = cache_ckv.shape[2]

    row = lambda v: v.reshape(1, -1).astype(F32)
    prm = dict(
        norm_gain=row(norm_gain[l]),
        w_pack=_pack_w_in(w_in[l]),
        w_conv=w_conv[l],
        alog_v=_lane_vec(a_log[l], ALPHA_OFF),
        dtb_v=_lane_vec(dt_bias[l], ALPHA_OFF),
        dn_out_norm=row(dn_out_norm[l]),
        q_nope_norm=row(q_nope_norm[l]),
        qr_gain=jnp.tile(row(q_rope_norm[l]), (1, LANES // QK_ROPE)),
        kv_norm=row(kv_norm[l]),
        kr_gain=_lane_vec(k_rope_norm[l], KR_OFF),
        k_nope_norm=row(k_nope_norm[l]),
        w_uk=w_uk[l].astype(BF16),
        w_uv=w_uv[l].astype(BF16),
        w_o_dn=w_o_dn[l].astype(BF16),
        w_o_mla=w_o_mla[l].astype(BF16),
        w_out=w_out[l].astype(BF16),
    )

    rows = B + Bs
    rows_pad = -(-rows // 8) * 8
    c_all = jnp.concatenate([c_prompt, c_sample, jnp.zeros((rows_pad - rows, D_MODEL), F32)], axis=0)
    mod = _ada(c_all, w_ada[l], b_ada[l].reshape(1, -1))

    zeros_conv = jnp.zeros((B, CONV_W - 1, DN_CONV_CH), F32)
    zeros_state = jnp.zeros((B, DN_HEADS, DN_DK, DN_DV), F32)
    yp, cvp, sdp, kvp, krp = _layer(x_prompt, mod[:B], zeros_conv, zeros_state, None, prm, q_off=0)
    ys, cvs, sds, kvs, krs = _layer(x_sample, mod[B:rows], state_conv[l], state_delta[l],
                                    (cache_ckv[l], cache_krope[l]), prm, q_off=past_len)
    st = lambda a: a[None]
    return (yp, ys, st(kvp), st(krp), st(sdp), st(cvp), st(kvs), st(krs), st(sds), st(cvs))
```
